```python
import math
import jax, jax.numpy as jnp
from jax import lax
import numpy as np

D_MODEL = 1024
BATCH = 32
SEQ = 256
DEPTH = 4
DEC_BATCH = 4
DEC_SEQ = 4096
PAST_LEN = 512

GRID_W = 64
POS_BASE = 10000.0
EPS = 1e-6
N_MOD = 9
D_FF = 2816
N_HEADS_A = 16
HEAD_DIM_A = 64
D_INNER_A = N_HEADS_A * HEAD_DIM_A
N_GROUPS_A = 2
D_STATE = 128
CONV_A = 5
SSD_CHUNK = 128
D_XBC = D_INNER_A + 2 * N_GROUPS_A * D_STATE
POOL_WINDOWS = (2, 4, 8, 16)
N_POOL_GROUPS = len(POOL_WINDOWS)
POOL_GROUP_DIM = 128
D_POOL = N_POOL_GROUPS * POOL_GROUP_DIM
D_IN_EVEN = D_INNER_A + D_XBC + 2 * N_HEADS_A + D_POOL
D_OUT_EVEN = D_INNER_A + D_POOL
N_HEADS_C = 8
MLP_CHUNK = 128
D_C = 1024
HEAD_DIM_C = D_C // N_HEADS_C
D_D = 1024
CONV_D = 31
D_IN_ODD = 2 * D_C + 2 * D_D
D_OUT_ODD = D_C + D_D
N_EVEN = (DEPTH + 1) // 2
N_ODD = DEPTH // 2

kernel_name = 'hybrid_ssd_pool_gmlp_conformer_diffusion_step'


def rms_norm(x, g):
    xf = x.astype(jnp.float32)
    y = xf * lax.rsqrt(jnp.mean(xf * xf, axis=-1, keepdims=True) + EPS)
    return (y * g.astype(jnp.float32)).astype(x.dtype)


def layer_norm(x, g, b):
    xf = x.astype(jnp.float32)
    xc = xf - jnp.mean(xf, axis=-1, keepdims=True)
    y = xc * lax.rsqrt(jnp.mean(xc * xc, axis=-1, keepdims=True) + EPS)
    return (y * g.astype(jnp.float32) + b.astype(jnp.float32)).astype(x.dtype)


def modulate(h, shift, scale):
    return h * (1.0 + scale[:, None, :]) + shift[:, None, :]


def swiglu(h, w_gate, w_up, w_down):
    return (jax.nn.silu(h @ w_gate) * (h @ w_up)) @ w_down


def depthwise_conv_centred(x, w, b):
    k = w.shape[0]
    y = lax.conv_general_dilated(x, w.astype(x.dtype)[:, None, :], window_strides=(1,),
                                 padding=[(k // 2, k // 2)],
                                 dimension_numbers=('NWC', 'WIO', 'NWC'),
                                 feature_group_count=x.shape[-1])
    return y + b


def grid_position_code(n_tokens, dtype):
    rows = n_tokens // GRID_W
    quarter = D_MODEL // 4
    freqs = jnp.exp(-math.log(POS_BASE) * jnp.arange(quarter, dtype=jnp.float32) / quarter)
    row = jnp.broadcast_to(jnp.arange(rows, dtype=jnp.float32)[:, None], (rows, GRID_W)).reshape(-1)
    col = jnp.broadcast_to(jnp.arange(GRID_W, dtype=jnp.float32)[None, :], (rows, GRID_W)).reshape(-1)
    ang_r = row[:, None] * freqs
    ang_c = col[:, None] * freqs
    return jnp.concatenate([jnp.sin(ang_r), jnp.cos(ang_r), jnp.sin(ang_c), jnp.cos(ang_c)],
                           axis=-1).astype(dtype)


def ssd_chunked(x, dt, a_neg, bmat, cmat, h0):
    b, L, H, P = x.shape
    G, N = bmat.shape[2], bmat.shape[3]
    E = H // G
    Q = SSD_CHUNK
    nc = L // Q
    f32 = jnp.float32
    dt = dt.astype(f32)
    xd = (x.astype(f32) * dt[..., None]).reshape(b, nc, Q, G, E, P)
    a_cs = jnp.cumsum((dt * a_neg).reshape(b, nc, Q, G, E), axis=2)
    bm = bmat.astype(f32).reshape(b, nc, Q, G, N)
    cm = cmat.astype(f32).reshape(b, nc, Q, G, N)
    seg = a_cs[:, :, :, None] - a_cs[:, :, None, :]
    lower = jnp.tril(jnp.ones((Q, Q), dtype=bool))[None, None, :, :, None, None]
    lmat = jnp.exp(jnp.where(lower, seg, -jnp.inf))
    scores = jnp.einsum('bclgn,bcsgn->bclsg', cm, bm)
    y_diag = jnp.einsum('bclsge,bcsgep->bclgep', scores[..., None] * lmat, xd)
    decay_to_end = jnp.exp(a_cs[:, :, -1:] - a_cs)
    states = jnp.einsum('bcsgn,bcsgep->bcgepn', bm, xd * decay_to_end[..., None])
    chunk_decay = jnp.exp(a_cs[:, :, -1])

    def step(h, inp):
        dec, st = inp
        return h * dec[..., None, None] + st, h

    h_last, h_prev = lax.scan(step, h0.astype(f32).reshape(b, G, E, P, N),
                              (jnp.moveaxis(chunk_decay, 1, 0), jnp.moveaxis(states, 1, 0)))
    h_prev = jnp.moveaxis(h_prev, 0, 1)
    y_off = jnp.einsum('bclgn,bcgepn->bclgep', cm, h_prev) * jnp.exp(a_cs)[..., None]
    y = (y_diag + y_off).reshape(b, L, H, P)
    return y, h_last.reshape(b, H, P, N)


def multiscale_pool(u, pool_w, pool_scale):
    b, L, _ = u.shape
    ug = u.astype(jnp.float32).reshape(b, L, N_POOL_GROUPS, POOL_GROUP_DIM)
    cs = jnp.concatenate([jnp.zeros((b, 1, N_POOL_GROUPS, POOL_GROUP_DIM), jnp.float32),
                          jnp.cumsum(ug, axis=1)], axis=1)
    t = jnp.arange(L)
    outs = []
    for gi, win in enumerate(POOL_WINDOWS):
        lo = jnp.clip(t - win // 2, 0, L)
        hi = jnp.clip(t - win // 2 + win, 0, L)
        cs_g = cs[:, :, gi]
        mean = (cs_g[:, hi] - cs_g[:, lo]) / (hi - lo).astype(jnp.float32)[None, :, None]
        outs.append(mean - ug[:, :, gi])
    pooled = jnp.stack(outs, axis=2)
    mixed = jnp.einsum('blgc,gcd->blgd', pooled, pool_w.astype(jnp.float32)).reshape(b, L, D_POOL)
    return (mixed * pool_scale.astype(jnp.float32)).astype(u.dtype)


def ssd_pool_mixer(h, h0, w_in, conv_w, conv_b, dt_bias, a_log, d_skip, ssd_norm_g,
                   pool_w, pool_scale, w_out):
    b, L, _ = h.shape
    z, xbc, dt_raw, pool_in = jnp.split(
        h @ w_in, [D_INNER_A, D_INNER_A + D_XBC, D_INNER_A + D_XBC + 2 * N_HEADS_A], axis=-1)
    xbc = jax.nn.silu(depthwise_conv_centred(xbc, conv_w, conv_b))
    xs, bm, cm = jnp.split(xbc, [D_INNER_A, D_INNER_A + N_GROUPS_A * D_STATE], axis=-1)
    xs = xs.reshape(b, L, N_HEADS_A, HEAD_DIM_A)
    bm = bm.reshape(b, L, N_GROUPS_A, D_STATE)
    cm = cm.reshape(b, L, N_GROUPS_A, D_STATE)
    dt = jax.nn.softplus(dt_raw.reshape(b, L, 2, N_HEADS_A).astype(jnp.float32)
                         + dt_bias.astype(jnp.float32))
    a_neg = -jnp.exp(a_log.astype(jnp.float32))
    y_f, hf = ssd_chunked(xs, dt[:, :, 0], a_neg[0], bm, cm, h0[:, 0])
    flip = lambda t: jnp.flip(t, axis=1)
    y_b, hb = ssd_chunked(flip(xs), flip(dt[:, :, 1]), a_neg[1], flip(bm), flip(cm), h0[:, 1])
    y = (y_f + flip(y_b)).astype(h.dtype) + d_skip[:, None] * xs
    y_a = rms_norm(y.reshape(b, L, D_INNER_A) * jax.nn.silu(z), ssd_norm_g)
    y_p = multiscale_pool(pool_in, pool_w, pool_scale)
    out = jnp.concatenate([y_a, y_p], axis=-1) @ w_out
    return out, jnp.stack([hf, hb], axis=1)


def gmlp_conv_mixer(h, w_in, v_ln_g, v_ln_b, sp_w, sp_b, dw_w, dw_b, cn_g, cn_b, w_out):
    b, L, _ = h.shape
    u, v, ga, gg = jnp.split(h @ w_in, [D_C, 2 * D_C, 2 * D_C + D_D], axis=-1)
    u = jax.nn.gelu(u)
    v = layer_norm(jax.nn.gelu(v), v_ln_g, v_ln_b)
    vc = v.reshape(b, L // MLP_CHUNK, MLP_CHUNK, N_HEADS_C, HEAD_DIM_C)
    sv = jnp.einsum('hts,bcshd->bcthd', sp_w, vc) + sp_b.T[None, None, :, :, None]
    y_c = u * sv.reshape(b, L, D_C)
    glu = ga * jax.nn.sigmoid(gg)
    y_d = jax.nn.silu(layer_norm(depthwise_conv_centred(glu, dw_w, dw_b), cn_g, cn_b))
    return jnp.concatenate([y_c, y_d], axis=-1) @ w_out


def setup_inputs(seed: int = 0) -> dict:
    key = jax.random.key(seed)
    k = jax.random.split(key, 32)
    f32 = jnp.float32

    def nrm(i, shape, scale=1.0):
        return jax.random.normal(k[i], shape, f32) * scale

    def near_one(i, shape):
        return 1.0 + 0.02 * jax.random.normal(k[i], shape, f32)

    dt0 = jnp.exp(jax.random.uniform(k[14], (N_EVEN, 2, N_HEADS_A), f32,
                                     math.log(1e-3), math.log(1e-1)))
    return {
        'x_prompt': nrm(0, (BATCH, SEQ, D_MODEL)),
        'x_sample': nrm(1, (DEC_BATCH, DEC_SEQ, D_MODEL)),
        'c': nrm(2, (DEC_BATCH, D_MODEL)),
        'state_ssd': nrm(3, (DEC_BATCH, N_EVEN, 2, N_HEADS_A, HEAD_DIM_A, D_STATE), 0.5),
        'c_ctx': nrm(4, (D_MODEL,)),
        'w_mod': nrm(5, (DEPTH, D_MODEL, N_MOD * D_MODEL), 0.5 * D_MODEL ** -0.5),
        'b_mod': nrm(6, (DEPTH, N_MOD * D_MODEL), 0.02),
        'norm_g': near_one(7, (DEPTH, 3, D_MODEL)),
        'ffn_w_gate': nrm(8, (DEPTH, 2, D_MODEL, D_FF), D_MODEL ** -0.5),
        'ffn_w_up': nrm(9, (DEPTH, 2, D_MODEL, D_FF), D_MODEL ** -0.5),
        'ffn_w_down': nrm(10, (DEPTH, 2, D_FF, D_MODEL), D_FF ** -0.5),
        'ev_w_in': nrm(11, (N_EVEN, D_MODEL, D_IN_EVEN), D_MODEL ** -0.5),
        'ev_conv_w': nrm(12, (N_EVEN, CONV_A, D_XBC), CONV_A ** -0.5),
        'ev_conv_b': nrm(13, (N_EVEN, D_XBC), 0.02),
        'ev_dt_bias': dt0 + jnp.log(-jnp.expm1(-dt0)),
        'ev_a_log': jnp.log(jax.random.uniform(k[15], (N_EVEN, 2, N_HEADS_A), f32, 1.0, 16.0)),
        'ev_d_skip': near_one(16, (N_EVEN, N_HEADS_A)),
        'ev_ssd_norm_g': near_one(17, (N_EVEN, D_INNER_A)),
        'ev_pool_w': nrm(18, (N_EVEN, N_POOL_GROUPS, POOL_GROUP_DIM, POOL_GROUP_DIM), POOL_GROUP_DIM ** -0.5),
        'ev_pool_scale': near_one(19, (N_EVEN, D_POOL)),
        'ev_w_out': nrm(20, (N_EVEN, D_OUT_EVEN, D_MODEL), D_OUT_EVEN ** -0.5),
        'od_w_in': nrm(21, (N_ODD, D_MODEL, D_IN_ODD), D_MODEL ** -0.5),
        'od_v_ln_g': near_one(22, (N_ODD, D_C)),
        'od_v_ln_b': nrm(23, (N_ODD, D_C), 0.02),
        'od_sp_w': nrm(24, (N_ODD, N_HEADS_C, MLP_CHUNK, MLP_CHUNK), MLP_CHUNK ** -0.5),
        'od_sp_b': near_one(25, (N_ODD, N_HEADS_C, MLP_CHUNK)),
        'od_dw_w': nrm(26, (N_ODD, CONV_D, D_D), CONV_D ** -0.5),
        'od_dw_b': nrm(27, (N_ODD, D_D), 0.02),
        'od_cn_g': near_one(28, (N_ODD, D_D)),
        'od_cn_b': nrm(29, (N_ODD, D_D), 0.02),
        'od_w_out': nrm(30, (N_ODD, D_OUT_ODD, D_MODEL), D_OUT_ODD ** -0.5),
        'final_norm_g': near_one(31, (D_MODEL,)),
    }


def reference(x_prompt, x_sample, c, state_ssd, c_ctx, w_mod, b_mod, norm_g,
              ffn_w_gate, ffn_w_up, ffn_w_down, ev_w_in, ev_conv_w, ev_conv_b,
              ev_dt_bias, ev_a_log, ev_d_skip, ev_ssd_norm_g, ev_pool_w, ev_pool_scale,
              ev_w_out, od_w_in, od_v_ln_g, od_v_ln_b, od_sp_w, od_sp_b, od_dw_w, od_dw_b,
              od_cn_g, od_cn_b, od_w_out, final_norm_g):

    def trunk(x, cond, ssd_init):
        sc = jax.nn.silu(cond)
        finals = []
        for i in range(DEPTH):
            m = (sc @ w_mod[i] + b_mod[i]).reshape(cond.shape[0], N_MOD, D_MODEL)
            hn = modulate(rms_norm(x, norm_g[i, 0]), m[:, 0], m[:, 1])
            x = x + 0.5 * m[:, 2][:, None] * swiglu(hn, ffn_w_gate[i, 0], ffn_w_up[i, 0], ffn_w_down[i, 0])
            hn = modulate(rms_norm(x, norm_g[i, 1]), m[:, 3], m[:, 4])
            j = i // 2
            if i % 2 == 0:
                mix, fin = ssd_pool_mixer(hn, ssd_init[:, j], ev_w_in[j], ev_conv_w[j], ev_conv_b[j],
                                          ev_dt_bias[j], ev_a_log[j], ev_d_skip[j], ev_ssd_norm_g[j],
                                          ev_pool_w[j], ev_pool_scale[j], ev_w_out[j])
                finals.append(fin)
            else:
                mix = gmlp_conv_mixer(hn, od_w_in[j], od_v_ln_g[j], od_v_ln_b[j], od_sp_w[j], od_sp_b[j],
                                      od_dw_w[j], od_dw_b[j], od_cn_g[j], od_cn_b[j], od_w_out[j])
            x = x + m[:, 5][:, None] * mix
            hn = modulate(rms_norm(x, norm_g[i, 2]), m[:, 6], m[:, 7])
            x = x + 0.5 * m[:, 8][:, None] * swiglu(hn, ffn_w_gate[i, 1], ffn_w_up[i, 1], ffn_w_down[i, 1])
        return rms_norm(x, final_norm_g), jnp.stack(finals, axis=1)

    b_ctx = x_prompt.shape[0]
    zero_state = jnp.zeros((b_ctx, N_EVEN, 2, N_HEADS_A, HEAD_DIM_A, D_STATE), x_prompt.dtype)
    y_prompt, ctx_states = trunk(x_prompt, c_ctx[None, :], zero_state)
    new_state_ssd = ctx_states.astype(x_prompt.dtype)

    x_lat = x_sample + grid_position_code(x_sample.shape[1], x_sample.dtype)[None]
    y_sample, _ = trunk(x_lat, c, state_ssd)

    return (y_prompt, y_sample, new_state_ssd)
```

```python
import collections
import functools
import math

import jax
import jax.numpy as jnp
from jax import lax
from jax.experimental import pallas as pl
from jax.experimental.pallas import tpu as pltpu

F32 = jnp.float32
BF16 = jnp.bfloat16

D_MODEL = 1024
DEPTH = 4
GRID_W = 64
POS_BASE = 10000.0
EPS = 1e-6
N_MOD = 9
D_FF = 2816
N_HEADS_A = 16
HEAD_DIM_A = 64
D_INNER_A = N_HEADS_A * HEAD_DIM_A
N_GROUPS_A = 2
HEADS_PER_GROUP = N_HEADS_A // N_GROUPS_A
D_STATE = 128
CONV_A = 5
SSD_CHUNK = 128
D_BC = N_GROUPS_A * D_STATE
D_XBC = D_INNER_A + 2 * D_BC
POOL_WINDOWS = (2, 4, 8, 16)
POOL_GROUP_DIM = 128
D_POOL = len(POOL_WINDOWS) * POOL_GROUP_DIM
N_HEADS_C = 8
MLP_CHUNK = 128
D_C = 1024
D_D = 1024
CONV_D = 31
N_EVEN = (DEPTH + 1) // 2

LANES = 128
SUBLANES = 8
DT_PAD = LANES
HALO_A = SUBLANES
HALO_D = 2 * SUBLANES
VMEM_LIMIT = 56 * 1024 * 1024

TM_FFN = 1024
TF_FFN = 256
TM_PROJ = 512
TM_MIX = 256

Cfg = collections.namedtuple("Cfg", "batch seq dec_batch dec_seq")


def _n_tokens(cfg):
    return cfg.batch * cfg.seq + cfg.dec_batch * cfg.dec_seq


def _tile_info(t, tm, cfg):
    npt = cfg.seq // tm
    nst = cfg.dec_seq // tm
    n_p = cfg.batch * npt
    is_prompt = t < n_p
    ts = jnp.maximum(t - n_p, 0)
    pos = jnp.where(is_prompt, lax.rem(t, npt), lax.rem(ts, nst))
    last = jnp.where(is_prompt, npt - 1, nst - 1)
    info = dict(
        is_prompt=is_prompt,
        is_start=pos == 0,
        is_end=pos == last,
        pos=pos,
        seq_len=jnp.where(is_prompt, cfg.seq, cfg.dec_seq),
        p_seq=jnp.minimum(lax.div(t, npt), cfg.batch - 1),
        s_seq=lax.div(ts, nst),
    )
    return info


def _mod_row(t, tm, cfg):
    n_p = (cfg.batch * cfg.seq) // tm
    per = cfg.dec_seq // tm
    return jnp.where(t < n_p, 0, 1 + lax.div(jnp.maximum(t - n_p, 0), per))


def _silu(x):
    return x * jax.nn.sigmoid(x)


def _softplus(x):
    return jnp.maximum(x, 0.0) + jnp.log1p(jnp.exp(-jnp.abs(x)))


def _rms(x, g):
    return x * lax.rsqrt(jnp.mean(x * x, axis=-1, keepdims=True) + EPS) * g


def _layer_norm(x, g, b):
    xc = x - jnp.mean(x, axis=-1, keepdims=True)
    y = xc * lax.rsqrt(jnp.mean(xc * xc, axis=-1, keepdims=True) + EPS)
    return y * g + b


def _dot(a, b):
    return jnp.dot(a, b, preferred_element_type=F32)


def _dot_nt(a, b):
    return lax.dot_general(a, b, (((1,), (1,)), ((), ())), preferred_element_type=F32)


def _params(sem):
    return pltpu.CompilerParams(dimension_semantics=sem, vmem_limit_bytes=VMEM_LIMIT)


def _mod_kernel(c_ref, w_ref, b_ref, o_ref):
    c = c_ref[...]
    sc = _silu(c).astype(BF16)
    o_ref[...] = _dot(sc, w_ref[...].astype(BF16)) + b_ref[...]


def _modulation(cond8, w_mod, b_mod):
    b4 = b_mod.reshape(DEPTH, N_MOD, 1, D_MODEL)
    return pl.pallas_call(
        _mod_kernel,
        grid=(DEPTH, N_MOD),
        in_specs=[
            pl.BlockSpec((SUBLANES, D_MODEL), lambda i, k: (0, 0)),
            pl.BlockSpec((None, D_MODEL, D_MODEL), lambda i, k: (i, 0, k)),
            pl.BlockSpec((None, None, 1, D_MODEL), lambda i, k: (i, k, 0, 0)),
        ],
        out_specs=pl.BlockSpec((None, None, SUBLANES, D_MODEL), lambda i, k: (i, k, 0, 0)),
        out_shape=jax.ShapeDtypeStruct((DEPTH, N_MOD, SUBLANES, D_MODEL), F32),
        compiler_params=_params(("arbitrary", "arbitrary")),
        name="modulation",
    )(cond8, w_mod, b4)


def _mod_spec(i, k, ngrid):
    if ngrid == 1:
        return pl.BlockSpec((None, None, SUBLANES, D_MODEL), lambda t: (i, k, 0, 0))
    return pl.BlockSpec((None, None, SUBLANES, D_MODEL), lambda t, f: (i, k, 0, 0))


def _ffn_kernel(x_ref, sh_ref, sc_ref, gt_ref, g_ref, wg_ref, wu_ref, wd_ref, *rest, cfg, nf, final):
    if final:
        fg_ref, o_ref, hn_ref, acc_ref = rest
    else:
        o_ref, hn_ref, acc_ref = rest
    f = pl.program_id(1)
    r = _mod_row(pl.program_id(0), TM_FFN, cfg)

    @pl.when(f == 0)
    def _():
        hn = _rms(x_ref[...], g_ref[...]) * (1.0 + sc_ref[pl.ds(r, 1), :]) + sh_ref[pl.ds(r, 1), :]
        hn_ref[...] = hn.astype(BF16)

    hn = hn_ref[...]
    g = _dot(hn, wg_ref[...])
    u = _dot(hn, wu_ref[...])
    a = (_silu(g) * u).astype(BF16)
    d = _dot(a, wd_ref[...])

    @pl.when(f == 0)
    def _():
        acc_ref[...] = d

    @pl.when(f > 0)
    def _():
        acc_ref[...] += d

    @pl.when(f == nf - 1)
    def _():
        xn = x_ref[...] + (0.5 * gt_ref[pl.ds(r, 1), :]) * acc_ref[...]
        if final:
            xn = _rms(xn, fg_ref[...])
        o_ref[...] = xn


def _ffn(x, mod, i, k0, g, wg, wu, wd, cfg, final_g=None):
    t_tok = x.shape[0]
    nf = D_FF // TF_FFN
    final = final_g is not None
    in_specs = [
        pl.BlockSpec((TM_FFN, D_MODEL), lambda t, f: (t, 0)),
        _mod_spec(i, k0, 2), _mod_spec(i, k0 + 1, 2), _mod_spec(i, k0 + 2, 2),
        pl.BlockSpec((1, D_MODEL), lambda t, f: (0, 0)),
        pl.BlockSpec((D_MODEL, TF_FFN), lambda t, f: (0, f)),
        pl.BlockSpec((D_MODEL, TF_FFN), lambda t, f: (0, f)),
        pl.BlockSpec((TF_FFN, D_MODEL), lambda t, f: (f, 0)),
    ]
    args = [x, mod, mod, mod, g.reshape(1, D_MODEL), wg, wu, wd]
    if final:
        in_specs.append(pl.BlockSpec((1, D_MODEL), lambda t, f: (0, 0)))
        args.append(final_g.reshape(1, D_MODEL))
    return pl.pallas_call(
        functools.partial(_ffn_kernel, cfg=cfg, nf=nf, final=final),
        grid=(t_tok // TM_FFN, nf),
        in_specs=in_specs,
        out_specs=pl.BlockSpec((TM_FFN, D_MODEL), lambda t, f: (t, 0)),
        out_shape=jax.ShapeDtypeStruct((t_tok, D_MODEL), F32),
        scratch_shapes=[pltpu.VMEM((TM_FFN, D_MODEL), BF16), pltpu.VMEM((TM_FFN, D_MODEL), F32)],
        compiler_params=_params(("parallel", "arbitrary")),
        name="ffn",
    )(*args)


def _norm_mod(x_ref, sh_ref, sc_ref, g_ref, cfg):
    r = _mod_row(pl.program_id(0), TM_PROJ, cfg)
    hn = _rms(x_ref[...], g_ref[...]) * (1.0 + sc_ref[pl.ds(r, 1), :]) + sh_ref[pl.ds(r, 1), :]
    return hn.astype(BF16)


def _even_in_kernel(x_ref, sh_ref, sc_ref, g_ref, w_ref, z_ref, xbc_ref, pool_ref, dt_ref, *, cfg):
    hn = _norm_mod(x_ref, sh_ref, sc_ref, g_ref, cfg)
    o = 0
    for ref, width in ((z_ref, D_INNER_A), (xbc_ref, D_XBC), (pool_ref, D_POOL), (dt_ref, DT_PAD)):
        ref[...] = _dot(hn, w_ref[:, o:o + width])
        o += width


def _even_in(x, mod, i, g, w, cfg):
    t_tok = x.shape[0]
    widths = (D_INNER_A, D_XBC, D_POOL, DT_PAD)
    n_tot = sum(widths)
    return pl.pallas_call(
        functools.partial(_even_in_kernel, cfg=cfg),
        grid=(t_tok // TM_PROJ,),
        in_specs=[
            pl.BlockSpec((TM_PROJ, D_MODEL), lambda t: (t, 0)),
            _mod_spec(i, 3, 1), _mod_spec(i, 4, 1),
            pl.BlockSpec((1, D_MODEL), lambda t: (0, 0)),
            pl.BlockSpec((D_MODEL, n_tot), lambda t: (0, 0)),
        ],
        out_specs=[pl.BlockSpec((TM_PROJ, wd), lambda t: (t, 0)) for wd in widths],
        out_shape=[jax.ShapeDtypeStruct((t_tok, wd), F32) for wd in widths],
        compiler_params=_params(("parallel",)),
        name="even_in",
    )(x, mod, mod, g.reshape(1, D_MODEL), w)


def _odd_in_kernel(x_ref, sh_ref, sc_ref, g_ref, w_ref, lg_ref, lb_ref, u_ref, v_ref, glu_ref, *, cfg):
    hn = _norm_mod(x_ref, sh_ref, sc_ref, g_ref, cfg)
    u_ref[...] = jax.nn.gelu(_dot(hn, w_ref[:, 0:D_C]))
    v = jax.nn.gelu(_dot(hn, w_ref[:, D_C:2 * D_C]))
    v_ref[...] = _layer_norm(v, lg_ref[...], lb_ref[...]).astype(BF16)
    ga = _dot(hn, w_ref[:, 2 * D_C:2 * D_C + D_D])
    gg = _dot(hn, w_ref[:, 2 * D_C + D_D:2 * D_C + 2 * D_D])
    glu_ref[...] = ga * jax.nn.sigmoid(gg)


def _odd_in(x, mod, i, g, w, ln_g, ln_b, cfg):
    t_tok = x.shape[0]
    row = lambda: pl.BlockSpec((1, D_MODEL), lambda t: (0, 0))
    return pl.pallas_call(
        functools.partial(_odd_in_kernel, cfg=cfg),
        grid=(t_tok // TM_PROJ,),
        in_specs=[
            pl.BlockSpec((TM_PROJ, D_MODEL), lambda t: (t, 0)),
            _mod_spec(i, 3, 1), _mod_spec(i, 4, 1),
            row(),
            pl.BlockSpec((D_MODEL, 2 * D_C + 2 * D_D), lambda t: (0, 0)),
            row(), row(),
        ],
        out_specs=[pl.BlockSpec((TM_PROJ, D_C), lambda t: (t, 0))] * 3,
        out_shape=[jax.ShapeDtypeStruct((t_tok, D_C), F32),
                   jax.ShapeDtypeStruct((t_tok, D_C), BF16),
                   jax.ShapeDtypeStruct((t_tok, D_D), F32)],
        compiler_params=_params(("parallel",)),
        name="odd_in",
    )(x, mod, mod, g.reshape(1, D_MODEL), w, ln_g.reshape(1, D_C), ln_b.reshape(1, D_C))


def _conv_silu(xm_ref, xp_ref, xn_ref, cw_ref, cb_ref, ext_ref, xbc_ref, is_start, is_end):
    q = SSD_CHUNK
    zero = jnp.zeros((HALO_A, D_XBC), F32)
    ext_ref[0:HALO_A, :] = jnp.where(is_start, zero, xp_ref[...])
    ext_ref[HALO_A:HALO_A + q, :] = xm_ref[...]
    ext_ref[HALO_A + q:, :] = jnp.where(is_end, zero, xn_ref[...])
    base = HALO_A - CONV_A // 2
    for j in range(D_XBC // LANES):
        sl = slice(j * LANES, (j + 1) * LANES)
        acc = cw_ref[0:1, sl] * ext_ref[base:base + q, sl]
        for k in range(1, CONV_A):
            acc = acc + cw_ref[k:k + 1, sl] * ext_ref[base + k:base + k + q, sl]
        acc = acc + cb_ref[:, sl]
        xbc_ref[:, sl] = _silu(acc)


def _transpose_xs(xbc_ref, xst_ref):
    for j in range(D_INNER_A // LANES):
        xst_ref[j * LANES:(j + 1) * LANES, :] = xbc_ref[:, j * LANES:(j + 1) * LANES].T


def _dt_terms(dt_ref, dtb_ref, alog_ref):
    dtt = dt_ref[...].T[0:2 * N_HEADS_A, :]
    dts = _softplus(dtt + dtb_ref[...])
    return dts, dts * (-jnp.exp(alog_ref[...]))


def _tri(lower_incl):
    ri = lax.broadcasted_iota(jnp.int32, (SSD_CHUNK, SSD_CHUNK), 0)
    ci = lax.broadcasted_iota(jnp.int32, (SSD_CHUNK, SSD_CHUNK), 1)
    return (ri >= ci) if lower_incl else (ri <= ci)


def _cumsum_lanes(v, mask):
    return jnp.dot(v, mask.astype(F32), precision=lax.Precision.HIGHEST, preferred_element_type=F32)


def _group_bc(xbc_ref, g):
    b = xbc_ref[:, D_INNER_A + g * D_STATE:D_INNER_A + (g + 1) * D_STATE].astype(BF16)
    c = xbc_ref[:, D_INNER_A + D_BC + g * D_STATE:D_INNER_A + D_BC + (g + 1) * D_STATE].astype(BF16)
    return b, c


def _state_update(s_ref, xst_ref, xd_ref, xbc_ref, scale, cdec):
    hd = HEAD_DIM_A
    for h in range(N_HEADS_A):
        xd_ref[h * hd:(h + 1) * hd, :] = (xst_ref[h * hd:(h + 1) * hd, :] * scale[h:h + 1, :]).astype(BF16)
    gw = HEADS_PER_GROUP * hd
    for g in range(N_GROUPS_A):
        b, _ = _group_bc(xbc_ref, g)
        upd = _dot(xd_ref[g * gw:(g + 1) * gw, :], b)
        for e in range(HEADS_PER_GROUP):
            h = g * HEADS_PER_GROUP + e
            s_ref[h * hd:(h + 1) * hd, :] = (s_ref[h * hd:(h + 1) * hd, :] * cdec[h:h + 1, :]
                                             + upd[e * hd:(e + 1) * hd, :])


def _ssd_fwd_kernel(xm_ref, xp_ref, xn_ref, dt_ref, cw_ref, cb_ref, dtb_ref, alog_ref, s0_ref,
                    ypt_ref, sfin_ref, ext_ref, xbc_ref, xst_ref, xd_ref, s_ref, *, cfg):
    q, hd, nh = SSD_CHUNK, HEAD_DIM_A, N_HEADS_A
    info = _tile_info(pl.program_id(0), q, cfg)

    @pl.when(jnp.logical_and(info["is_start"], info["is_prompt"]))
    def _():
        s_ref[...] = jnp.zeros_like(s_ref)

    @pl.when(jnp.logical_and(info["is_start"], jnp.logical_not(info["is_prompt"])))
    def _():
        s_ref[...] = s0_ref[...]

    _conv_silu(xm_ref, xp_ref, xn_ref, cw_ref, cb_ref, ext_ref, xbc_ref, info["is_start"], info["is_end"])
    _transpose_xs(xbc_ref, xst_ref)

    dts, dta = _dt_terms(dt_ref, dtb_ref, alog_ref)
    low, upp = _tri(True), _tri(False)
    acs_f = _cumsum_lanes(dta[0:nh], upp)
    rcs_b = _cumsum_lanes(dta[nh:2 * nh], low)
    src = jnp.concatenate([acs_f - jnp.log(dts[0:nh]), rcs_b - jnp.log(dts[nh:2 * nh]),
                           jnp.zeros((LANES - 2 * nh, q), F32)], axis=0).T
    e_acs = jnp.exp(acs_f)
    neg_inf = jnp.float32(-jnp.inf)

    gw = HEADS_PER_GROUP * hd
    for g in range(N_GROUPS_A):
        b, c = _group_bc(xbc_ref, g)
        sct = _dot_nt(b, c)
        yoff = _dot_nt(s_ref[g * gw:(g + 1) * gw, :].astype(BF16), c)
        for e in range(HEADS_PER_GROUP):
            h = g * HEADS_PER_GROUP + e
            seg_f = acs_f[h:h + 1, :] - src[:, h:h + 1]
            seg_b = rcs_b[h:h + 1, :] - src[:, nh + h:nh + h + 1]
            wt = sct * (jnp.exp(jnp.where(upp, seg_f, neg_inf)) + jnp.exp(jnp.where(low, seg_b, neg_inf)))
            yd = _dot(xst_ref[h * hd:(h + 1) * hd, :].astype(BF16), wt.astype(BF16))
            ypt_ref[h * hd:(h + 1) * hd, :] = yd + yoff[e * hd:(e + 1) * hd, :] * e_acs[h:h + 1, :]

    tot = acs_f[:, q - 1:q]
    scale = dts[0:nh] * jnp.exp(tot - acs_f)
    cdec = jnp.exp(jnp.broadcast_to(tot, (nh, D_STATE)))
    _state_update(s_ref, xst_ref, xd_ref, xbc_ref, scale, cdec)

    @pl.when(jnp.logical_and(info["is_end"], info["is_prompt"]))
    def _():
        sfin_ref[...] = s_ref[...]


def _ssd_bwd_kernel(xm_ref, xp_ref, xn_ref, dt_ref, z_ref, ypt_ref, cw_ref, cb_ref, dtb_ref, alog_ref,
                    dsk_ref, ng_ref, s0_ref, ya_ref, sfin_ref, ext_ref, xbc_ref, xst_ref, xd_ref, yt_ref,
                    y_ref, s_ref, *, cfg, n_chunks):
    q, hd, nh = SSD_CHUNK, HEAD_DIM_A, N_HEADS_A
    info = _tile_info(n_chunks - 1 - pl.program_id(0), q, cfg)

    @pl.when(jnp.logical_and(info["is_end"], info["is_prompt"]))
    def _():
        s_ref[...] = jnp.zeros_like(s_ref)

    @pl.when(jnp.logical_and(info["is_end"], jnp.logical_not(info["is_prompt"])))
    def _():
        s_ref[...] = s0_ref[...]

    _conv_silu(xm_ref, xp_ref, xn_ref, cw_ref, cb_ref, ext_ref, xbc_ref, info["is_start"], info["is_end"])
    _transpose_xs(xbc_ref, xst_ref)

    dts, dta = _dt_terms(dt_ref, dtb_ref, alog_ref)
    rcs_b = _cumsum_lanes(dta[nh:2 * nh], _tri(True))
    e_rcs = jnp.exp(rcs_b)

    gw = HEADS_PER_GROUP * hd
    for g in range(N_GROUPS_A):
        _, c = _group_bc(xbc_ref, g)
        yoff = _dot_nt(s_ref[g * gw:(g + 1) * gw, :].astype(BF16), c)
        for e in range(HEADS_PER_GROUP):
            h = g * HEADS_PER_GROUP + e
            yt_ref[h * hd:(h + 1) * hd, :] = (ypt_ref[h * hd:(h + 1) * hd, :]
                                              + yoff[e * hd:(e + 1) * hd, :] * e_rcs[h:h + 1, :])

    for j in range(D_INNER_A // LANES):
        sl = slice(j * LANES, (j + 1) * LANES)
        y = yt_ref[j * LANES:(j + 1) * LANES, :].T + dsk_ref[:, sl] * xbc_ref[:, sl]
        y_ref[:, sl] = y * _silu(z_ref[:, sl])
    ya_ref[...] = _rms(y_ref[...], ng_ref[...]).astype(BF16)

    tot = rcs_b[:, 0:1]
    scale = dts[nh:2 * nh] * jnp.exp(tot - rcs_b)
    cdec = jnp.exp(jnp.broadcast_to(tot, (nh, D_STATE)))
    _state_update(s_ref, xst_ref, xd_ref, xbc_ref, scale, cdec)

    @pl.when(jnp.logical_and(info["is_start"], info["is_prompt"]))
    def _():
        sfin_ref[...] = s_ref[...]


def _halo_specs(width, tile, halo, n_rows, chunk_of):
    per = tile // halo
    last = n_rows // halo - 1
    return [
        pl.BlockSpec((tile, width), lambda i: (chunk_of(i), 0)),
        pl.BlockSpec((halo, width), lambda i: (jnp.maximum(chunk_of(i) * per - 1, 0), 0)),
        pl.BlockSpec((halo, width), lambda i: (jnp.minimum((chunk_of(i) + 1) * per, last), 0)),
    ]


def _ssd(xbc, dt, z, conv_w, conv_b, dtb, alog, dsk, norm_g, s0, cfg):
    t_tok = xbc.shape[0]
    q = SSD_CHUNK
    n_chunks = t_tok // q
    hp = N_HEADS_A * HEAD_DIM_A
    n_pc = cfg.batch * (cfg.seq // q)
    ncs = cfg.dec_seq // q
    ncp = cfg.seq // q

    def s_seq(c):
        return jnp.clip(lax.div(jnp.maximum(c - n_pc, 0), ncs), 0, cfg.dec_batch - 1)

    def p_seq(c):
        return jnp.minimum(lax.div(c, ncp), cfg.batch - 1)

    small = lambda shape: pl.BlockSpec(shape, lambda i: (0,) * len(shape))
    scratch = [pltpu.VMEM((q + 2 * HALO_A, D_XBC), F32), pltpu.VMEM((q, D_XBC), F32),
               pltpu.VMEM((hp, q), F32), pltpu.VMEM((hp, q), BF16)]

    fwd = lambda i: i
    ypt, sf = pl.pallas_call(
        functools.partial(_ssd_fwd_kernel, cfg=cfg),
        grid=(n_chunks,),
        in_specs=_halo_specs(D_XBC, q, HALO_A, t_tok, fwd) + [
            pl.BlockSpec((q, DT_PAD), lambda i: (i, 0)),
            small((CONV_A, D_XBC)), small((1, D_XBC)),
            small((2 * N_HEADS_A, q)), small((2 * N_HEADS_A, q)),
            pl.BlockSpec((None, None, hp, D_STATE), lambda i: (s_seq(i), 0, 0, 0)),
        ],
        out_specs=[pl.BlockSpec((hp, q), lambda i: (0, i)),
                   pl.BlockSpec((None, hp, D_STATE), lambda i: (p_seq(i), 0, 0))],
        out_shape=[jax.ShapeDtypeStruct((hp, t_tok), F32),
                   jax.ShapeDtypeStruct((cfg.batch, hp, D_STATE), F32)],
        scratch_shapes=scratch + [pltpu.VMEM((hp, D_STATE), F32)],
        compiler_params=_params(("arbitrary",)),
        name="ssd_fwd",
    )(xbc, xbc, xbc, dt, conv_w, conv_b, dtb, alog, s0)

    rev = lambda i: n_chunks - 1 - i
    ya, sb = pl.pallas_call(
        functools.partial(_ssd_bwd_kernel, cfg=cfg, n_chunks=n_chunks),
        grid=(n_chunks,),
        in_specs=_halo_specs(D_XBC, q, HALO_A, t_tok, rev) + [
            pl.BlockSpec((q, DT_PAD), lambda i: (rev(i), 0)),
            pl.BlockSpec((q, D_INNER_A), lambda i: (rev(i), 0)),
            pl.BlockSpec((hp, q), lambda i: (0, rev(i))),
            small((CONV_A, D_XBC)), small((1, D_XBC)),
            small((2 * N_HEADS_A, q)), small((2 * N_HEADS_A, q)),
            small((1, D_INNER_A)), small((1, D_INNER_A)),
            pl.BlockSpec((None, None, hp, D_STATE), lambda i: (s_seq(rev(i)), 1, 0, 0)),
        ],
        out_specs=[pl.BlockSpec((q, D_INNER_A), lambda i: (rev(i), 0)),
                   pl.BlockSpec((None, hp, D_STATE), lambda i: (p_seq(rev(i)), 0, 0))],
        out_shape=[jax.ShapeDtypeStruct((t_tok, D_INNER_A), BF16),
                   jax.ShapeDtypeStruct((cfg.batch, hp, D_STATE), F32)],
        scratch_shapes=scratch + [pltpu.VMEM((hp, q), F32), pltpu.VMEM((q, D_INNER_A), F32),
                                  pltpu.VMEM((hp, D_STATE), F32)],
        compiler_params=_params(("arbitrary",)),
        name="ssd_bwd",
    )(xbc, xbc, xbc, dt, z, ypt, conv_w, conv_b, dtb, alog, dsk, norm_g, s0)
    return ya, sf, sb


def _even_out_kernel(ya_ref, pm_ref, pp_ref, pn_ref, x_ref, gt_ref, pw_ref, ps_ref, woa_ref, wop_ref,
                     o_ref, ext_ref, yp_ref, *, cfg):
    tm = TM_MIX
    t = pl.program_id(0)
    info = _tile_info(t, tm, cfg)
    r = _mod_row(t, tm, cfg)
    zero = jnp.zeros((HALO_A, D_POOL), F32)
    ext_ref[0:HALO_A, :] = jnp.where(info["is_start"], zero, pp_ref[...])
    ext_ref[HALO_A:HALO_A + tm, :] = pm_ref[...]
    ext_ref[HALO_A + tm:, :] = jnp.where(info["is_end"], zero, pn_ref[...])
    pos = info["pos"] * tm + lax.broadcasted_iota(jnp.int32, (tm, POOL_GROUP_DIM), 0)
    for gi, win in enumerate(POOL_WINDOWS):
        sl = slice(gi * POOL_GROUP_DIM, (gi + 1) * POOL_GROUP_DIM)
        lo = HALO_A - win // 2
        s = ext_ref[lo:lo + tm, sl]
        for j in range(1, win):
            s = s + ext_ref[lo + j:lo + j + tm, sl]
        cnt = (jnp.minimum(pos - win // 2 + win, info["seq_len"]) - jnp.maximum(pos - win // 2, 0)).astype(F32)
        pooled = s / cnt - ext_ref[HALO_A:HALO_A + tm, sl]
        yp_ref[:, sl] = (_dot(pooled.astype(BF16), pw_ref[gi]) * ps_ref[:, sl]).astype(BF16)
    out = _dot(ya_ref[...], woa_ref[...]) + _dot(yp_ref[...], wop_ref[...])
    o_ref[...] = x_ref[...] + gt_ref[pl.ds(r, 1), :] * out


def _even_out(ya, pool_in, x, mod, i, pool_w, pool_scale, wo_a, wo_p, cfg):
    t_tok = x.shape[0]
    tm = TM_MIX
    small = lambda shape: pl.BlockSpec(shape, lambda t: (0,) * len(shape))
    return pl.pallas_call(
        functools.partial(_even_out_kernel, cfg=cfg),
        grid=(t_tok // tm,),
        in_specs=[pl.BlockSpec((tm, D_INNER_A), lambda t: (t, 0))]
        + _halo_specs(D_POOL, tm, HALO_A, t_tok, lambda t: t) + [
            pl.BlockSpec((tm, D_MODEL), lambda t: (t, 0)),
            _mod_spec(i, 5, 1),
            small((len(POOL_WINDOWS), POOL_GROUP_DIM, POOL_GROUP_DIM)), small((1, D_POOL)),
            small((D_INNER_A, D_MODEL)), small((D_POOL, D_MODEL)),
        ],
        out_specs=pl.BlockSpec((tm, D_MODEL), lambda t: (t, 0)),
        out_shape=jax.ShapeDtypeStruct((t_tok, D_MODEL), F32),
        scratch_shapes=[pltpu.VMEM((tm + 2 * HALO_A, D_POOL), F32), pltpu.VMEM((tm, D_POOL), BF16)],
        compiler_params=_params(("parallel",)),
        name="even_out",
    )(ya, pool_in, pool_in, pool_in, x, mod, pool_w, pool_scale, wo_a, wo_p)


def _odd_out_kernel(u_ref, v_ref, gm_ref, gp_ref, gn_ref, x_ref, gt_ref, spw_ref, spb_ref, dww_ref, dwb_ref,
                    cg_ref, cb_ref, woc_ref, wod_ref, o_ref, ext_ref, yc_ref, yd_ref, *, cfg):
    tm = TM_MIX
    t = pl.program_id(0)
    info = _tile_info(t, tm, cfg)
    r = _mod_row(t, tm, cfg)
    hc = D_C // N_HEADS_C
    for ck in range(tm // MLP_CHUNK):
        rows = slice(ck * MLP_CHUNK, (ck + 1) * MLP_CHUNK)
        for h in range(N_HEADS_C):
            sl = slice(h * hc, (h + 1) * hc)
            sv = _dot(spw_ref[h], v_ref[rows, sl]) + spb_ref[:, sl]
            yc_ref[rows, sl] = (u_ref[rows, sl] * sv).astype(BF16)
    zero = jnp.zeros((HALO_D, D_D), F32)
    ext_ref[0:HALO_D, :] = jnp.where(info["is_start"], zero, gp_ref[...])
    ext_ref[HALO_D:HALO_D + tm, :] = gm_ref[...]
    ext_ref[HALO_D + tm:, :] = jnp.where(info["is_end"], zero, gn_ref[...])
    base = HALO_D - CONV_D // 2
    rb = MLP_CHUNK
    for j in range(D_D // LANES):
        sl = slice(j * LANES, (j + 1) * LANES)
        for rr in range(tm // rb):
            acc = dww_ref[0:1, sl] * ext_ref[base + rr * rb:base + rr * rb + rb, sl]
            for k in range(1, CONV_D):
                acc = acc + dww_ref[k:k + 1, sl] * ext_ref[base + rr * rb + k:base + rr * rb + k + rb, sl]
            yd_ref[rr * rb:(rr + 1) * rb, sl] = acc + dwb_ref[:, sl]
    ydn = _silu(_layer_norm(yd_ref[...], cg_ref[...], cb_ref[...])).astype(BF16)
    out = _dot(yc_ref[...], woc_ref[...]) + _dot(ydn, wod_ref[...])
    o_ref[...] = x_ref[...] + gt_ref[pl.ds(r, 1), :] * out


def _odd_out(u, v, glu, x, mod, i, sp_w, sp_b_full, dw_w, dw_b, cn_g, cn_b, wo_c, wo_d, cfg):
    t_tok = x.shape[0]
    tm = TM_MIX
    small = lambda shape: pl.BlockSpec(shape, lambda t: (0,) * len(shape))
    return pl.pallas_call(
        functools.partial(_odd_out_kernel, cfg=cfg),
        grid=(t_tok // tm,),
        in_specs=[pl.BlockSpec((tm, D_C), lambda t: (t, 0)), pl.BlockSpec((tm, D_C), lambda t: (t, 0))]
        + _halo_specs(D_D, tm, HALO_D, t_tok, lambda t: t) + [
            pl.BlockSpec((tm, D_MODEL), lambda t: (t, 0)),
            _mod_spec(i, 5, 1),
            small((N_HEADS_C, MLP_CHUNK, MLP_CHUNK)), small((MLP_CHUNK, D_C)),
            small((CONV_D, D_D)), small((1, D_D)), small((1, D_D)), small((1, D_D)),
            small((D_C, D_MODEL)), small((D_D, D_MODEL)),
        ],
        out_specs=pl.BlockSpec((tm, D_MODEL), lambda t: (t, 0)),
        out_shape=jax.ShapeDtypeStruct((t_tok, D_MODEL), F32),
        scratch_shapes=[pltpu.VMEM((tm + 2 * HALO_D, D_D), F32), pltpu.VMEM((tm, D_C), BF16),
                        pltpu.VMEM((tm, D_D), F32)],
        compiler_params=_params(("parallel",)),
        name="odd_out",
    )(u, v, glu, glu, glu, x, mod, sp_w, sp_b_full, dw_w, dw_b, cn_g, cn_b, wo_c, wo_d)


def _grid_position_code(n_tokens):
    rows = n_tokens // GRID_W
    quarter = D_MODEL // 4
    freqs = jnp.exp(-math.log(POS_BASE) * jnp.arange(quarter, dtype=F32) / quarter)
    row = jnp.broadcast_to(jnp.arange(rows, dtype=F32)[:, None], (rows, GRID_W)).reshape(-1)
    col = jnp.broadcast_to(jnp.arange(GRID_W, dtype=F32)[None, :], (rows, GRID_W)).reshape(-1)
    ang_r = row[:, None] * freqs
    ang_c = col[:, None] * freqs
    return jnp.concatenate([jnp.sin(ang_r), jnp.cos(ang_r), jnp.sin(ang_c), jnp.cos(ang_c)], axis=-1)


def _forward(cfg, x_prompt, x_sample, c, state_ssd, c_ctx, w_mod, b_mod, norm_g, ffn_w_gate, ffn_w_up,
             ffn_w_down, ev_w_in, ev_conv_w, ev_conv_b, ev_dt_bias, ev_a_log, ev_d_skip, ev_ssd_norm_g,
             ev_pool_w, ev_pool_scale, ev_w_out, od_w_in, od_v_ln_g, od_v_ln_b, od_sp_w, od_sp_b, od_dw_w,
             od_dw_b, od_cn_g, od_cn_b, od_w_out, final_norm_g):
    t_p = cfg.batch * cfg.seq
    hp = N_HEADS_A * HEAD_DIM_A
    x_lat = x_sample + _grid_position_code(cfg.dec_seq)[None]
    x = jnp.concatenate([x_prompt.reshape(t_p, D_MODEL), x_lat.reshape(-1, D_MODEL)], axis=0)

    cond8 = jnp.concatenate([c_ctx[None, :], c, jnp.zeros((SUBLANES - 1 - cfg.dec_batch, D_MODEL), F32)], axis=0)
    mod = _modulation(cond8, w_mod, b_mod)

    finals = []
    for i in range(DEPTH):
        j = i // 2
        x = _ffn(x, mod, i, 0, norm_g[i, 0], ffn_w_gate[i, 0].astype(BF16), ffn_w_up[i, 0].astype(BF16),
                 ffn_w_down[i, 0].astype(BF16), cfg)
        if i % 2 == 0:
            w = ev_w_in[j]
            o_dt = D_INNER_A + D_XBC
            o_pool = o_dt + 2 * N_HEADS_A
            w_all = jnp.concatenate(
                [w[:, :o_dt], w[:, o_pool:], w[:, o_dt:o_pool],
                 jnp.zeros((D_MODEL, DT_PAD - 2 * N_HEADS_A), F32)], axis=1).astype(BF16)
            z, xbc, pool_in, dt = _even_in(x, mod, i, norm_g[i, 1], w_all, cfg)
            lane_b = lambda v: jnp.broadcast_to(v.reshape(2 * N_HEADS_A, 1), (2 * N_HEADS_A, SSD_CHUNK))
            dsk = jnp.broadcast_to(ev_d_skip[j][:, None], (N_HEADS_A, HEAD_DIM_A)).reshape(1, D_INNER_A)
            s0 = state_ssd[:, j].reshape(cfg.dec_batch, 2, hp, D_STATE)
            ya, sf, sb = _ssd(xbc, dt, z, ev_conv_w[j], ev_conv_b[j].reshape(1, D_XBC), lane_b(ev_dt_bias[j]),
                              lane_b(ev_a_log[j]), dsk, ev_ssd_norm_g[j].reshape(1, D_INNER_A), s0, cfg)
            finals.append(jnp.stack([sf, sb], axis=1))
            x = _even_out(ya, pool_in, x, mod, i, ev_pool_w[j].astype(BF16), ev_pool_scale[j].reshape(1, D_POOL),
                          ev_w_out[j, :D_INNER_A].astype(BF16), ev_w_out[j, D_INNER_A:].astype(BF16), cfg)
        else:
            u, v, glu = _odd_in(x, mod, i, norm_g[i, 1], od_w_in[j].astype(BF16), od_v_ln_g[j], od_v_ln_b[j], cfg)
            spb = jnp.broadcast_to(od_sp_b[j].T[:, :, None], (MLP_CHUNK, N_HEADS_C, D_C // N_HEADS_C))
            x = _odd_out(u, v, glu, x, mod, i, od_sp_w[j].astype(BF16), spb.reshape(MLP_CHUNK, D_C), od_dw_w[j],
                         od_dw_b[j].reshape(1, D_D), od_cn_g[j].reshape(1, D_D), od_cn_b[j].reshape(1, D_D),
                         od_w_out[j, :D_C].astype(BF16), od_w_out[j, D_C:].astype(BF16), cfg)
        x = _ffn(x, mod, i, 6, norm_g[i, 2], ffn_w_gate[i, 1].astype(BF16), ffn_w_up[i, 1].astype(BF16),
                 ffn_w_down[i, 1].astype(BF16), cfg, final_g=final_norm_g if i == DEPTH - 1 else None)

    y_prompt = x[:t_p].reshape(cfg.batch, cfg.seq, D_MODEL)
    y_sample = x[t_p:].reshape(cfg.dec_batch, cfg.dec_seq, D_MODEL)
    new_state = jnp.stack(finals, axis=1).reshape(cfg.batch, N_EVEN, 2, N_HEADS_A, HEAD_DIM_A, D_STATE)
    return y_prompt, y_sample, new_state


def kernel(x_prompt, x_sample, c, state_ssd, c_ctx, w_mod, b_mod, norm_g, ffn_w_gate, ffn_w_up, ffn_w_down, ev_w_in, ev_conv_w, ev_conv_b, ev_dt_bias, ev_a_log, ev_d_skip, ev_ssd_norm_g, ev_pool_w, ev_pool_scale, ev_w_out, od_w_in, od_v_ln_g, od_v_ln_b, od_sp_w, od_sp_b, od_dw_w, od_dw_b, od_cn_g, od_cn_b, od_w_out, final_norm_g):
    cfg = Cfg(x_prompt.shape[0], x_prompt.shape[1], x_sample.shape[0], x_sample.shape[1])
    return _forward(cfg, x_prompt, x_sample, c, state_ssd, c_ctx, w_mod, b_mod, norm_g, ffn_w_gate, ffn_w_up,
                    ffn_w_down, ev_w_in, ev_conv_w, ev_conv_b, ev_dt_bias, ev_a_log, ev_d_skip, ev_ssd_norm_g,
                    ev_pool_w, ev_pool_scale, ev_w_out, od_w_in, od_v_ln_g, od_v_ln_b, od_sp_w, od_sp_b, od_dw_w,
                    od_dw_b, od_cn_g, od_cn_b, od_w_out, final_norm_g)
```

```python
import collections
import functools
import math

import jax
import jax.numpy as jnp
from jax import lax
from jax.experimental import pallas as pl
from jax.experimental.pallas import tpu as pltpu

F32 = jnp.float32
BF16 = jnp.bfloat16

D_MODEL = 1024
DEPTH = 4
GRID_W = 64
POS_BASE = 10000.0
EPS = 1e-6
N_MOD = 9
D_FF = 2816
N_HEADS_A = 16
HEAD_DIM_A = 64
D_INNER_A = N_HEADS_A * HEAD_DIM_A
N_GROUPS_A = 2
HEADS_PER_GROUP = N_HEADS_A // N_GROUPS_A
D_STATE = 128
CONV_A = 5
SSD_CHUNK = 128
D_BC = N_GROUPS_A * D_STATE
D_XBC = D_INNER_A + 2 * D_BC
POOL_WINDOWS = (2, 4, 8, 16)
POOL_GROUP_DIM = 128
D_POOL = len(POOL_WINDOWS) * POOL_GROUP_DIM
N_HEADS_C = 8
MLP_CHUNK = 128
D_C = 1024
D_D = 1024
CONV_D = 31
N_EVEN = (DEPTH + 1) // 2

LANES = 128
SUBLANES = 8
DT_PAD = LANES
HALO_A = SUBLANES
HALO_D = 2 * SUBLANES
VMEM_LIMIT = 56 * 1024 * 1024

TM_FFN = 512
TF_FFN = 256
TM_PROJ = 512
TM_MIX = 256

Cfg = collections.namedtuple("Cfg", "batch seq dec_batch dec_seq")


def _n_tokens(cfg):
    return cfg.batch * cfg.seq + cfg.dec_batch * cfg.dec_seq


def _tile_info(t, tm, cfg):
    npt = cfg.seq // tm
    nst = cfg.dec_seq // tm
    n_p = cfg.batch * npt
    is_prompt = t < n_p
    ts = jnp.maximum(t - n_p, 0)
    pos = jnp.where(is_prompt, lax.rem(t, npt), lax.rem(ts, nst))
    last = jnp.where(is_prompt, npt - 1, nst - 1)
    info = dict(
        is_prompt=is_prompt,
        is_start=pos == 0,
        is_end=pos == last,
        pos=pos,
        seq_len=jnp.where(is_prompt, cfg.seq, cfg.dec_seq),
        p_seq=jnp.minimum(lax.div(t, npt), cfg.batch - 1),
        s_seq=lax.div(ts, nst),
    )
    return info


def _mod_row(t, tm, cfg):
    n_p = (cfg.batch * cfg.seq) // tm
    per = cfg.dec_seq // tm
    return jnp.where(t < n_p, 0, 1 + lax.div(jnp.maximum(t - n_p, 0), per))


def _silu(x):
    return x * jax.nn.sigmoid(x)


def _softplus(x):
    return jnp.maximum(x, 0.0) + jnp.log1p(jnp.exp(-jnp.abs(x)))


def _rms(x, g):
    return x * lax.rsqrt(jnp.mean(x * x, axis=-1, keepdims=True) + EPS) * g


def _layer_norm(x, g, b):
    xc = x - jnp.mean(x, axis=-1, keepdims=True)
    y = xc * lax.rsqrt(jnp.mean(xc * xc, axis=-1, keepdims=True) + EPS)
    return y * g + b


def _dot(a, b):
    return jnp.dot(a, b, preferred_element_type=F32)


def _dot_nt(a, b):
    return lax.dot_general(a, b, (((1,), (1,)), ((), ())), preferred_element_type=F32)


def _params(sem):
    return pltpu.CompilerParams(dimension_semantics=sem, vmem_limit_bytes=VMEM_LIMIT)


def _mod_kernel(c_ref, w_ref, b_ref, o_ref):
    c = c_ref[...]
    sc = _silu(c).astype(BF16)
    o_ref[...] = _dot(sc, w_ref[...].astype(BF16)) + b_ref[...]


def _modulation(cond8, w_mod, b_mod):
    b4 = b_mod.reshape(DEPTH, N_MOD, 1, D_MODEL)
    return pl.pallas_call(
        _mod_kernel,
        grid=(DEPTH, N_MOD),
        in_specs=[
            pl.BlockSpec((SUBLANES, D_MODEL), lambda i, k: (0, 0)),
            pl.BlockSpec((None, D_MODEL, D_MODEL), lambda i, k: (i, 0, k)),
            pl.BlockSpec((None, None, 1, D_MODEL), lambda i, k: (i, k, 0, 0)),
        ],
        out_specs=pl.BlockSpec((None, None, SUBLANES, D_MODEL), lambda i, k: (i, k, 0, 0)),
        out_shape=jax.ShapeDtypeStruct((DEPTH, N_MOD, SUBLANES, D_MODEL), F32),
        compiler_params=_params(("arbitrary", "arbitrary")),
        name="modulation",
    )(cond8, w_mod, b4)


def _mod_spec(i, k):
    return pl.BlockSpec((None, None, SUBLANES, D_MODEL), lambda t: (i, k, 0, 0))


def _ffn_kernel(x_ref, sh_ref, sc_ref, gt_ref, g_ref, wg_ref, wu_ref, wd_ref, *rest, cfg, final):
    if final:
        fg_ref, o_ref, hn_ref, act_ref = rest
    else:
        o_ref, hn_ref, act_ref = rest
    r = _mod_row(pl.program_id(0), TM_FFN, cfg)
    hn = _rms(x_ref[...], g_ref[...]) * (1.0 + sc_ref[pl.ds(r, 1), :]) + sh_ref[pl.ds(r, 1), :]
    hn_ref[...] = hn.astype(BF16)
    for f in range(D_FF // TF_FFN):
        sl = slice(f * TF_FFN, (f + 1) * TF_FFN)
        g = _dot(hn_ref[...], wg_ref[:, sl])
        u = _dot(hn_ref[...], wu_ref[:, sl])
        act_ref[:, sl] = (_silu(g) * u).astype(BF16)
    xn = x_ref[...] + (0.5 * gt_ref[pl.ds(r, 1), :]) * _dot(act_ref[...], wd_ref[...])
    if final:
        xn = _rms(xn, fg_ref[...])
    o_ref[...] = xn


def _resident(shape):
    return pl.BlockSpec(shape, lambda t: (0,) * len(shape), pipeline_mode=pl.Buffered(1))


def _ffn(x, mod, i, k0, g, wg, wu, wd, cfg, final_g=None):
    t_tok = x.shape[0]
    final = final_g is not None
    in_specs = [
        pl.BlockSpec((TM_FFN, D_MODEL), lambda t: (t, 0)),
        _mod_spec(i, k0), _mod_spec(i, k0 + 1), _mod_spec(i, k0 + 2),
        pl.BlockSpec((1, D_MODEL), lambda t: (0, 0)),
        _resident((D_MODEL, D_FF)), _resident((D_MODEL, D_FF)), _resident((D_FF, D_MODEL)),
    ]
    args = [x, mod, mod, mod, g.reshape(1, D_MODEL), wg, wu, wd]
    if final:
        in_specs.append(pl.BlockSpec((1, D_MODEL), lambda t: (0, 0)))
        args.append(final_g.reshape(1, D_MODEL))
    return pl.pallas_call(
        functools.partial(_ffn_kernel, cfg=cfg, final=final),
        grid=(t_tok // TM_FFN,),
        in_specs=in_specs,
        out_specs=pl.BlockSpec((TM_FFN, D_MODEL), lambda t: (t, 0)),
        out_shape=jax.ShapeDtypeStruct((t_tok, D_MODEL), F32),
        scratch_shapes=[pltpu.VMEM((TM_FFN, D_MODEL), BF16), pltpu.VMEM((TM_FFN, D_FF), BF16)],
        compiler_params=_params(("parallel",)),
        name="ffn",
    )(*args)


def _norm_mod(x_ref, sh_ref, sc_ref, g_ref, cfg):
    r = _mod_row(pl.program_id(0), TM_PROJ, cfg)
    hn = _rms(x_ref[...], g_ref[...]) * (1.0 + sc_ref[pl.ds(r, 1), :]) + sh_ref[pl.ds(r, 1), :]
    return hn.astype(BF16)


def _even_in_kernel(x_ref, sh_ref, sc_ref, g_ref, w_ref, z_ref, xbc_ref, pool_ref, dt_ref, *, cfg):
    hn = _norm_mod(x_ref, sh_ref, sc_ref, g_ref, cfg)
    o = 0
    for ref, width in ((z_ref, D_INNER_A), (xbc_ref, D_XBC), (pool_ref, D_POOL), (dt_ref, DT_PAD)):
        ref[...] = _dot(hn, w_ref[:, o:o + width])
        o += width


def _even_in(x, mod, i, g, w, cfg):
    t_tok = x.shape[0]
    widths = (D_INNER_A, D_XBC, D_POOL, DT_PAD)
    n_tot = sum(widths)
    return pl.pallas_call(
        functools.partial(_even_in_kernel, cfg=cfg),
        grid=(t_tok // TM_PROJ,),
        in_specs=[
            pl.BlockSpec((TM_PROJ, D_MODEL), lambda t: (t, 0)),
            _mod_spec(i, 3), _mod_spec(i, 4),
            pl.BlockSpec((1, D_MODEL), lambda t: (0, 0)),
            pl.BlockSpec((D_MODEL, n_tot), lambda t: (0, 0)),
        ],
        out_specs=[pl.BlockSpec((TM_PROJ, wd), lambda t: (t, 0)) for wd in widths],
        out_shape=[jax.ShapeDtypeStruct((t_tok, wd), F32) for wd in widths],
        compiler_params=_params(("parallel",)),
        name="even_in",
    )(x, mod, mod, g.reshape(1, D_MODEL), w)


def _odd_in_kernel(x_ref, sh_ref, sc_ref, g_ref, w_ref, lg_ref, lb_ref, u_ref, v_ref, glu_ref, *, cfg):
    hn = _norm_mod(x_ref, sh_ref, sc_ref, g_ref, cfg)
    u_ref[...] = jax.nn.gelu(_dot(hn, w_ref[:, 0:D_C]))
    v = jax.nn.gelu(_dot(hn, w_ref[:, D_C:2 * D_C]))
    v_ref[...] = _layer_norm(v, lg_ref[...], lb_ref[...]).astype(BF16)
    ga = _dot(hn, w_ref[:, 2 * D_C:2 * D_C + D_D])
    gg = _dot(hn, w_ref[:, 2 * D_C + D_D:2 * D_C + 2 * D_D])
    glu_ref[...] = ga * jax.nn.sigmoid(gg)


def _odd_in(x, mod, i, g, w, ln_g, ln_b, cfg):
    t_tok = x.shape[0]
    row = lambda: pl.BlockSpec((1, D_MODEL), lambda t: (0, 0))
    return pl.pallas_call(
        functools.partial(_odd_in_kernel, cfg=cfg),
        grid=(t_tok // TM_PROJ,),
        in_specs=[
            pl.BlockSpec((TM_PROJ, D_MODEL), lambda t: (t, 0)),
            _mod_spec(i, 3), _mod_spec(i, 4),
            row(),
            pl.BlockSpec((D_MODEL, 2 * D_C + 2 * D_D), lambda t: (0, 0)),
            row(), row(),
        ],
        out_specs=[pl.BlockSpec((TM_PROJ, D_C), lambda t: (t, 0))] * 3,
        out_shape=[jax.ShapeDtypeStruct((t_tok, D_C), F32),
                   jax.ShapeDtypeStruct((t_tok, D_C), BF16),
                   jax.ShapeDtypeStruct((t_tok, D_D), F32)],
        compiler_params=_params(("parallel",)),
        name="odd_in",
    )(x, mod, mod, g.reshape(1, D_MODEL), w, ln_g.reshape(1, D_C), ln_b.reshape(1, D_C))


def _conv_silu(xm_ref, xp_ref, xn_ref, cw_ref, cb_ref, ext_ref, xbc_ref, is_start, is_end):
    q = SSD_CHUNK
    zero = jnp.zeros((HALO_A, D_XBC), F32)
    ext_ref[0:HALO_A, :] = jnp.where(is_start, zero, xp_ref[...])
    ext_ref[HALO_A:HALO_A + q, :] = xm_ref[...]
    ext_ref[HALO_A + q:, :] = jnp.where(is_end, zero, xn_ref[...])
    base = HALO_A - CONV_A // 2
    for j in range(D_XBC // LANES):
        sl = slice(j * LANES, (j + 1) * LANES)
        acc = cw_ref[0:1, sl] * ext_ref[base:base + q, sl]
        for k in range(1, CONV_A):
            acc = acc + cw_ref[k:k + 1, sl] * ext_ref[base + k:base + k + q, sl]
        acc = acc + cb_ref[:, sl]
        xbc_ref[:, sl] = _silu(acc)


def _transpose_xs(xbc_ref, xst_ref):
    for j in range(D_INNER_A // LANES):
        xst_ref[j * LANES:(j + 1) * LANES, :] = xbc_ref[:, j * LANES:(j + 1) * LANES].T


def _dt_terms(dt_ref, dtb_ref, alog_ref):
    dtt = dt_ref[...].T[0:2 * N_HEADS_A, :]
    dts = _softplus(dtt + dtb_ref[...])
    return dts, dts * (-jnp.exp(alog_ref[...]))


def _tri(lower_incl):
    ri = lax.broadcasted_iota(jnp.int32, (SSD_CHUNK, SSD_CHUNK), 0)
    ci = lax.broadcasted_iota(jnp.int32, (SSD_CHUNK, SSD_CHUNK), 1)
    return (ri >= ci) if lower_incl else (ri <= ci)


def _cumsum_lanes(v, mask):
    return jnp.dot(v, mask.astype(F32), precision=lax.Precision.HIGHEST, preferred_element_type=F32)


def _group_bc(xbc_ref, g):
    b = xbc_ref[:, D_INNER_A + g * D_STATE:D_INNER_A + (g + 1) * D_STATE].astype(BF16)
    c = xbc_ref[:, D_INNER_A + D_BC + g * D_STATE:D_INNER_A + D_BC + (g + 1) * D_STATE].astype(BF16)
    return b, c


def _state_update(s_ref, xst_ref, xd_ref, xbc_ref, scale, cdec):
    hd = HEAD_DIM_A
    for h in range(N_HEADS_A):
        xd_ref[h * hd:(h + 1) * hd, :] = (xst_ref[h * hd:(h + 1) * hd, :] * scale[h:h + 1, :]).astype(BF16)
    gw = HEADS_PER_GROUP * hd
    for g in range(N_GROUPS_A):
        b, _ = _group_bc(xbc_ref, g)
        upd = _dot(xd_ref[g * gw:(g + 1) * gw, :], b)
        for e in range(HEADS_PER_GROUP):
            h = g * HEADS_PER_GROUP + e
            s_ref[h * hd:(h + 1) * hd, :] = (s_ref[h * hd:(h + 1) * hd, :] * cdec[h:h + 1, :]
                                             + upd[e * hd:(e + 1) * hd, :])


def _ssd_fwd_kernel(xm_ref, xp_ref, xn_ref, dt_ref, cw_ref, cb_ref, dtb_ref, alog_ref, s0_ref,
                    ypt_ref, sfin_ref, ext_ref, xbc_ref, xst_ref, xd_ref, s_ref, *, cfg):
    q, hd, nh = SSD_CHUNK, HEAD_DIM_A, N_HEADS_A
    info = _tile_info(pl.program_id(0), q, cfg)

    @pl.when(jnp.logical_and(info["is_start"], info["is_prompt"]))
    def _():
        s_ref[...] = jnp.zeros_like(s_ref)

    @pl.when(jnp.logical_and(info["is_start"], jnp.logical_not(info["is_prompt"])))
    def _():
        s_ref[...] = s0_ref[...]

    _conv_silu(xm_ref, xp_ref, xn_ref, cw_ref, cb_ref, ext_ref, xbc_ref, info["is_start"], info["is_end"])
    _transpose_xs(xbc_ref, xst_ref)

    dts, dta = _dt_terms(dt_ref, dtb_ref, alog_ref)
    low, upp = _tri(True), _tri(False)
    acs_f = _cumsum_lanes(dta[0:nh], upp)
    rcs_b = _cumsum_lanes(dta[nh:2 * nh], low)
    src = jnp.concatenate([acs_f - jnp.log(dts[0:nh]), rcs_b - jnp.log(dts[nh:2 * nh]),
                           jnp.zeros((LANES - 2 * nh, q), F32)], axis=0).T
    e_acs = jnp.exp(acs_f)
    neg_inf = jnp.float32(-jnp.inf)

    gw = HEADS_PER_GROUP * hd
    for g in range(N_GROUPS_A):
        b, c = _group_bc(xbc_ref, g)
        sct = _dot_nt(b, c)
        yoff = _dot_nt(s_ref[g * gw:(g + 1) * gw, :].astype(BF16), c)
        for e in range(HEADS_PER_GROUP):
            h = g * HEADS_PER_GROUP + e
            seg_f = acs_f[h:h + 1, :] - src[:, h:h + 1]
            seg_b = rcs_b[h:h + 1, :] - src[:, nh + h:nh + h + 1]
            wt = sct * (jnp.exp(jnp.where(upp, seg_f, neg_inf)) + jnp.exp(jnp.where(low, seg_b, neg_inf)))
            yd = _dot(xst_ref[h * hd:(h + 1) * hd, :].astype(BF16), wt.astype(BF16))
            ypt_ref[h * hd:(h + 1) * hd, :] = yd + yoff[e * hd:(e + 1) * hd, :] * e_acs[h:h + 1, :]

    tot = acs_f[:, q - 1:q]
    scale = dts[0:nh] * jnp.exp(tot - acs_f)
    cdec = jnp.exp(jnp.broadcast_to(tot, (nh, D_STATE)))
    _state_update(s_ref, xst_ref, xd_ref, xbc_ref, scale, cdec)

    @pl.when(jnp.logical_and(info["is_end"], info["is_prompt"]))
    def _():
        sfin_ref[...] = s_ref[...]


def _ssd_bwd_kernel(xm_ref, xp_ref, xn_ref, dt_ref, z_ref, ypt_ref, cw_ref, cb_ref, dtb_ref, alog_ref,
                    dsk_ref, ng_ref, s0_ref, ya_ref, sfin_ref, ext_ref, xbc_ref, xst_ref, xd_ref, yt_ref,
                    y_ref, s_ref, *, cfg, n_chunks):
    q, hd, nh = SSD_CHUNK, HEAD_DIM_A, N_HEADS_A
    info = _tile_info(n_chunks - 1 - pl.program_id(0), q, cfg)

    @pl.when(jnp.logical_and(info["is_end"], info["is_prompt"]))
    def _():
        s_ref[...] = jnp.zeros_like(s_ref)

    @pl.when(jnp.logical_and(info["is_end"], jnp.logical_not(info["is_prompt"])))
    def _():
        s_ref[...] = s0_ref[...]

    _conv_silu(xm_ref, xp_ref, xn_ref, cw_ref, cb_ref, ext_ref, xbc_ref, info["is_start"], info["is_end"])
    _transpose_xs(xbc_ref, xst_ref)

    dts, dta = _dt_terms(dt_ref, dtb_ref, alog_ref)
    rcs_b = _cumsum_lanes(dta[nh:2 * nh], _tri(True))
    e_rcs = jnp.exp(rcs_b)

    gw = HEADS_PER_GROUP * hd
    for g in range(N_GROUPS_A):
        _, c = _group_bc(xbc_ref, g)
        yoff = _dot_nt(s_ref[g * gw:(g + 1) * gw, :].astype(BF16), c)
        for e in range(HEADS_PER_GROUP):
            h = g * HEADS_PER_GROUP + e
            yt_ref[h * hd:(h + 1) * hd, :] = (ypt_ref[h * hd:(h + 1) * hd, :]
                                              + yoff[e * hd:(e + 1) * hd, :] * e_rcs[h:h + 1, :])

    for j in range(D_INNER_A // LANES):
        sl = slice(j * LANES, (j + 1) * LANES)
        y = yt_ref[j * LANES:(j + 1) * LANES, :].T + dsk_ref[:, sl] * xbc_ref[:, sl]
        y_ref[:, sl] = y * _silu(z_ref[:, sl])
    ya_ref[...] = _rms(y_ref[...], ng_ref[...]).astype(BF16)

    tot = rcs_b[:, 0:1]
    scale = dts[nh:2 * nh] * jnp.exp(tot - rcs_b)
    cdec = jnp.exp(jnp.broadcast_to(tot, (nh, D_STATE)))
    _state_update(s_ref, xst_ref, xd_ref, xbc_ref, scale, cdec)

    @pl.when(jnp.logical_and(info["is_start"], info["is_prompt"]))
    def _():
        sfin_ref[...] = s_ref[...]


def _halo_specs(width, tile, halo, n_rows, chunk_of):
    per = tile // halo
    last = n_rows // halo - 1
    return [
        pl.BlockSpec((tile, width), lambda i: (chunk_of(i), 0)),
        pl.BlockSpec((halo, width), lambda i: (jnp.maximum(chunk_of(i) * per - 1, 0), 0)),
        pl.BlockSpec((halo, width), lambda i: (jnp.minimum((chunk_of(i) + 1) * per, last), 0)),
    ]


def _ssd(xbc, dt, z, conv_w, conv_b, dtb, alog, dsk, norm_g, s0, cfg):
    t_tok = xbc.shape[0]
    q = SSD_CHUNK
    n_chunks = t_tok // q
    hp = N_HEADS_A * HEAD_DIM_A
    n_pc = cfg.batch * (cfg.seq // q)
    ncs = cfg.dec_seq // q
    ncp = cfg.seq // q

    def s_seq(c):
        return jnp.clip(lax.div(jnp.maximum(c - n_pc, 0), ncs), 0, cfg.dec_batch - 1)

    def p_seq(c):
        return jnp.minimum(lax.div(c, ncp), cfg.batch - 1)

    small = lambda shape: pl.BlockSpec(shape, lambda i: (0,) * len(shape))
    scratch = [pltpu.VMEM((q + 2 * HALO_A, D_XBC), F32), pltpu.VMEM((q, D_XBC), F32),
               pltpu.VMEM((hp, q), F32), pltpu.VMEM((hp, q), BF16)]

    fwd = lambda i: i
    ypt, sf = pl.pallas_call(
        functools.partial(_ssd_fwd_kernel, cfg=cfg),
        grid=(n_chunks,),
        in_specs=_halo_specs(D_XBC, q, HALO_A, t_tok, fwd) + [
            pl.BlockSpec((q, DT_PAD), lambda i: (i, 0)),
            small((CONV_A, D_XBC)), small((1, D_XBC)),
            small((2 * N_HEADS_A, q)), small((2 * N_HEADS_A, q)),
            pl.BlockSpec((None, None, hp, D_STATE), lambda i: (s_seq(i), 0, 0, 0)),
        ],
        out_specs=[pl.BlockSpec((hp, q), lambda i: (0, i)),
                   pl.BlockSpec((None, hp, D_STATE), lambda i: (p_seq(i), 0, 0))],
        out_shape=[jax.ShapeDtypeStruct((hp, t_tok), F32),
                   jax.ShapeDtypeStruct((cfg.batch, hp, D_STATE), F32)],
        scratch_shapes=scratch + [pltpu.VMEM((hp, D_STATE), F32)],
        compiler_params=_params(("arbitrary",)),
        name="ssd_fwd",
    )(xbc, xbc, xbc, dt, conv_w, conv_b, dtb, alog, s0)

    rev = lambda i: n_chunks - 1 - i
    ya, sb = pl.pallas_call(
        functools.partial(_ssd_bwd_kernel, cfg=cfg, n_chunks=n_chunks),
        grid=(n_chunks,),
        in_specs=_halo_specs(D_XBC, q, HALO_A, t_tok, rev) + [
            pl.BlockSpec((q, DT_PAD), lambda i: (rev(i), 0)),
            pl.BlockSpec((q, D_INNER_A), lambda i: (rev(i), 0)),
            pl.BlockSpec((hp, q), lambda i: (0, rev(i))),
            small((CONV_A, D_XBC)), small((1, D_XBC)),
            small((2 * N_HEADS_A, q)), small((2 * N_HEADS_A, q)),
            small((1, D_INNER_A)), small((1, D_INNER_A)),
            pl.BlockSpec((None, None, hp, D_STATE), lambda i: (s_seq(rev(i)), 1, 0, 0)),
        ],
        out_specs=[pl.BlockSpec((q, D_INNER_A), lambda i: (rev(i), 0)),
                   pl.BlockSpec((None, hp, D_STATE), lambda i: (p_seq(rev(i)), 0, 0))],
        out_shape=[jax.ShapeDtypeStruct((t_tok, D_INNER_A), BF16),
                   jax.ShapeDtypeStruct((cfg.batch, hp, D_STATE), F32)],
        scratch_shapes=scratch + [pltpu.VMEM((hp, q), F32), pltpu.VMEM((q, D_INNER_A), F32),
                                  pltpu.VMEM((hp, D_STATE), F32)],
        compiler_params=_params(("arbitrary",)),
        name="ssd_bwd",
    )(xbc, xbc, xbc, dt, z, ypt, conv_w, conv_b, dtb, alog, dsk, norm_g, s0)
    return ya, sf, sb


def _even_out_kernel(ya_ref, pm_ref, pp_ref, pn_ref, x_ref, gt_ref, pw_ref, ps_ref, woa_ref, wop_ref,
                     o_ref, ext_ref, yp_ref, *, cfg):
    tm = TM_MIX
    t = pl.program_id(0)
    info = _tile_info(t, tm, cfg)
    r = _mod_row(t, tm, cfg)
    zero = jnp.zeros((HALO_A, D_POOL), F32)
    ext_ref[0:HALO_A, :] = jnp.where(info["is_start"], zero, pp_ref[...])
    ext_ref[HALO_A:HALO_A + tm, :] = pm_ref[...]
    ext_ref[HALO_A + tm:, :] = jnp.where(info["is_end"], zero, pn_ref[...])
    pos = info["pos"] * tm + lax.broadcasted_iota(jnp.int32, (tm, POOL_GROUP_DIM), 0)
    for gi, win in enumerate(POOL_WINDOWS):
        sl = slice(gi * POOL_GROUP_DIM, (gi + 1) * POOL_GROUP_DIM)
        lo = HALO_A - win // 2
        s = ext_ref[lo:lo + tm, sl]
        for j in range(1, win):
            s = s + ext_ref[lo + j:lo + j + tm, sl]
        cnt = (jnp.minimum(pos - win // 2 + win, info["seq_len"]) - jnp.maximum(pos - win // 2, 0)).astype(F32)
        pooled = s / cnt - ext_ref[HALO_A:HALO_A + tm, sl]
        yp_ref[:, sl] = (_dot(pooled.astype(BF16), pw_ref[gi]) * ps_ref[:, sl]).astype(BF16)
    out = _dot(ya_ref[...], woa_ref[...]) + _dot(yp_ref[...], wop_ref[...])
    o_ref[...] = x_ref[...] + gt_ref[pl.ds(r, 1), :] * out


def _even_out(ya, pool_in, x, mod, i, pool_w, pool_scale, wo_a, wo_p, cfg):
    t_tok = x.shape[0]
    tm = TM_MIX
    small = lambda shape: pl.BlockSpec(shape, lambda t: (0,) * len(shape))
    return pl.pallas_call(
        functools.partial(_even_out_kernel, cfg=cfg),
        grid=(t_tok // tm,),
        in_specs=[pl.BlockSpec((tm, D_INNER_A), lambda t: (t, 0))]
        + _halo_specs(D_POOL, tm, HALO_A, t_tok, lambda t: t) + [
            pl.BlockSpec((tm, D_MODEL), lambda t: (t, 0)),
            _mod_spec(i, 5),
            small((len(POOL_WINDOWS), POOL_GROUP_DIM, POOL_GROUP_DIM)), small((1, D_POOL)),
            small((D_INNER_A, D_MODEL)), small((D_POOL, D_MODEL)),
        ],
        out_specs=pl.BlockSpec((tm, D_MODEL), lambda t: (t, 0)),
        out_shape=jax.ShapeDtypeStruct((t_tok, D_MODEL), F32),
        scratch_shapes=[pltpu.VMEM((tm + 2 * HALO_A, D_POOL), F32), pltpu.VMEM((tm, D_POOL), BF16)],
        compiler_params=_params(("parallel",)),
        name="even_out",
    )(ya, pool_in, pool_in, pool_in, x, mod, pool_w, pool_scale, wo_a, wo_p)


def _odd_out_kernel(u_ref, v_ref, gm_ref, gp_ref, gn_ref, x_ref, gt_ref, spw_ref, spb_ref, dww_ref, dwb_ref,
                    cg_ref, cb_ref, woc_ref, wod_ref, o_ref, ext_ref, yc_ref, yd_ref, *, cfg):
    tm = TM_MIX
    t = pl.program_id(0)
    info = _tile_info(t, tm, cfg)
    r = _mod_row(t, tm, cfg)
    hc = D_C // N_HEADS_C
    for ck in range(tm // MLP_CHUNK):
        rows = slice(ck * MLP_CHUNK, (ck + 1) * MLP_CHUNK)
        for h in range(N_HEADS_C):
            sl = slice(h * hc, (h + 1) * hc)
            sv = _dot(spw_ref[h], v_ref[rows, sl]) + spb_ref[:, sl]
            yc_ref[rows, sl] = (u_ref[rows, sl] * sv).astype(BF16)
    zero = jnp.zeros((HALO_D, D_D), F32)
    ext_ref[0:HALO_D, :] = jnp.where(info["is_start"], zero, gp_ref[...])
    ext_ref[HALO_D:HALO_D + tm, :] = gm_ref[...]
    ext_ref[HALO_D + tm:, :] = jnp.where(info["is_end"], zero, gn_ref[...])
    base = HALO_D - CONV_D // 2
    rb = MLP_CHUNK
    for j in range(D_D // LANES):
        sl = slice(j * LANES, (j + 1) * LANES)
        for rr in range(tm // rb):
            acc = dww_ref[0:1, sl] * ext_ref[base + rr * rb:base + rr * rb + rb, sl]
            for k in range(1, CONV_D):
                acc = acc + dww_ref[k:k + 1, sl] * ext_ref[base + rr * rb + k:base + rr * rb + k + rb, sl]
            yd_ref[rr * rb:(rr + 1) * rb, sl] = acc + dwb_ref[:, sl]
    ydn = _silu(_layer_norm(yd_ref[...], cg_ref[...], cb_ref[...])).astype(BF16)
    out = _dot(yc_ref[...], woc_ref[...]) + _dot(ydn, wod_ref[...])
    o_ref[...] = x_ref[...] + gt_ref[pl.ds(r, 1), :] * out


def _odd_out(u, v, glu, x, mod, i, sp_w, sp_b_full, dw_w, dw_b, cn_g, cn_b, wo_c, wo_d, cfg):
    t_tok = x.shape[0]
    tm = TM_MIX
    small = lambda shape: pl.BlockSpec(shape, lambda t: (0,) * len(shape))
    return pl.pallas_call(
        functools.partial(_odd_out_kernel, cfg=cfg),
        grid=(t_tok // tm,),
        in_specs=[pl.BlockSpec((tm, D_C), lambda t: (t, 0)), pl.BlockSpec((tm, D_C), lambda t: (t, 0))]
        + _halo_specs(D_D, tm, HALO_D, t_tok, lambda t: t) + [
            pl.BlockSpec((tm, D_MODEL), lambda t: (t, 0)),
            _mod_spec(i, 5),
            small((N_HEADS_C, MLP_CHUNK, MLP_CHUNK)), small((MLP_CHUNK, D_C)),
            small((CONV_D, D_D)), small((1, D_D)), small((1, D_D)), small((1, D_D)),
            small((D_C, D_MODEL)), small((D_D, D_MODEL)),
        ],
        out_specs=pl.BlockSpec((tm, D_MODEL), lambda t: (t, 0)),
        out_shape=jax.ShapeDtypeStruct((t_tok, D_MODEL), F32),
        scratch_shapes=[pltpu.VMEM((tm + 2 * HALO_D, D_D), F32), pltpu.VMEM((tm, D_C), BF16),
                        pltpu.VMEM((tm, D_D), F32)],
        compiler_params=_params(("parallel",)),
        name="odd_out",
    )(u, v, glu, glu, glu, x, mod, sp_w, sp_b_full, dw_w, dw_b, cn_g, cn_b, wo_c, wo_d)


def _grid_position_code(n_tokens):
    rows = n_tokens // GRID_W
    quarter = D_MODEL // 4
    freqs = jnp.exp(-math.log(POS_BASE) * jnp.arange(quarter, dtype=F32) / quarter)
    row = jnp.broadcast_to(jnp.arange(rows, dtype=F32)[:, None], (rows, GRID_W)).reshape(-1)
    col = jnp.broadcast_to(jnp.arange(GRID_W, dtype=F32)[None, :], (rows, GRID_W)).reshape(-1)
    ang_r = row[:, None] * freqs
    ang_c = col[:, None] * freqs
    return jnp.concatenate([jnp.sin(ang_r), jnp.cos(ang_r), jnp.sin(ang_c), jnp.cos(ang_c)], axis=-1)


def _forward(cfg, x_prompt, x_sample, c, state_ssd, c_ctx, w_mod, b_mod, norm_g, ffn_w_gate, ffn_w_up,
             ffn_w_down, ev_w_in, ev_conv_w, ev_conv_b, ev_dt_bias, ev_a_log, ev_d_skip, ev_ssd_norm_g,
             ev_pool_w, ev_pool_scale, ev_w_out, od_w_in, od_v_ln_g, od_v_ln_b, od_sp_w, od_sp_b, od_dw_w,
             od_dw_b, od_cn_g, od_cn_b, od_w_out, final_norm_g):
    t_p = cfg.batch * cfg.seq
    hp = N_HEADS_A * HEAD_DIM_A
    x_lat = x_sample + _grid_position_code(cfg.dec_seq)[None]
    x = jnp.concatenate([x_prompt.reshape(t_p, D_MODEL), x_lat.reshape(-1, D_MODEL)], axis=0)

    cond8 = jnp.concatenate([c_ctx[None, :], c, jnp.zeros((SUBLANES - 1 - cfg.dec_batch, D_MODEL), F32)], axis=0)
    mod = _modulation(cond8, w_mod, b_mod)

    finals = []
    for i in range(DEPTH):
        j = i // 2
        x = _ffn(x, mod, i, 0, norm_g[i, 0], ffn_w_gate[i, 0].astype(BF16), ffn_w_up[i, 0].astype(BF16),
                 ffn_w_down[i, 0].astype(BF16), cfg)
        if i % 2 == 0:
            w = ev_w_in[j]
            o_dt = D_INNER_A + D_XBC
            o_pool = o_dt + 2 * N_HEADS_A
            w_all = jnp.concatenate(
                [w[:, :o_dt], w[:, o_pool:], w[:, o_dt:o_pool],
                 jnp.zeros((D_MODEL, DT_PAD - 2 * N_HEADS_A), F32)], axis=1).astype(BF16)
            z, xbc, pool_in, dt = _even_in(x, mod, i, norm_g[i, 1], w_all, cfg)
            lane_b = lambda v: jnp.broadcast_to(v.reshape(2 * N_HEADS_A, 1), (2 * N_HEADS_A, SSD_CHUNK))
            dsk = jnp.broadcast_to(ev_d_skip[j][:, None], (N_HEADS_A, HEAD_DIM_A)).reshape(1, D_INNER_A)
            s0 = state_ssd[:, j].reshape(cfg.dec_batch, 2, hp, D_STATE)
            ya, sf, sb = _ssd(xbc, dt, z, ev_conv_w[j], ev_conv_b[j].reshape(1, D_XBC), lane_b(ev_dt_bias[j]),
                              lane_b(ev_a_log[j]), dsk, ev_ssd_norm_g[j].reshape(1, D_INNER_A), s0, cfg)
            finals.append(jnp.stack([sf, sb], axis=1))
            x = _even_out(ya, pool_in, x, mod, i, ev_pool_w[j].astype(BF16), ev_pool_scale[j].reshape(1, D_POOL),
                          ev_w_out[j, :D_INNER_A].astype(BF16), ev_w_out[j, D_INNER_A:].astype(BF16), cfg)
        else:
            u, v, glu = _odd_in(x, mod, i, norm_g[i, 1], od_w_in[j].astype(BF16), od_v_ln_g[j], od_v_ln_b[j], cfg)
            spb = jnp.broadcast_to(od_sp_b[j].T[:, :, None], (MLP_CHUNK, N_HEADS_C, D_C // N_HEADS_C))
            x = _odd_out(u, v, glu, x, mod, i, od_sp_w[j].astype(BF16), spb.reshape(MLP_CHUNK, D_C), od_dw_w[j],
                         od_dw_b[j].reshape(1, D_D), od_cn_g[j].reshape(1, D_D), od_cn_b[j].reshape(1, D_D),
                         od_w_out[j, :D_C].astype(BF16), od_w_out[j, D_C:].astype(BF16), cfg)
        x = _ffn(x, mod, i, 6, norm_g[i, 2], ffn_w_gate[i, 1].astype(BF16), ffn_w_up[i, 1].astype(BF16),
                 ffn_w_down[i, 1].astype(BF16), cfg, final_g=final_norm_g if i == DEPTH - 1 else None)

    y_prompt = x[:t_p].reshape(cfg.batch, cfg.seq, D_MODEL)
    y_sample = x[t_p:].reshape(cfg.dec_batch, cfg.dec_seq, D_MODEL)
    new_state = jnp.stack(finals, axis=1).reshape(cfg.batch, N_EVEN, 2, N_HEADS_A, HEAD_DIM_A, D_STATE)
    return y_prompt, y_sample, new_state


def kernel(x_prompt, x_sample, c, state_ssd, c_ctx, w_mod, b_mod, norm_g, ffn_w_gate, ffn_w_up, ffn_w_down, ev_w_in, ev_conv_w, ev_conv_b, ev_dt_bias, ev_a_log, ev_d_skip, ev_ssd_norm_g, ev_pool_w, ev_pool_scale, ev_w_out, od_w_in, od_v_ln_g, od_v_ln_b, od_sp_w, od_sp_b, od_dw_w, od_dw_b, od_cn_g, od_cn_b, od_w_out, final_norm_g):
    cfg = Cfg(x_prompt.shape[0], x_prompt.shape[1], x_sample.shape[0], x_sample.shape[1])
    return _forward(cfg, x_prompt, x_sample, c, state_ssd, c_ctx, w_mod, b_mod, norm_g, ffn_w_gate, ffn_w_up,
                    ffn_w_down, ev_w_in, ev_conv_w, ev_conv_b, ev_dt_bias, ev_a_log, ev_d_skip, ev_ssd_norm_g,
                    ev_pool_w, ev_pool_scale, ev_w_out, od_w_in, od_v_ln_g, od_v_ln_b, od_sp_w, od_sp_b, od_dw_w,
                    od_dw_b, od_cn_g, od_cn_b, od_w_out, final_norm_g)
```

```python
import collections
import functools
import math

import jax
import jax.numpy as jnp
from jax import lax
from jax.experimental import pallas as pl
from jax.experimental.pallas import tpu as pltpu

F32 = jnp.float32
BF16 = jnp.bfloat16

D_MODEL = 1024
DEPTH = 4
GRID_W = 64
POS_BASE = 10000.0
EPS = 1e-6
N_MOD = 9
D_FF = 2816
N_HEADS_A = 16
HEAD_DIM_A = 64
D_INNER_A = N_HEADS_A * HEAD_DIM_A
N_GROUPS_A = 2
HEADS_PER_GROUP = N_HEADS_A // N_GROUPS_A
D_STATE = 128
CONV_A = 5
SSD_CHUNK = 128
D_BC = N_GROUPS_A * D_STATE
D_XBC = D_INNER_A + 2 * D_BC
POOL_WINDOWS = (2, 4, 8, 16)
POOL_GROUP_DIM = 128
D_POOL = len(POOL_WINDOWS) * POOL_GROUP_DIM
N_HEADS_C = 8
MLP_CHUNK = 128
D_C = 1024
D_D = 1024
CONV_D = 31
N_EVEN = (DEPTH + 1) // 2

LANES = 128
SUBLANES = 8
DT_PAD = LANES
HALO_A = SUBLANES
HALO_D = 2 * SUBLANES
VMEM_LIMIT = 56 * 1024 * 1024

TM_FFN = 512
TF_FFN = 256
TM_PROJ = 512
TM_MIX = 256

Cfg = collections.namedtuple("Cfg", "batch seq dec_batch dec_seq")


def _tile_info(t, tm, cfg):
    npt = cfg.seq // tm
    nst = cfg.dec_seq // tm
    n_p = cfg.batch * npt
    is_prompt = t < n_p
    ts = jnp.maximum(t - n_p, 0)
    pos = jnp.where(is_prompt, lax.rem(t, npt), lax.rem(ts, nst))
    last = jnp.where(is_prompt, npt - 1, nst - 1)
    info = dict(
        is_prompt=is_prompt,
        is_start=pos == 0,
        is_end=pos == last,
        pos=pos,
        seq_len=jnp.where(is_prompt, cfg.seq, cfg.dec_seq),
    )
    return info


def _mod_row(t, tm, cfg):
    n_p = (cfg.batch * cfg.seq) // tm
    per = cfg.dec_seq // tm
    return jnp.where(t < n_p, 0, 1 + lax.div(jnp.maximum(t - n_p, 0), per))


def _sigmoid(x):
    return 0.5 + 0.5 * jnp.tanh(0.5 * x)


def _silu(x):
    h = 0.5 * x
    return h + h * jnp.tanh(h)


def _softplus(x):
    return jnp.maximum(x, 0.0) + jnp.log1p(jnp.exp(-jnp.abs(x)))


def _rms(x, g):
    return x * lax.rsqrt(jnp.mean(x * x, axis=-1, keepdims=True) + EPS) * g


def _layer_norm(x, g, b):
    xc = x - jnp.mean(x, axis=-1, keepdims=True)
    y = xc * lax.rsqrt(jnp.mean(xc * xc, axis=-1, keepdims=True) + EPS)
    return y * g + b


def _dot(a, b):
    return jnp.dot(a, b, preferred_element_type=F32)


def _dot_nt(a, b):
    return lax.dot_general(a, b, (((1,), (1,)), ((), ())), preferred_element_type=F32)


def _params(sem):
    return pltpu.CompilerParams(dimension_semantics=sem, vmem_limit_bytes=VMEM_LIMIT)


def _mod_kernel(c_ref, w_ref, b_ref, o_ref):
    c = c_ref[...]
    sc = _silu(c).astype(BF16)
    o_ref[...] = _dot(sc, w_ref[...].astype(BF16)) + b_ref[...]


def _modulation(cond8, w_mod, b_mod):
    b4 = b_mod.reshape(DEPTH, N_MOD, 1, D_MODEL)
    return pl.pallas_call(
        _mod_kernel,
        grid=(DEPTH, N_MOD),
        in_specs=[
            pl.BlockSpec((SUBLANES, D_MODEL), lambda i, k: (0, 0)),
            pl.BlockSpec((None, D_MODEL, D_MODEL), lambda i, k: (i, 0, k)),
            pl.BlockSpec((None, None, 1, D_MODEL), lambda i, k: (i, k, 0, 0)),
        ],
        out_specs=pl.BlockSpec((None, None, SUBLANES, D_MODEL), lambda i, k: (i, k, 0, 0)),
        out_shape=jax.ShapeDtypeStruct((DEPTH, N_MOD, SUBLANES, D_MODEL), F32),
        compiler_params=_params(("arbitrary", "arbitrary")),
        name="modulation",
    )(cond8, w_mod, b4)


def _mod_spec(i, k):
    return pl.BlockSpec((None, None, SUBLANES, D_MODEL), lambda t: (i, k, 0, 0))


def _ffn_body(x, r, sh_ref, sc_ref, gt_ref, g_ref, wg_ref, wu_ref, wd_ref, hn_ref, act_ref):
    hn = _rms(x, g_ref[...]) * (1.0 + sc_ref[pl.ds(r, 1), :]) + sh_ref[pl.ds(r, 1), :]
    hn_ref[...] = hn.astype(BF16)
    for f in range(D_FF // TF_FFN):
        sl = slice(f * TF_FFN, (f + 1) * TF_FFN)
        g = _dot(hn_ref[...], wg_ref[:, sl])
        u = _dot(hn_ref[...], wu_ref[:, sl])
        act_ref[:, sl] = (_silu(g) * u).astype(BF16)
    return x + (0.5 * gt_ref[pl.ds(r, 1), :]) * _dot(act_ref[...], wd_ref[...])


def _ffn_kernel(x_ref, sh_ref, sc_ref, gt_ref, g_ref, wg_ref, wu_ref, wd_ref, o_ref, hn_ref, act_ref, *, cfg):
    r = _mod_row(pl.program_id(0), TM_FFN, cfg)
    o_ref[...] = _ffn_body(x_ref[...], r, sh_ref, sc_ref, gt_ref, g_ref, wg_ref, wu_ref, wd_ref, hn_ref, act_ref)


def _ffn_first_kernel(xp_ref, xs_ref, rt_ref, ct_ref, sh_ref, sc_ref, gt_ref, g_ref, wg_ref, wu_ref, wd_ref,
                      o_ref, x_ref, hn_ref, act_ref, *, cfg):
    t = pl.program_id(0)
    n_p = (cfg.batch * cfg.seq) // TM_FFN
    is_prompt = t < n_p

    @pl.when(is_prompt)
    def _():
        x_ref[...] = xp_ref[...]

    @pl.when(jnp.logical_not(is_prompt))
    def _():
        half = D_MODEL // 2
        row0 = lax.rem(jnp.maximum(t - n_p, 0), cfg.dec_seq // TM_FFN) * (TM_FFN // GRID_W)
        for rr in range(TM_FFN // GRID_W):
            rows = slice(rr * GRID_W, (rr + 1) * GRID_W)
            x_ref[rows, 0:half] = xs_ref[rows, 0:half] + rt_ref[pl.ds(row0 + rr, 1), :]
            x_ref[rows, half:D_MODEL] = xs_ref[rows, half:D_MODEL] + ct_ref[...]

    r = _mod_row(t, TM_FFN, cfg)
    o_ref[...] = _ffn_body(x_ref[...], r, sh_ref, sc_ref, gt_ref, g_ref, wg_ref, wu_ref, wd_ref, hn_ref, act_ref)


def _ffn_last_kernel(x_ref, sh_ref, sc_ref, gt_ref, g_ref, wg_ref, wu_ref, wd_ref, fg_ref, op_ref, os_ref,
                     hn_ref, act_ref, *, cfg):
    t = pl.program_id(0)
    r = _mod_row(t, TM_FFN, cfg)
    xn = _ffn_body(x_ref[...], r, sh_ref, sc_ref, gt_ref, g_ref, wg_ref, wu_ref, wd_ref, hn_ref, act_ref)
    y = _rms(xn, fg_ref[...])
    is_prompt = t < (cfg.batch * cfg.seq) // TM_FFN

    @pl.when(is_prompt)
    def _():
        op_ref[...] = y

    @pl.when(jnp.logical_not(is_prompt))
    def _():
        os_ref[...] = y


def _resident(shape, index):
    return pl.BlockSpec(shape, lambda t: index, pipeline_mode=pl.Buffered(1))


def _ffn_specs(i, which, k0):
    return [
        _mod_spec(i, k0), _mod_spec(i, k0 + 1), _mod_spec(i, k0 + 2),
        pl.BlockSpec((None, None, 1, D_MODEL), lambda t: (i, 2 * which, 0, 0)),
        _resident((None, None, D_MODEL, D_FF), (i, which, 0, 0)),
        _resident((None, None, D_MODEL, D_FF), (i, which, 0, 0)),
        _resident((None, None, D_FF, D_MODEL), (i, which, 0, 0)),
    ]


def _ffn_scratch():
    return [pltpu.VMEM((TM_FFN, D_MODEL), BF16), pltpu.VMEM((TM_FFN, D_FF), BF16)]


def _ffn(x, mod, i, which, norm_g4, wg, wu, wd, cfg):
    t_tok = x.shape[0]
    row = pl.BlockSpec((TM_FFN, D_MODEL), lambda t: (t, 0))
    return pl.pallas_call(
        functools.partial(_ffn_kernel, cfg=cfg),
        grid=(t_tok // TM_FFN,),
        in_specs=[row] + _ffn_specs(i, which, 6 * which),
        out_specs=row,
        out_shape=jax.ShapeDtypeStruct((t_tok, D_MODEL), F32),
        scratch_shapes=_ffn_scratch(),
        compiler_params=_params(("parallel",)),
        name="ffn",
    )(x, mod, mod, mod, norm_g4, wg, wu, wd)


def _ffn_first(xp, xs, row_tab, col_tab, mod, norm_g4, wg, wu, wd, cfg):
    n_p = xp.shape[0] // TM_FFN
    t_tok = xp.shape[0] + xs.shape[0]
    half = D_MODEL // 2
    return pl.pallas_call(
        functools.partial(_ffn_first_kernel, cfg=cfg),
        grid=(t_tok // TM_FFN,),
        in_specs=[
            pl.BlockSpec((TM_FFN, D_MODEL), lambda t: (jnp.minimum(t, n_p - 1), 0)),
            pl.BlockSpec((TM_FFN, D_MODEL), lambda t: (jnp.maximum(t - n_p, 0), 0)),
            pl.BlockSpec((cfg.dec_seq // GRID_W, half), lambda t: (0, 0)),
            pl.BlockSpec((GRID_W, half), lambda t: (0, 0)),
        ] + _ffn_specs(0, 0, 0),
        out_specs=pl.BlockSpec((TM_FFN, D_MODEL), lambda t: (t, 0)),
        out_shape=jax.ShapeDtypeStruct((t_tok, D_MODEL), F32),
        scratch_shapes=[pltpu.VMEM((TM_FFN, D_MODEL), F32)] + _ffn_scratch(),
        compiler_params=_params(("parallel",)),
        name="ffn_first",
    )(xp, xs, row_tab, col_tab, mod, mod, mod, norm_g4, wg, wu, wd)


def _ffn_last(x, mod, norm_g4, wg, wu, wd, final_g, cfg):
    i = DEPTH - 1
    t_p = cfg.batch * cfg.seq
    n_p = t_p // TM_FFN
    return pl.pallas_call(
        functools.partial(_ffn_last_kernel, cfg=cfg),
        grid=(x.shape[0] // TM_FFN,),
        in_specs=[pl.BlockSpec((TM_FFN, D_MODEL), lambda t: (t, 0))] + _ffn_specs(i, 1, 6)
        + [pl.BlockSpec((1, D_MODEL), lambda t: (0, 0))],
        out_specs=[pl.BlockSpec((TM_FFN, D_MODEL), lambda t: (jnp.minimum(t, n_p - 1), 0)),
                   pl.BlockSpec((TM_FFN, D_MODEL), lambda t: (jnp.maximum(t - n_p, 0), 0))],
        out_shape=[jax.ShapeDtypeStruct((t_p, D_MODEL), F32),
                   jax.ShapeDtypeStruct((x.shape[0] - t_p, D_MODEL), F32)],
        scratch_shapes=_ffn_scratch(),
        compiler_params=_params(("arbitrary",)),
        name="ffn_last",
    )(x, mod, mod, mod, norm_g4, wg, wu, wd, final_g.reshape(1, D_MODEL))


def _norm_mod(x_ref, sh_ref, sc_ref, g_ref, cfg):
    r = _mod_row(pl.program_id(0), TM_PROJ, cfg)
    hn = _rms(x_ref[...], g_ref[...]) * (1.0 + sc_ref[pl.ds(r, 1), :]) + sh_ref[pl.ds(r, 1), :]
    return hn.astype(BF16)


def _even_in_kernel(x_ref, sh_ref, sc_ref, g_ref, w_ref, z_ref, xbc_ref, pool_ref, dt_ref, *, cfg):
    hn = _norm_mod(x_ref, sh_ref, sc_ref, g_ref, cfg)
    o = 0
    for ref, width in ((z_ref, D_INNER_A), (xbc_ref, D_XBC), (pool_ref, D_POOL), (dt_ref, DT_PAD)):
        ref[...] = _dot(hn, w_ref[:, o:o + width])
        o += width


def _even_in(x, mod, i, g, w, cfg):
    t_tok = x.shape[0]
    widths = (D_INNER_A, D_XBC, D_POOL, DT_PAD)
    n_tot = sum(widths)
    return pl.pallas_call(
        functools.partial(_even_in_kernel, cfg=cfg),
        grid=(t_tok // TM_PROJ,),
        in_specs=[
            pl.BlockSpec((TM_PROJ, D_MODEL), lambda t: (t, 0)),
            _mod_spec(i, 3), _mod_spec(i, 4),
            pl.BlockSpec((1, D_MODEL), lambda t: (0, 0)),
            pl.BlockSpec((D_MODEL, n_tot), lambda t: (0, 0)),
        ],
        out_specs=[pl.BlockSpec((TM_PROJ, wd), lambda t: (t, 0)) for wd in widths],
        out_shape=[jax.ShapeDtypeStruct((t_tok, wd), F32) for wd in widths],
        compiler_params=_params(("parallel",)),
        name="even_in",
    )(x, mod, mod, g.reshape(1, D_MODEL), w)


def _odd_in_kernel(x_ref, sh_ref, sc_ref, g_ref, w_ref, lg_ref, lb_ref, u_ref, v_ref, glu_ref, *, cfg):
    hn = _norm_mod(x_ref, sh_ref, sc_ref, g_ref, cfg)
    u_ref[...] = jax.nn.gelu(_dot(hn, w_ref[:, 0:D_C]))
    v = jax.nn.gelu(_dot(hn, w_ref[:, D_C:2 * D_C]))
    v_ref[...] = _layer_norm(v, lg_ref[...], lb_ref[...]).astype(BF16)
    ga = _dot(hn, w_ref[:, 2 * D_C:2 * D_C + D_D])
    gg = _dot(hn, w_ref[:, 2 * D_C + D_D:2 * D_C + 2 * D_D])
    glu_ref[...] = ga * _sigmoid(gg)


def _odd_in(x, mod, i, g, w, ln_g, ln_b, cfg):
    t_tok = x.shape[0]
    row = lambda: pl.BlockSpec((1, D_MODEL), lambda t: (0, 0))
    return pl.pallas_call(
        functools.partial(_odd_in_kernel, cfg=cfg),
        grid=(t_tok // TM_PROJ,),
        in_specs=[
            pl.BlockSpec((TM_PROJ, D_MODEL), lambda t: (t, 0)),
            _mod_spec(i, 3), _mod_spec(i, 4),
            row(),
            pl.BlockSpec((D_MODEL, 2 * D_C + 2 * D_D), lambda t: (0, 0)),
            row(), row(),
        ],
        out_specs=[pl.BlockSpec((TM_PROJ, D_C), lambda t: (t, 0))] * 3,
        out_shape=[jax.ShapeDtypeStruct((t_tok, D_C), F32),
                   jax.ShapeDtypeStruct((t_tok, D_C), BF16),
                   jax.ShapeDtypeStruct((t_tok, D_D), F32)],
        compiler_params=_params(("parallel",)),
        name="odd_in",
    )(x, mod, mod, g.reshape(1, D_MODEL), w, ln_g.reshape(1, D_C), ln_b.reshape(1, D_C))


def _conv_silu(xm_ref, xp_ref, xn_ref, cw_ref, cb_ref, ext_ref, xbc_ref, is_start, is_end):
    q = SSD_CHUNK
    zero = jnp.zeros((HALO_A, D_XBC), F32)
    ext_ref[0:HALO_A, :] = jnp.where(is_start, zero, xp_ref[...])
    ext_ref[HALO_A:HALO_A + q, :] = xm_ref[...]
    ext_ref[HALO_A + q:, :] = jnp.where(is_end, zero, xn_ref[...])
    base = HALO_A - CONV_A // 2
    for j in range(D_XBC // LANES):
        sl = slice(j * LANES, (j + 1) * LANES)
        acc = cw_ref[0:1, sl] * ext_ref[base:base + q, sl]
        for k in range(1, CONV_A):
            acc = acc + cw_ref[k:k + 1, sl] * ext_ref[base + k:base + k + q, sl]
        xbc_ref[:, sl] = _silu(acc + cb_ref[:, sl])


def _dt_terms(dt_ref, dtb_ref, alog_ref, rows):
    dtt = dt_ref[...].T[rows, :]
    dts = _softplus(dtt + dtb_ref[rows, :])
    return dts, dts * (-jnp.exp(alog_ref[rows, :]))


def _tri(lower_incl):
    ri = lax.broadcasted_iota(jnp.int32, (SSD_CHUNK, SSD_CHUNK), 0)
    ci = lax.broadcasted_iota(jnp.int32, (SSD_CHUNK, SSD_CHUNK), 1)
    return (ri >= ci) if lower_incl else (ri <= ci)


def _cumsum_lanes(v, mask):
    return jnp.dot(v, mask.astype(F32), precision=lax.Precision.HIGHEST, preferred_element_type=F32)


def _group_bc(bc_ref, g):
    b = bc_ref[:, g * D_STATE:(g + 1) * D_STATE]
    c = bc_ref[:, D_BC + g * D_STATE:D_BC + (g + 1) * D_STATE]
    return b, c


def _state_update(s_ref, xst_ref, xd_ref, bc_ref, scale, cdec):
    hd = HEAD_DIM_A
    for h in range(N_HEADS_A):
        xd_ref[h * hd:(h + 1) * hd, :] = (xst_ref[h * hd:(h + 1) * hd, :] * scale[h:h + 1, :]).astype(BF16)
    gw = HEADS_PER_GROUP * hd
    for g in range(N_GROUPS_A):
        b, _ = _group_bc(bc_ref, g)
        upd = _dot(xd_ref[g * gw:(g + 1) * gw, :], b)
        for e in range(HEADS_PER_GROUP):
            h = g * HEADS_PER_GROUP + e
            s_ref[h * hd:(h + 1) * hd, :] = (s_ref[h * hd:(h + 1) * hd, :] * cdec[h:h + 1, :]
                                             + upd[e * hd:(e + 1) * hd, :])


def _ssd_fwd_kernel(xm_ref, xp_ref, xn_ref, dt_ref, cw_ref, cb_ref, dtb_ref, alog_ref, dsk_ref, s0_ref,
                    ypt_ref, xst_ref, bc_ref, sfin_ref, ext_ref, xbc_ref, xd_ref, s_ref, *, cfg):
    q, hd, nh = SSD_CHUNK, HEAD_DIM_A, N_HEADS_A
    info = _tile_info(pl.program_id(0), q, cfg)

    @pl.when(jnp.logical_and(info["is_start"], info["is_prompt"]))
    def _():
        s_ref[...] = jnp.zeros_like(s_ref)

    @pl.when(jnp.logical_and(info["is_start"], jnp.logical_not(info["is_prompt"])))
    def _():
        s_ref[...] = s0_ref[...]

    _conv_silu(xm_ref, xp_ref, xn_ref, cw_ref, cb_ref, ext_ref, xbc_ref, info["is_start"], info["is_end"])
    for j in range(D_INNER_A // LANES):
        xst_ref[j * LANES:(j + 1) * LANES, :] = xbc_ref[:, j * LANES:(j + 1) * LANES].T
    bc_ref[...] = xbc_ref[:, D_INNER_A:].astype(BF16)

    dts, dta = _dt_terms(dt_ref, dtb_ref, alog_ref, slice(0, 2 * nh))
    low, upp = _tri(True), _tri(False)
    acs_f = _cumsum_lanes(dta[0:nh], upp)
    rcs_b = _cumsum_lanes(dta[nh:2 * nh], low)
    src = jnp.concatenate([acs_f - jnp.log(dts[0:nh]), rcs_b - jnp.log(dts[nh:2 * nh]),
                           jnp.zeros((LANES - 2 * nh, q), F32)], axis=0).T
    e_acs = jnp.exp(acs_f)
    neg_inf = jnp.float32(-jnp.inf)

    gw = HEADS_PER_GROUP * hd
    for g in range(N_GROUPS_A):
        b, c = _group_bc(bc_ref, g)
        sct = _dot_nt(b, c)
        yoff = _dot_nt(s_ref[g * gw:(g + 1) * gw, :].astype(BF16), c)
        for e in range(HEADS_PER_GROUP):
            h = g * HEADS_PER_GROUP + e
            rows = slice(h * hd, (h + 1) * hd)
            seg_f = acs_f[h:h + 1, :] - src[:, h:h + 1]
            seg_b = rcs_b[h:h + 1, :] - src[:, nh + h:nh + h + 1]
            wt = sct * (jnp.exp(jnp.where(upp, seg_f, neg_inf)) + jnp.exp(jnp.where(low, seg_b, neg_inf)))
            xh = xst_ref[rows, :]
            yd = _dot(xh.astype(BF16), wt.astype(BF16))
            ypt_ref[rows, :] = yd + yoff[e * hd:(e + 1) * hd, :] * e_acs[h:h + 1, :] + dsk_ref[rows, :] * xh

    tot = acs_f[:, q - 1:q]
    scale = dts[0:nh] * jnp.exp(tot - acs_f)
    cdec = jnp.exp(jnp.broadcast_to(tot, (nh, D_STATE)))
    _state_update(s_ref, xst_ref, xd_ref, bc_ref, scale, cdec)

    @pl.when(jnp.logical_and(info["is_end"], info["is_prompt"]))
    def _():
        sfin_ref[...] = s_ref[...]


def _ssd_bwd_kernel(dt_ref, z_ref, ypt_ref, xst_ref, bc_ref, dtb_ref, alog_ref, ng_ref, s0_ref, sf_ref, *rest,
                    cfg, n_chunks, layer):
    if layer == 0:
        ya_ref, so_ref, xd_ref, yt_ref, y_ref, s_ref = rest
    else:
        sprev_ref, ya_ref, so_ref, xd_ref, yt_ref, y_ref, s_ref = rest
    q, hd, nh = SSD_CHUNK, HEAD_DIM_A, N_HEADS_A
    info = _tile_info(n_chunks - 1 - pl.program_id(0), q, cfg)

    @pl.when(jnp.logical_and(info["is_end"], info["is_prompt"]))
    def _():
        s_ref[...] = jnp.zeros_like(s_ref)

    @pl.when(jnp.logical_and(info["is_end"], jnp.logical_not(info["is_prompt"])))
    def _():
        s_ref[...] = s0_ref[...]

    dts, dta = _dt_terms(dt_ref, dtb_ref, alog_ref, slice(nh, 2 * nh))
    rcs_b = _cumsum_lanes(dta, _tri(True))
    e_rcs = jnp.exp(rcs_b)

    gw = HEADS_PER_GROUP * hd
    for g in range(N_GROUPS_A):
        _, c = _group_bc(bc_ref, g)
        yoff = _dot_nt(s_ref[g * gw:(g + 1) * gw, :].astype(BF16), c)
        for e in range(HEADS_PER_GROUP):
            h = g * HEADS_PER_GROUP + e
            yt_ref[h * hd:(h + 1) * hd, :] = (ypt_ref[h * hd:(h + 1) * hd, :]
                                              + yoff[e * hd:(e + 1) * hd, :] * e_rcs[h:h + 1, :])

    for j in range(D_INNER_A // LANES):
        sl = slice(j * LANES, (j + 1) * LANES)
        y_ref[:, sl] = yt_ref[j * LANES:(j + 1) * LANES, :].T * _silu(z_ref[:, sl])
    ya_ref[...] = _rms(y_ref[...], ng_ref[...]).astype(BF16)

    tot = rcs_b[:, 0:1]
    scale = dts * jnp.exp(tot - rcs_b)
    cdec = jnp.exp(jnp.broadcast_to(tot, (nh, D_STATE)))
    _state_update(s_ref, xst_ref, xd_ref, bc_ref, scale, cdec)

    @pl.when(jnp.logical_and(info["is_start"], info["is_prompt"]))
    def _():
        if layer == 0:
            so_ref[0] = sf_ref[...]
            so_ref[1] = s_ref[...]
        else:
            so_ref[0] = sprev_ref[...]
            so_ref[1, 0] = sf_ref[...]
            so_ref[1, 1] = s_ref[...]


def _halo_specs(width, tile, halo, n_rows, chunk_of):
    per = tile // halo
    last = n_rows // halo - 1
    return [
        pl.BlockSpec((tile, width), lambda i: (chunk_of(i), 0)),
        pl.BlockSpec((halo, width), lambda i: (jnp.maximum(chunk_of(i) * per - 1, 0), 0)),
        pl.BlockSpec((halo, width), lambda i: (jnp.minimum((chunk_of(i) + 1) * per, last), 0)),
    ]


def _ssd(xbc, dt, z, conv_w, conv_b, dtb, alog, dsk_t, norm_g, s0, s_prev, layer, cfg):
    t_tok = xbc.shape[0]
    q = SSD_CHUNK
    n_chunks = t_tok // q
    hp = N_HEADS_A * HEAD_DIM_A
    n_pc = cfg.batch * (cfg.seq // q)
    ncs = cfg.dec_seq // q
    ncp = cfg.seq // q

    def s_seq(c):
        return jnp.clip(lax.div(jnp.maximum(c - n_pc, 0), ncs), 0, cfg.dec_batch - 1)

    def p_seq(c):
        return jnp.minimum(lax.div(c, ncp), cfg.batch - 1)

    small = lambda shape: pl.BlockSpec(shape, lambda i: (0,) * len(shape))
    head_rows = small((2 * N_HEADS_A, q))

    ypt, xst, bc, sf = pl.pallas_call(
        functools.partial(_ssd_fwd_kernel, cfg=cfg),
        grid=(n_chunks,),
        in_specs=_halo_specs(D_XBC, q, HALO_A, t_tok, lambda i: i) + [
            pl.BlockSpec((q, DT_PAD), lambda i: (i, 0)),
            small((CONV_A, D_XBC)), small((1, D_XBC)), head_rows, head_rows, small((hp, q)),
            pl.BlockSpec((None, None, hp, D_STATE), lambda i: (s_seq(i), 0, 0, 0)),
        ],
        out_specs=[pl.BlockSpec((hp, q), lambda i: (0, i)),
                   pl.BlockSpec((hp, q), lambda i: (0, i)),
                   pl.BlockSpec((q, 2 * D_BC), lambda i: (i, 0)),
                   pl.BlockSpec((None, hp, D_STATE), lambda i: (p_seq(i), 0, 0))],
        out_shape=[jax.ShapeDtypeStruct((hp, t_tok), F32),
                   jax.ShapeDtypeStruct((hp, t_tok), F32),
                   jax.ShapeDtypeStruct((t_tok, 2 * D_BC), BF16),
                   jax.ShapeDtypeStruct((cfg.batch, hp, D_STATE), F32)],
        scratch_shapes=[pltpu.VMEM((q + 2 * HALO_A, D_XBC), F32), pltpu.VMEM((q, D_XBC), F32),
                        pltpu.VMEM((hp, q), BF16), pltpu.VMEM((hp, D_STATE), F32)],
        compiler_params=_params(("arbitrary",)),
        name="ssd_fwd",
    )(xbc, xbc, xbc, dt, conv_w, conv_b, dtb, alog, dsk_t, s0)

    rev = lambda i: n_chunks - 1 - i
    in_specs = [
        pl.BlockSpec((q, DT_PAD), lambda i: (rev(i), 0)),
        pl.BlockSpec((q, D_INNER_A), lambda i: (rev(i), 0)),
        pl.BlockSpec((hp, q), lambda i: (0, rev(i))),
        pl.BlockSpec((hp, q), lambda i: (0, rev(i))),
        pl.BlockSpec((q, 2 * D_BC), lambda i: (rev(i), 0)),
        head_rows, head_rows, small((1, D_INNER_A)),
        pl.BlockSpec((None, None, hp, D_STATE), lambda i: (s_seq(rev(i)), 1, 0, 0)),
        pl.BlockSpec((None, hp, D_STATE), lambda i: (p_seq(rev(i)), 0, 0)),
    ]
    args = [dt, z, ypt, xst, bc, dtb, alog, norm_g, s0, sf]
    if layer == 0:
        so_block, so_shape = (None, 2, hp, D_STATE), (cfg.batch, 2, hp, D_STATE)
        so_index = lambda i: (p_seq(rev(i)), 0, 0, 0)
    else:
        in_specs.append(pl.BlockSpec((None, 2, hp, D_STATE), lambda i: (p_seq(rev(i)), 0, 0, 0)))
        args.append(s_prev)
        so_block, so_shape = (None, 2, 2, hp, D_STATE), (cfg.batch, 2, 2, hp, D_STATE)
        so_index = lambda i: (p_seq(rev(i)), 0, 0, 0, 0)
    ya, so = pl.pallas_call(
        functools.partial(_ssd_bwd_kernel, cfg=cfg, n_chunks=n_chunks, layer=layer),
        grid=(n_chunks,),
        in_specs=in_specs,
        out_specs=[pl.BlockSpec((q, D_INNER_A), lambda i: (rev(i), 0)), pl.BlockSpec(so_block, so_index)],
        out_shape=[jax.ShapeDtypeStruct((t_tok, D_INNER_A), BF16), jax.ShapeDtypeStruct(so_shape, F32)],
        scratch_shapes=[pltpu.VMEM((hp, q), BF16), pltpu.VMEM((hp, q), F32), pltpu.VMEM((q, D_INNER_A), F32),
                        pltpu.VMEM((hp, D_STATE), F32)],
        compiler_params=_params(("arbitrary",)),
        name="ssd_bwd",
    )(*args)
    return ya, so


def _even_out_kernel(ya_ref, pm_ref, pp_ref, pn_ref, x_ref, gt_ref, pw_ref, ps_ref, woa_ref, wop_ref,
                     o_ref, ext_ref, yp_ref, *, cfg):
    tm = TM_MIX
    t = pl.program_id(0)
    info = _tile_info(t, tm, cfg)
    r = _mod_row(t, tm, cfg)
    zero = jnp.zeros((HALO_A, D_POOL), F32)
    ext_ref[0:HALO_A, :] = jnp.where(info["is_start"], zero, pp_ref[...])
    ext_ref[HALO_A:HALO_A + tm, :] = pm_ref[...]
    ext_ref[HALO_A + tm:, :] = jnp.where(info["is_end"], zero, pn_ref[...])
    pos = info["pos"] * tm + lax.broadcasted_iota(jnp.int32, (tm, POOL_GROUP_DIM), 0)
    for gi, win in enumerate(POOL_WINDOWS):
        sl = slice(gi * POOL_GROUP_DIM, (gi + 1) * POOL_GROUP_DIM)
        lo = HALO_A - win // 2
        s = ext_ref[lo:lo + tm, sl]
        for j in range(1, win):
            s = s + ext_ref[lo + j:lo + j + tm, sl]
        cnt = (jnp.minimum(pos - win // 2 + win, info["seq_len"]) - jnp.maximum(pos - win // 2, 0)).astype(F32)
        pooled = s / cnt - ext_ref[HALO_A:HALO_A + tm, sl]
        yp_ref[:, sl] = (_dot(pooled.astype(BF16), pw_ref[gi]) * ps_ref[:, sl]).astype(BF16)
    out = _dot(ya_ref[...], woa_ref[...]) + _dot(yp_ref[...], wop_ref[...])
    o_ref[...] = x_ref[...] + gt_ref[pl.ds(r, 1), :] * out


def _even_out(ya, pool_in, x, mod, i, pool_w, pool_scale, wo_a, wo_p, cfg):
    t_tok = x.shape[0]
    tm = TM_MIX
    small = lambda shape: pl.BlockSpec(shape, lambda t: (0,) * len(shape))
    return pl.pallas_call(
        functools.partial(_even_out_kernel, cfg=cfg),
        grid=(t_tok // tm,),
        in_specs=[pl.BlockSpec((tm, D_INNER_A), lambda t: (t, 0))]
        + _halo_specs(D_POOL, tm, HALO_A, t_tok, lambda t: t) + [
            pl.BlockSpec((tm, D_MODEL), lambda t: (t, 0)),
            _mod_spec(i, 5),
            small((len(POOL_WINDOWS), POOL_GROUP_DIM, POOL_GROUP_DIM)), small((1, D_POOL)),
            small((D_INNER_A, D_MODEL)), small((D_POOL, D_MODEL)),
        ],
        out_specs=pl.BlockSpec((tm, D_MODEL), lambda t: (t, 0)),
        out_shape=jax.ShapeDtypeStruct((t_tok, D_MODEL), F32),
        scratch_shapes=[pltpu.VMEM((tm + 2 * HALO_A, D_POOL), F32), pltpu.VMEM((tm, D_POOL), BF16)],
        compiler_params=_params(("parallel",)),
        name="even_out",
    )(ya, pool_in, pool_in, pool_in, x, mod, pool_w, pool_scale, wo_a, wo_p)


def _odd_out_kernel(u_ref, v_ref, gm_ref, gp_ref, gn_ref, x_ref, gt_ref, spw_ref, spb_ref, dww_ref, dwb_ref,
                    cg_ref, cb_ref, woc_ref, wod_ref, o_ref, ext_ref, sh_ref, yc_ref, yd_ref, *, cfg):
    tm = TM_MIX
    t = pl.program_id(0)
    info = _tile_info(t, tm, cfg)
    r = _mod_row(t, tm, cfg)
    hc = D_C // N_HEADS_C
    for ck in range(tm // MLP_CHUNK):
        rows = slice(ck * MLP_CHUNK, (ck + 1) * MLP_CHUNK)
        for h in range(N_HEADS_C):
            sl = slice(h * hc, (h + 1) * hc)
            sv = _dot(spw_ref[h], v_ref[rows, sl]) + spb_ref[:, sl]
            yc_ref[rows, sl] = (u_ref[rows, sl] * sv).astype(BF16)
    zero = jnp.zeros((HALO_D, D_D), F32)
    ext_ref[0:HALO_D, :] = jnp.where(info["is_start"], zero, gp_ref[...])
    ext_ref[HALO_D:HALO_D + tm, :] = gm_ref[...]
    ext_ref[HALO_D + tm:, :] = jnp.where(info["is_end"], zero, gn_ref[...])
    span = tm + 2 * HALO_D - SUBLANES
    base = HALO_D - CONV_D // 2
    rb = MLP_CHUNK
    for j in range(D_D // LANES):
        sl = slice(j * LANES, (j + 1) * LANES)
        for s in range(1, SUBLANES):
            sh_ref[s - 1, :, sl] = ext_ref[s:s + span, sl]
        for rr in range(tm // rb):
            acc = None
            for k in range(CONV_D):
                q8, s = divmod(base + k, SUBLANES)
                r0 = q8 * SUBLANES + rr * rb
                tap = ext_ref[r0:r0 + rb, sl] if s == 0 else sh_ref[s - 1, r0:r0 + rb, sl]
                term = dww_ref[k:k + 1, sl] * tap
                acc = term if acc is None else acc + term
            yd_ref[rr * rb:(rr + 1) * rb, sl] = acc + dwb_ref[:, sl]
    ydn = _silu(_layer_norm(yd_ref[...], cg_ref[...], cb_ref[...])).astype(BF16)
    out = _dot(yc_ref[...], woc_ref[...]) + _dot(ydn, wod_ref[...])
    o_ref[...] = x_ref[...] + gt_ref[pl.ds(r, 1), :] * out


def _odd_out(u, v, glu, x, mod, i, sp_w, sp_b_full, dw_w, dw_b, cn_g, cn_b, wo_c, wo_d, cfg):
    t_tok = x.shape[0]
    tm = TM_MIX
    small = lambda shape: pl.BlockSpec(shape, lambda t: (0,) * len(shape))
    return pl.pallas_call(
        functools.partial(_odd_out_kernel, cfg=cfg),
        grid=(t_tok // tm,),
        in_specs=[pl.BlockSpec((tm, D_C), lambda t: (t, 0)), pl.BlockSpec((tm, D_C), lambda t: (t, 0))]
        + _halo_specs(D_D, tm, HALO_D, t_tok, lambda t: t) + [
            pl.BlockSpec((tm, D_MODEL), lambda t: (t, 0)),
            _mod_spec(i, 5),
            small((N_HEADS_C, MLP_CHUNK, MLP_CHUNK)), small((MLP_CHUNK, D_C)),
            small((CONV_D, D_D)), small((1, D_D)), small((1, D_D)), small((1, D_D)),
            small((D_C, D_MODEL)), small((D_D, D_MODEL)),
        ],
        out_specs=pl.BlockSpec((tm, D_MODEL), lambda t: (t, 0)),
        out_shape=jax.ShapeDtypeStruct((t_tok, D_MODEL), F32),
        scratch_shapes=[pltpu.VMEM((tm + 2 * HALO_D, D_D), F32),
                        pltpu.VMEM((SUBLANES - 1, tm + 2 * HALO_D - SUBLANES, D_D), F32),
                        pltpu.VMEM((tm, D_C), BF16), pltpu.VMEM((tm, D_D), F32)],
        compiler_params=_params(("parallel",)),
        name="odd_out",
    )(u, v, glu, glu, glu, x, mod, sp_w, sp_b_full, dw_w, dw_b, cn_g, cn_b, wo_c, wo_d)


def _position_tables(dec_seq):
    quarter = D_MODEL // 4
    freqs = jnp.exp(-math.log(POS_BASE) * jnp.arange(quarter, dtype=F32) / quarter)
    ang_r = jnp.arange(dec_seq // GRID_W, dtype=F32)[:, None] * freqs
    ang_c = jnp.arange(GRID_W, dtype=F32)[:, None] * freqs
    return (jnp.concatenate([jnp.sin(ang_r), jnp.cos(ang_r)], axis=-1),
            jnp.concatenate([jnp.sin(ang_c), jnp.cos(ang_c)], axis=-1))


def _forward(cfg, x_prompt, x_sample, c, state_ssd, c_ctx, w_mod, b_mod, norm_g, ffn_w_gate, ffn_w_up,
             ffn_w_down, ev_w_in, ev_conv_w, ev_conv_b, ev_dt_bias, ev_a_log, ev_d_skip, ev_ssd_norm_g,
             ev_pool_w, ev_pool_scale, ev_w_out, od_w_in, od_v_ln_g, od_v_ln_b, od_sp_w, od_sp_b, od_dw_w,
             od_dw_b, od_cn_g, od_cn_b, od_w_out, final_norm_g):
    assert DEPTH == 4 and cfg.seq % TM_MIX == 0 and cfg.dec_seq % TM_FFN == 0 and (cfg.batch * cfg.seq) % TM_FFN == 0
    t_p = cfg.batch * cfg.seq
    hp = N_HEADS_A * HEAD_DIM_A
    row_tab, col_tab = _position_tables(cfg.dec_seq)

    cond8 = jnp.concatenate([c_ctx[None, :], c, jnp.zeros((SUBLANES - 1 - cfg.dec_batch, D_MODEL), F32)], axis=0)
    mod = _modulation(cond8, w_mod, b_mod)

    wg, wu, wd = ffn_w_gate.astype(BF16), ffn_w_up.astype(BF16), ffn_w_down.astype(BF16)
    norm_g4 = norm_g.reshape(DEPTH, 3, 1, D_MODEL)
    states = None
    for i in range(DEPTH):
        j = i // 2
        if i == 0:
            x = _ffn_first(x_prompt.reshape(t_p, D_MODEL), x_sample.reshape(-1, D_MODEL), row_tab, col_tab, mod,
                           norm_g4, wg, wu, wd, cfg)
        else:
            x = _ffn(x, mod, i, 0, norm_g4, wg, wu, wd, cfg)
        if i % 2 == 0:
            w = ev_w_in[j]
            o_dt = D_INNER_A + D_XBC
            o_pool = o_dt + 2 * N_HEADS_A
            w_all = jnp.concatenate(
                [w[:, :o_dt], w[:, o_pool:], w[:, o_dt:o_pool],
                 jnp.zeros((D_MODEL, DT_PAD - 2 * N_HEADS_A), F32)], axis=1).astype(BF16)
            z, xbc, pool_in, dt = _even_in(x, mod, i, norm_g[i, 1], w_all, cfg)
            lane_b = lambda v: jnp.broadcast_to(v.reshape(2 * N_HEADS_A, 1), (2 * N_HEADS_A, SSD_CHUNK))
            dsk_t = jnp.broadcast_to(ev_d_skip[j][:, None, None], (N_HEADS_A, HEAD_DIM_A, SSD_CHUNK)).reshape(hp, SSD_CHUNK)
            s0 = state_ssd[:, j].reshape(cfg.dec_batch, 2, hp, D_STATE)
            ya, states = _ssd(xbc, dt, z, ev_conv_w[j], ev_conv_b[j].reshape(1, D_XBC), lane_b(ev_dt_bias[j]),
                              lane_b(ev_a_log[j]), dsk_t, ev_ssd_norm_g[j].reshape(1, D_INNER_A), s0, states, j, cfg)
            x = _even_out(ya, pool_in, x, mod, i, ev_pool_w[j].astype(BF16), ev_pool_scale[j].reshape(1, D_POOL),
                          ev_w_out[j, :D_INNER_A].astype(BF16), ev_w_out[j, D_INNER_A:].astype(BF16), cfg)
        else:
            u, v, glu = _odd_in(x, mod, i, norm_g[i, 1], od_w_in[j].astype(BF16), od_v_ln_g[j], od_v_ln_b[j], cfg)
            spb = jnp.broadcast_to(od_sp_b[j].T[:, :, None], (MLP_CHUNK, N_HEADS_C, D_C // N_HEADS_C))
            x = _odd_out(u, v, glu, x, mod, i, od_sp_w[j].astype(BF16), spb.reshape(MLP_CHUNK, D_C), od_dw_w[j],
                         od_dw_b[j].reshape(1, D_D), od_cn_g[j].reshape(1, D_D), od_cn_b[j].reshape(1, D_D),
                         od_w_out[j, :D_C].astype(BF16), od_w_out[j, D_C:].astype(BF16), cfg)
        if i < DEPTH - 1:
            x = _ffn(x, mod, i, 1, norm_g4, wg, wu, wd, cfg)
        else:
            y_p, y_s = _ffn_last(x, mod, norm_g4, wg, wu, wd, final_norm_g, cfg)

    return (y_p.reshape(cfg.batch, cfg.seq, D_MODEL), y_s.reshape(cfg.dec_batch, cfg.dec_seq, D_MODEL),
            states.reshape(cfg.batch, N_EVEN, 2, N_HEADS_A, HEAD_DIM_A, D_STATE))


def kernel(x_prompt, x_sample, c, state_ssd, c_ctx, w_mod, b_mod, norm_g, ffn_w_gate, ffn_w_up, ffn_w_down, ev_w_in, ev_conv_w, ev_conv_b, ev_dt_bias, ev_a_log, ev_d_skip, ev_ssd_norm_g, ev_pool_w, ev_pool_scale, ev_w_out, od_w_in, od_v_ln_g, od_v_ln_b, od_sp_w, od_sp_b, od_dw_w, od_dw_b, od_cn_g, od_cn_b, od_w_out, final_norm_g):
    cfg = Cfg(x_prompt.shape[0], x_prompt.shape[1], x_sample.shape[0], x_sample.shape[1])
    return _forward(cfg, x_prompt, x_sample, c, state_ssd, c_ctx, w_mod, b_mod, norm_g, ffn_w_gate, ffn_w_up,
                    ffn_w_down, ev_w_in, ev_conv_w, ev_conv_b, ev_dt_bias, ev_a_log, ev_d_skip, ev_ssd_norm_g,
                    ev_pool_w, ev_pool_scale, ev_w_out, od_w_in, od_v_ln_g, od_v_ln_b, od_sp_w, od_sp_b, od_dw_w,
                    od_dw_b, od_cn_g, od_cn_b, od_w_out, final_norm_g)
```

```python
import collections
import functools
import math

import jax
import jax.numpy as jnp
from jax import lax
from jax.experimental import pallas as pl
from jax.experimental.pallas import tpu as pltpu

F32 = jnp.float32
BF16 = jnp.bfloat16

D_MODEL = 1024
DEPTH = 4
GRID_W = 64
POS_BASE = 10000.0
EPS = 1e-6
N_MOD = 9
D_FF = 2816
N_HEADS_A = 16
HEAD_DIM_A = 64
D_INNER_A = N_HEADS_A * HEAD_DIM_A
N_GROUPS_A = 2
HEADS_PER_GROUP = N_HEADS_A // N_GROUPS_A
D_STATE = 128
CONV_A = 5
SSD_CHUNK = 128
D_BC = N_GROUPS_A * D_STATE
D_XBC = D_INNER_A + 2 * D_BC
POOL_WINDOWS = (2, 4, 8, 16)
POOL_GROUP_DIM = 128
D_POOL = len(POOL_WINDOWS) * POOL_GROUP_DIM
N_HEADS_C = 8
MLP_CHUNK = 128
D_C = 1024
D_D = 1024
CONV_D = 31
N_EVEN = (DEPTH + 1) // 2

LANES = 128
SUBLANES = 8
DT_PAD = LANES
HALO_A = SUBLANES
HALO_D = 2 * SUBLANES
VMEM_LIMIT = 56 * 1024 * 1024

TM_FFN = 512
TF_FFN = 256
TM_PROJ = 512
TM_MIX = 256
CONV_COLS = 512
Cfg = collections.namedtuple("Cfg", "batch seq dec_batch dec_seq")


def _tile_info(t, tm, cfg):
    npt = cfg.seq // tm
    nst = cfg.dec_seq // tm
    n_p = cfg.batch * npt
    is_prompt = t < n_p
    ts = jnp.maximum(t - n_p, 0)
    pos = jnp.where(is_prompt, lax.rem(t, npt), lax.rem(ts, nst))
    last = jnp.where(is_prompt, npt - 1, nst - 1)
    info = dict(
        is_prompt=is_prompt,
        is_start=pos == 0,
        is_end=pos == last,
        pos=pos,
        seq_len=jnp.where(is_prompt, cfg.seq, cfg.dec_seq),
    )
    return info


def _mod_row(t, tm, cfg):
    n_p = (cfg.batch * cfg.seq) // tm
    per = cfg.dec_seq // tm
    return jnp.where(t < n_p, 0, 1 + lax.div(jnp.maximum(t - n_p, 0), per))


def _sigmoid(x):
    return 0.5 + 0.5 * jnp.tanh(0.5 * x)


def _silu(x):
    h = 0.5 * x
    return h + h * jnp.tanh(h)


def _softplus(x):
    return jnp.maximum(x, 0.0) + jnp.log1p(jnp.exp(-jnp.abs(x)))


def _rms(x, g):
    return x * lax.rsqrt(jnp.mean(x * x, axis=-1, keepdims=True) + EPS) * g


def _layer_norm(x, g, b):
    xc = x - jnp.mean(x, axis=-1, keepdims=True)
    y = xc * lax.rsqrt(jnp.mean(xc * xc, axis=-1, keepdims=True) + EPS)
    return y * g + b


def _dot(a, b):
    return jnp.dot(a, b, preferred_element_type=F32)


def _dot_nt(a, b):
    return lax.dot_general(a, b, (((1,), (1,)), ((), ())), preferred_element_type=F32)


def _params(sem):
    return pltpu.CompilerParams(dimension_semantics=sem, vmem_limit_bytes=VMEM_LIMIT)


def _mod_kernel(c_ref, w_ref, b_ref, o_ref):
    c = c_ref[...]
    sc = _silu(c).astype(BF16)
    o_ref[...] = _dot(sc, w_ref[...].astype(BF16)) + b_ref[...]


def _modulation(cond8, w_mod, b_mod):
    b4 = b_mod.reshape(DEPTH, N_MOD, 1, D_MODEL)
    return pl.pallas_call(
        _mod_kernel,
        grid=(DEPTH, N_MOD),
        in_specs=[
            pl.BlockSpec((SUBLANES, D_MODEL), lambda i, k: (0, 0)),
            pl.BlockSpec((None, D_MODEL, D_MODEL), lambda i, k: (i, 0, k)),
            pl.BlockSpec((None, None, 1, D_MODEL), lambda i, k: (i, k, 0, 0)),
        ],
        out_specs=pl.BlockSpec((None, None, SUBLANES, D_MODEL), lambda i, k: (i, k, 0, 0)),
        out_shape=jax.ShapeDtypeStruct((DEPTH, N_MOD, SUBLANES, D_MODEL), F32),
        compiler_params=_params(("arbitrary", "arbitrary")),
        name="modulation",
    )(cond8, w_mod, b4)


def _mod_spec(i, k):
    return pl.BlockSpec((None, None, SUBLANES, D_MODEL), lambda t: (i, k, 0, 0))


def _ffn_body(x, r, sh_ref, sc_ref, gt_ref, g_ref, wg_ref, wu_ref, wd_ref, hn_ref, act_ref):
    hn = _rms(x, g_ref[...]) * (1.0 + sc_ref[pl.ds(r, 1), :]) + sh_ref[pl.ds(r, 1), :]
    hn_ref[...] = hn.astype(BF16)
    for f in range(D_FF // TF_FFN):
        sl = slice(f * TF_FFN, (f + 1) * TF_FFN)
        g = _dot(hn_ref[...], wg_ref[:, sl])
        u = _dot(hn_ref[...], wu_ref[:, sl])
        act_ref[:, sl] = (_silu(g) * u).astype(BF16)
    return x + (0.5 * gt_ref[pl.ds(r, 1), :]) * _dot(act_ref[...], wd_ref[...])


def _ffn_kernel(x_ref, sh_ref, sc_ref, gt_ref, g_ref, wg_ref, wu_ref, wd_ref, o_ref, hn_ref, act_ref, *, cfg):
    r = _mod_row(pl.program_id(0), TM_FFN, cfg)
    o_ref[...] = _ffn_body(x_ref[...], r, sh_ref, sc_ref, gt_ref, g_ref, wg_ref, wu_ref, wd_ref, hn_ref, act_ref)


def _ffn_first_kernel(xp_ref, xs_ref, rt_ref, ct_ref, sh_ref, sc_ref, gt_ref, g_ref, wg_ref, wu_ref, wd_ref,
                      o_ref, x_ref, hn_ref, act_ref, *, cfg):
    t = pl.program_id(0)
    n_p = (cfg.batch * cfg.seq) // TM_FFN
    is_prompt = t < n_p

    @pl.when(is_prompt)
    def _():
        x_ref[...] = xp_ref[...]

    @pl.when(jnp.logical_not(is_prompt))
    def _():
        half = D_MODEL // 2
        row0 = lax.rem(jnp.maximum(t - n_p, 0), cfg.dec_seq // TM_FFN) * (TM_FFN // GRID_W)
        for rr in range(TM_FFN // GRID_W):
            rows = slice(rr * GRID_W, (rr + 1) * GRID_W)
            x_ref[rows, 0:half] = xs_ref[rows, 0:half] + rt_ref[pl.ds(row0 + rr, 1), :]
            x_ref[rows, half:D_MODEL] = xs_ref[rows, half:D_MODEL] + ct_ref[...]

    r = _mod_row(t, TM_FFN, cfg)
    o_ref[...] = _ffn_body(x_ref[...], r, sh_ref, sc_ref, gt_ref, g_ref, wg_ref, wu_ref, wd_ref, hn_ref, act_ref)


def _ffn_last_kernel(x_ref, sh_ref, sc_ref, gt_ref, g_ref, wg_ref, wu_ref, wd_ref, fg_ref, op_ref, os_ref,
                     hn_ref, act_ref, *, cfg):
    t = pl.program_id(0)
    r = _mod_row(t, TM_FFN, cfg)
    xn = _ffn_body(x_ref[...], r, sh_ref, sc_ref, gt_ref, g_ref, wg_ref, wu_ref, wd_ref, hn_ref, act_ref)
    y = _rms(xn, fg_ref[...])
    is_prompt = t < (cfg.batch * cfg.seq) // TM_FFN

    @pl.when(is_prompt)
    def _():
        op_ref[...] = y

    @pl.when(jnp.logical_not(is_prompt))
    def _():
        os_ref[...] = y


def _resident(shape, index):
    return pl.BlockSpec(shape, lambda t: index, pipeline_mode=pl.Buffered(1))


def _ffn_specs(i, which, k0):
    return [
        _mod_spec(i, k0), _mod_spec(i, k0 + 1), _mod_spec(i, k0 + 2),
        pl.BlockSpec((None, None, 1, D_MODEL), lambda t: (i, 2 * which, 0, 0)),
        _resident((None, None, D_MODEL, D_FF), (i, which, 0, 0)),
        _resident((None, None, D_MODEL, D_FF), (i, which, 0, 0)),
        _resident((None, None, D_FF, D_MODEL), (i, which, 0, 0)),
    ]


def _ffn_scratch():
    return [pltpu.VMEM((TM_FFN, D_MODEL), BF16), pltpu.VMEM((TM_FFN, D_FF), BF16)]


def _ffn(x, mod, i, which, norm_g4, wg, wu, wd, cfg):
    t_tok = x.shape[0]
    row = pl.BlockSpec((TM_FFN, D_MODEL), lambda t: (t, 0))
    return pl.pallas_call(
        functools.partial(_ffn_kernel, cfg=cfg),
        grid=(t_tok // TM_FFN,),
        in_specs=[row] + _ffn_specs(i, which, 6 * which),
        out_specs=row,
        out_shape=jax.ShapeDtypeStruct((t_tok, D_MODEL), F32),
        scratch_shapes=_ffn_scratch(),
        compiler_params=_params(("parallel",)),
        name="ffn",
    )(x, mod, mod, mod, norm_g4, wg, wu, wd)


def _ffn_first(xp, xs, row_tab, col_tab, mod, norm_g4, wg, wu, wd, cfg):
    n_p = xp.shape[0] // TM_FFN
    t_tok = xp.shape[0] + xs.shape[0]
    half = D_MODEL // 2
    return pl.pallas_call(
        functools.partial(_ffn_first_kernel, cfg=cfg),
        grid=(t_tok // TM_FFN,),
        in_specs=[
            pl.BlockSpec((TM_FFN, D_MODEL), lambda t: (jnp.minimum(t, n_p - 1), 0)),
            pl.BlockSpec((TM_FFN, D_MODEL), lambda t: (jnp.maximum(t - n_p, 0), 0)),
            pl.BlockSpec((cfg.dec_seq // GRID_W, half), lambda t: (0, 0)),
            pl.BlockSpec((GRID_W, half), lambda t: (0, 0)),
        ] + _ffn_specs(0, 0, 0),
        out_specs=pl.BlockSpec((TM_FFN, D_MODEL), lambda t: (t, 0)),
        out_shape=jax.ShapeDtypeStruct((t_tok, D_MODEL), F32),
        scratch_shapes=[pltpu.VMEM((TM_FFN, D_MODEL), F32)] + _ffn_scratch(),
        compiler_params=_params(("parallel",)),
        name="ffn_first",
    )(xp, xs, row_tab, col_tab, mod, mod, mod, norm_g4, wg, wu, wd)


def _ffn_last(x, mod, norm_g4, wg, wu, wd, final_g, cfg):
    i = DEPTH - 1
    t_p = cfg.batch * cfg.seq
    n_p = t_p // TM_FFN
    return pl.pallas_call(
        functools.partial(_ffn_last_kernel, cfg=cfg),
        grid=(x.shape[0] // TM_FFN,),
        in_specs=[pl.BlockSpec((TM_FFN, D_MODEL), lambda t: (t, 0))] + _ffn_specs(i, 1, 6)
        + [pl.BlockSpec((1, D_MODEL), lambda t: (0, 0))],
        out_specs=[pl.BlockSpec((TM_FFN, D_MODEL), lambda t: (jnp.minimum(t, n_p - 1), 0)),
                   pl.BlockSpec((TM_FFN, D_MODEL), lambda t: (jnp.maximum(t - n_p, 0), 0))],
        out_shape=[jax.ShapeDtypeStruct((t_p, D_MODEL), F32),
                   jax.ShapeDtypeStruct((x.shape[0] - t_p, D_MODEL), F32)],
        scratch_shapes=_ffn_scratch(),
        compiler_params=_params(("arbitrary",)),
        name="ffn_last",
    )(x, mod, mod, mod, norm_g4, wg, wu, wd, final_g.reshape(1, D_MODEL))


def _norm_mod(x_ref, sh_ref, sc_ref, g_ref, cfg):
    r = _mod_row(pl.program_id(0), TM_PROJ, cfg)
    hn = _rms(x_ref[...], g_ref[...]) * (1.0 + sc_ref[pl.ds(r, 1), :]) + sh_ref[pl.ds(r, 1), :]
    return hn.astype(BF16)


def _even_in_kernel(xm_ref, xp_ref, xn_ref, sh_ref, sc_ref, g_ref, w_ref, cw_ref, cb_ref,
                    z_ref, xst_ref, bc_ref, pool_ref, dt_ref, hn_ref, ext_ref, *, cfg):
    tm, q = TM_MIX, SSD_CHUNK
    t = pl.program_id(0)
    info = _tile_info(t, tm, cfg)
    r = _mod_row(t, tm, cfg)
    scale, shift, g = 1.0 + sc_ref[pl.ds(r, 1), :], sh_ref[pl.ds(r, 1), :], g_ref[...]
    hn_ref[0:tm, :] = _rms(xm_ref[...], g) * scale + shift
    hn_ref[tm:tm + HALO_A, :] = _rms(xp_ref[...], g) * scale + shift
    hn_ref[tm + HALO_A:, :] = _rms(xn_ref[...], g) * scale + shift
    hn = hn_ref[...].astype(BF16)
    o_xbc, o_pool, o_dt = D_INNER_A, D_INNER_A + D_XBC, D_INNER_A + D_XBC + D_POOL
    base = HALO_A - CONV_A // 2
    zero = jnp.zeros((HALO_A, CONV_COLS), F32)
    others = [(z_ref.at[:, 0:CONV_COLS], 0), (z_ref.at[:, CONV_COLS:D_INNER_A], CONV_COLS), (pool_ref, o_pool)]
    for c0 in range(0, D_XBC, CONV_COLS):
        cols = slice(c0, c0 + CONV_COLS)
        xg = _dot(hn, w_ref[:, o_xbc + c0:o_xbc + c0 + CONV_COLS])
        o_ref, o_col = others[c0 // CONV_COLS]
        o_ref[...] = _dot(hn[0:tm], w_ref[:, o_col:o_col + CONV_COLS])
        ext_ref[0:HALO_A, cols] = jnp.where(info["is_start"], zero, xg[tm:tm + HALO_A])
        ext_ref[HALO_A:HALO_A + tm, cols] = xg[0:tm]
        ext_ref[HALO_A + tm:, cols] = jnp.where(info["is_end"], zero, xg[tm + HALO_A:])
        for ck in range(tm // q):
            for j in range(c0 // LANES, (c0 + CONV_COLS) // LANES):
                sl = slice(j * LANES, (j + 1) * LANES)
                r0 = base + ck * q
                acc = cw_ref[0:1, sl] * ext_ref[r0:r0 + q, sl]
                for k in range(1, CONV_A):
                    acc = acc + cw_ref[k:k + 1, sl] * ext_ref[r0 + k:r0 + k + q, sl]
                y = _silu(acc + cb_ref[:, sl])
                if j < D_INNER_A // LANES:
                    xst_ref[ck, j * LANES:(j + 1) * LANES, :] = y.T
                else:
                    bc_ref[ck * q:(ck + 1) * q, j * LANES - D_INNER_A:(j + 1) * LANES - D_INNER_A] = y.astype(BF16)
    dt_ref[...] = _dot(hn[0:tm], w_ref[:, o_dt:o_dt + DT_PAD])


def _even_in(x, mod, i, g, w, conv_w, conv_b, cfg):
    t_tok = x.shape[0]
    tm, q = TM_MIX, SSD_CHUNK
    n_tot = D_INNER_A + D_XBC + D_POOL + DT_PAD
    small = lambda shape: pl.BlockSpec(shape, lambda t: (0,) * len(shape))
    row_block = lambda width: pl.BlockSpec((tm, width), lambda t: (t, 0))
    return pl.pallas_call(
        functools.partial(_even_in_kernel, cfg=cfg),
        grid=(t_tok // tm,),
        in_specs=_halo_specs(D_MODEL, tm, HALO_A, t_tok, lambda t: t) + [
            _mod_spec(i, 3), _mod_spec(i, 4), small((1, D_MODEL)), small((D_MODEL, n_tot)),
            small((CONV_A, D_XBC)), small((1, D_XBC)),
        ],
        out_specs=[row_block(D_INNER_A),
                   pl.BlockSpec((tm // q, D_INNER_A, q), lambda t: (t, 0, 0)),
                   row_block(2 * D_BC), row_block(D_POOL), row_block(DT_PAD)],
        out_shape=[jax.ShapeDtypeStruct((t_tok, D_INNER_A), F32),
                   jax.ShapeDtypeStruct((t_tok // q, D_INNER_A, q), F32),
                   jax.ShapeDtypeStruct((t_tok, 2 * D_BC), BF16),
                   jax.ShapeDtypeStruct((t_tok, D_POOL), F32),
                   jax.ShapeDtypeStruct((t_tok, DT_PAD), F32)],
        scratch_shapes=[pltpu.VMEM((tm + 2 * HALO_A, D_MODEL), F32), pltpu.VMEM((tm + 2 * HALO_A, D_XBC), F32)],
        compiler_params=_params(("parallel",)),
        name="even_in",
    )(x, x, x, mod, mod, g.reshape(1, D_MODEL), w, conv_w, conv_b)


def _odd_in_kernel(x_ref, sh_ref, sc_ref, g_ref, w_ref, lg_ref, lb_ref, u_ref, v_ref, glu_ref, *, cfg):
    hn = _norm_mod(x_ref, sh_ref, sc_ref, g_ref, cfg)
    u_ref[...] = jax.nn.gelu(_dot(hn, w_ref[:, 0:D_C]))
    v = jax.nn.gelu(_dot(hn, w_ref[:, D_C:2 * D_C]))
    v_ref[...] = _layer_norm(v, lg_ref[...], lb_ref[...]).astype(BF16)
    ga = _dot(hn, w_ref[:, 2 * D_C:2 * D_C + D_D])
    gg = _dot(hn, w_ref[:, 2 * D_C + D_D:2 * D_C + 2 * D_D])
    glu_ref[...] = ga * _sigmoid(gg)


def _odd_in(x, mod, i, g, w, ln_g, ln_b, cfg):
    t_tok = x.shape[0]
    row = lambda: pl.BlockSpec((1, D_MODEL), lambda t: (0, 0))
    return pl.pallas_call(
        functools.partial(_odd_in_kernel, cfg=cfg),
        grid=(t_tok // TM_PROJ,),
        in_specs=[
            pl.BlockSpec((TM_PROJ, D_MODEL), lambda t: (t, 0)),
            _mod_spec(i, 3), _mod_spec(i, 4),
            row(),
            pl.BlockSpec((D_MODEL, 2 * D_C + 2 * D_D), lambda t: (0, 0)),
            row(), row(),
        ],
        out_specs=[pl.BlockSpec((TM_PROJ, D_C), lambda t: (t, 0))] * 3,
        out_shape=[jax.ShapeDtypeStruct((t_tok, D_C), F32),
                   jax.ShapeDtypeStruct((t_tok, D_C), BF16),
                   jax.ShapeDtypeStruct((t_tok, D_D), F32)],
        compiler_params=_params(("parallel",)),
        name="odd_in",
    )(x, mod, mod, g.reshape(1, D_MODEL), w, ln_g.reshape(1, D_C), ln_b.reshape(1, D_C))


def _dt_terms(dt_ref, dtb_ref, alog_ref, ck, rows):
    q = SSD_CHUNK
    dtt = dt_ref[ck * q:(ck + 1) * q, :].T[rows, :]
    dts = _softplus(dtt + dtb_ref[rows, :])
    return dts, dts * (-jnp.exp(alog_ref[rows, :]))


def _tri(lower_incl):
    ri = lax.broadcasted_iota(jnp.int32, (SSD_CHUNK, SSD_CHUNK), 0)
    ci = lax.broadcasted_iota(jnp.int32, (SSD_CHUNK, SSD_CHUNK), 1)
    return (ri >= ci) if lower_incl else (ri <= ci)


def _cumsum_lanes(v, mask):
    return jnp.dot(v, mask.astype(F32), precision=lax.Precision.HIGHEST, preferred_element_type=F32)


def _group_bc(bc_ref, ck, g):
    rows = slice(ck * SSD_CHUNK, (ck + 1) * SSD_CHUNK)
    b = bc_ref[rows, g * D_STATE:(g + 1) * D_STATE]
    c = bc_ref[rows, D_BC + g * D_STATE:D_BC + (g + 1) * D_STATE]
    return b, c


def _state_update(s_ref, xst_ref, xd_ref, bc_ref, ck, scale, cdec):
    hd = HEAD_DIM_A
    for h in range(N_HEADS_A):
        xd_ref[ck, h * hd:(h + 1) * hd, :] = (xst_ref[ck, h * hd:(h + 1) * hd, :] * scale[h:h + 1, :]).astype(BF16)
    gw = HEADS_PER_GROUP * hd
    for g in range(N_GROUPS_A):
        b, _ = _group_bc(bc_ref, ck, g)
        upd = _dot(xd_ref[ck, g * gw:(g + 1) * gw, :], b)
        for e in range(HEADS_PER_GROUP):
            h = g * HEADS_PER_GROUP + e
            s_ref[h * hd:(h + 1) * hd, :] = (s_ref[h * hd:(h + 1) * hd, :] * cdec[h:h + 1, :]
                                             + upd[e * hd:(e + 1) * hd, :])


def _ssd_fwd_chunk(ck, xst_ref, bc_ref, dt_ref, dtb_ref, alog_ref, dsk_ref, ypt_ref, xd_ref, s_ref):
    q, hd, nh = SSD_CHUNK, HEAD_DIM_A, N_HEADS_A
    dts, dta = _dt_terms(dt_ref, dtb_ref, alog_ref, ck, slice(0, 2 * nh))
    low, upp = _tri(True), _tri(False)
    acs_f = _cumsum_lanes(dta[0:nh], upp)
    rcs_b = _cumsum_lanes(dta[nh:2 * nh], low)
    src = jnp.concatenate([acs_f - jnp.log(dts[0:nh]), rcs_b - jnp.log(dts[nh:2 * nh]),
                           jnp.zeros((LANES - 2 * nh, q), F32)], axis=0).T
    e_acs = jnp.exp(acs_f)
    neg_inf = jnp.float32(-jnp.inf)

    gw = HEADS_PER_GROUP * hd
    for g in range(N_GROUPS_A):
        b, c = _group_bc(bc_ref, ck, g)
        sct = _dot_nt(b, c)
        yoff = _dot_nt(s_ref[g * gw:(g + 1) * gw, :].astype(BF16), c)
        for e in range(HEADS_PER_GROUP):
            h = g * HEADS_PER_GROUP + e
            rows = slice(h * hd, (h + 1) * hd)
            seg_f = acs_f[h:h + 1, :] - src[:, h:h + 1]
            seg_b = rcs_b[h:h + 1, :] - src[:, nh + h:nh + h + 1]
            wt = sct * (jnp.exp(jnp.where(upp, seg_f, neg_inf)) + jnp.exp(jnp.where(low, seg_b, neg_inf)))
            xh = xst_ref[ck, rows, :]
            yd = _dot(xh.astype(BF16), wt.astype(BF16))
            ypt_ref[ck, rows, :] = yd + yoff[e * hd:(e + 1) * hd, :] * e_acs[h:h + 1, :] + dsk_ref[rows, :] * xh

    tot = acs_f[:, q - 1:q]
    scale = dts[0:nh] * jnp.exp(tot - acs_f)
    cdec = jnp.exp(jnp.broadcast_to(tot, (nh, D_STATE)))
    _state_update(s_ref, xst_ref, xd_ref, bc_ref, ck, scale, cdec)


def _ssd_fwd_kernel(xst_ref, bc_ref, dt_ref, dtb_ref, alog_ref, dsk_ref, s0_ref, ypt_ref, sfin_ref, xd_ref, s_ref,
                    *, cfg):
    info = _tile_info(pl.program_id(0), TM_MIX, cfg)

    @pl.when(jnp.logical_and(info["is_start"], info["is_prompt"]))
    def _():
        s_ref[...] = jnp.zeros_like(s_ref)

    @pl.when(jnp.logical_and(info["is_start"], jnp.logical_not(info["is_prompt"])))
    def _():
        s_ref[...] = s0_ref[...]

    for ck in range(TM_MIX // SSD_CHUNK):
        _ssd_fwd_chunk(ck, xst_ref, bc_ref, dt_ref, dtb_ref, alog_ref, dsk_ref, ypt_ref, xd_ref, s_ref)

    @pl.when(jnp.logical_and(info["is_end"], info["is_prompt"]))
    def _():
        sfin_ref[...] = s_ref[...]


def _ssd_bwd_chunk(ck, dt_ref, z_ref, ypt_ref, xst_ref, bc_ref, dtb_ref, alog_ref, yt_ref, y_ref, xd_ref, s_ref):
    q, hd, nh = SSD_CHUNK, HEAD_DIM_A, N_HEADS_A
    dts, dta = _dt_terms(dt_ref, dtb_ref, alog_ref, ck, slice(nh, 2 * nh))
    rcs_b = _cumsum_lanes(dta, _tri(True))
    e_rcs = jnp.exp(rcs_b)

    gw = HEADS_PER_GROUP * hd
    for g in range(N_GROUPS_A):
        _, c = _group_bc(bc_ref, ck, g)
        yoff = _dot_nt(s_ref[g * gw:(g + 1) * gw, :].astype(BF16), c)
        for e in range(HEADS_PER_GROUP):
            h = g * HEADS_PER_GROUP + e
            yt_ref[ck, h * hd:(h + 1) * hd, :] = (ypt_ref[ck, h * hd:(h + 1) * hd, :]
                                                  + yoff[e * hd:(e + 1) * hd, :] * e_rcs[h:h + 1, :])

    rows = slice(ck * q, (ck + 1) * q)
    for j in range(D_INNER_A // LANES):
        sl = slice(j * LANES, (j + 1) * LANES)
        y_ref[rows, sl] = yt_ref[ck, j * LANES:(j + 1) * LANES, :].T * _silu(z_ref[rows, sl])

    tot = rcs_b[:, 0:1]
    scale = dts * jnp.exp(tot - rcs_b)
    cdec = jnp.exp(jnp.broadcast_to(tot, (nh, D_STATE)))
    _state_update(s_ref, xst_ref, xd_ref, bc_ref, ck, scale, cdec)


def _ssd_bwd_kernel(dt_ref, z_ref, ypt_ref, xst_ref, bc_ref, dtb_ref, alog_ref, ng_ref, s0_ref, sf_ref, *rest,
                    cfg, n_tiles, layer):
    if layer == 0:
        ya_ref, so_ref, xd_ref, yt_ref, y_ref, s_ref = rest
    else:
        sprev_ref, ya_ref, so_ref, xd_ref, yt_ref, y_ref, s_ref = rest
    info = _tile_info(n_tiles - 1 - pl.program_id(0), TM_MIX, cfg)

    @pl.when(jnp.logical_and(info["is_end"], info["is_prompt"]))
    def _():
        s_ref[...] = jnp.zeros_like(s_ref)

    @pl.when(jnp.logical_and(info["is_end"], jnp.logical_not(info["is_prompt"])))
    def _():
        s_ref[...] = s0_ref[...]

    for ck in reversed(range(TM_MIX // SSD_CHUNK)):
        _ssd_bwd_chunk(ck, dt_ref, z_ref, ypt_ref, xst_ref, bc_ref, dtb_ref, alog_ref, yt_ref, y_ref, xd_ref, s_ref)
    ya_ref[...] = _rms(y_ref[...], ng_ref[...]).astype(BF16)

    @pl.when(jnp.logical_and(info["is_start"], info["is_prompt"]))
    def _():
        if layer == 0:
            so_ref[0] = sf_ref[...]
            so_ref[1] = s_ref[...]
        else:
            so_ref[0] = sprev_ref[...]
            so_ref[1, 0] = sf_ref[...]
            so_ref[1, 1] = s_ref[...]


def _halo_specs(width, tile, halo, n_rows, chunk_of):
    per = tile // halo
    last = n_rows // halo - 1
    return [
        pl.BlockSpec((tile, width), lambda i: (chunk_of(i), 0)),
        pl.BlockSpec((halo, width), lambda i: (jnp.maximum(chunk_of(i) * per - 1, 0), 0)),
        pl.BlockSpec((halo, width), lambda i: (jnp.minimum((chunk_of(i) + 1) * per, last), 0)),
    ]


def _ssd(xst, bc, dt, z, dtb, alog, dsk_t, norm_g, s0, s_prev, layer, cfg):
    t_tok = bc.shape[0]
    q, tm = SSD_CHUNK, TM_MIX
    cpt = tm // q
    n_tiles = t_tok // tm
    hp = N_HEADS_A * HEAD_DIM_A
    npt = cfg.seq // tm
    nst = cfg.dec_seq // tm
    n_pt = cfg.batch * npt

    def s_seq(t):
        return jnp.clip(lax.div(jnp.maximum(t - n_pt, 0), nst), 0, cfg.dec_batch - 1)

    def p_seq(t):
        return jnp.minimum(lax.div(t, npt), cfg.batch - 1)

    small = lambda shape: pl.BlockSpec(shape, lambda i: (0,) * len(shape))
    head_rows = small((2 * N_HEADS_A, q))

    ypt, sf = pl.pallas_call(
        functools.partial(_ssd_fwd_kernel, cfg=cfg),
        grid=(n_tiles,),
        in_specs=[
            pl.BlockSpec((cpt, hp, q), lambda i: (i, 0, 0)),
            pl.BlockSpec((tm, 2 * D_BC), lambda i: (i, 0)),
            pl.BlockSpec((tm, DT_PAD), lambda i: (i, 0)),
            head_rows, head_rows, small((hp, q)),
            pl.BlockSpec((None, None, hp, D_STATE), lambda i: (s_seq(i), 0, 0, 0)),
        ],
        out_specs=[pl.BlockSpec((cpt, hp, q), lambda i: (i, 0, 0)),
                   pl.BlockSpec((None, hp, D_STATE), lambda i: (p_seq(i), 0, 0))],
        out_shape=[jax.ShapeDtypeStruct((t_tok // q, hp, q), F32),
                   jax.ShapeDtypeStruct((cfg.batch, hp, D_STATE), F32)],
        scratch_shapes=[pltpu.VMEM((cpt, hp, q), BF16), pltpu.VMEM((hp, D_STATE), F32)],
        compiler_params=_params(("arbitrary",)),
        name="ssd_fwd",
    )(xst, bc, dt, dtb, alog, dsk_t, s0)

    rev = lambda i: n_tiles - 1 - i
    in_specs = [
        pl.BlockSpec((tm, DT_PAD), lambda i: (rev(i), 0)),
        pl.BlockSpec((tm, D_INNER_A), lambda i: (rev(i), 0)),
        pl.BlockSpec((cpt, hp, q), lambda i: (rev(i), 0, 0)),
        pl.BlockSpec((cpt, hp, q), lambda i: (rev(i), 0, 0)),
        pl.BlockSpec((tm, 2 * D_BC), lambda i: (rev(i), 0)),
        head_rows, head_rows, small((1, D_INNER_A)),
        pl.BlockSpec((None, None, hp, D_STATE), lambda i: (s_seq(rev(i)), 1, 0, 0)),
        pl.BlockSpec((None, hp, D_STATE), lambda i: (p_seq(rev(i)), 0, 0)),
    ]
    args = [dt, z, ypt, xst, bc, dtb, alog, norm_g, s0, sf]
    if layer == 0:
        so_block, so_shape = (None, 2, hp, D_STATE), (cfg.batch, 2, hp, D_STATE)
        so_index = lambda i: (p_seq(rev(i)), 0, 0, 0)
    else:
        in_specs.append(pl.BlockSpec((None, 2, hp, D_STATE), lambda i: (p_seq(rev(i)), 0, 0, 0)))
        args.append(s_prev)
        so_block, so_shape = (None, 2, 2, hp, D_STATE), (cfg.batch, 2, 2, hp, D_STATE)
        so_index = lambda i: (p_seq(rev(i)), 0, 0, 0, 0)
    ya, so = pl.pallas_call(
        functools.partial(_ssd_bwd_kernel, cfg=cfg, n_tiles=n_tiles, layer=layer),
        grid=(n_tiles,),
        in_specs=in_specs,
        out_specs=[pl.BlockSpec((tm, D_INNER_A), lambda i: (rev(i), 0)), pl.BlockSpec(so_block, so_index)],
        out_shape=[jax.ShapeDtypeStruct((t_tok, D_INNER_A), BF16), jax.ShapeDtypeStruct(so_shape, F32)],
        scratch_shapes=[pltpu.VMEM((cpt, hp, q), BF16), pltpu.VMEM((cpt, hp, q), F32),
                        pltpu.VMEM((tm, D_INNER_A), F32), pltpu.VMEM((hp, D_STATE), F32)],
        compiler_params=_params(("arbitrary",)),
        name="ssd_bwd",
    )(*args)
    return ya, so


def _even_out_kernel(ya_ref, pm_ref, pp_ref, pn_ref, x_ref, gt_ref, pw_ref, ps_ref, woa_ref, wop_ref,
                     o_ref, ext_ref, yp_ref, *, cfg):
    tm = TM_MIX
    t = pl.program_id(0)
    info = _tile_info(t, tm, cfg)
    r = _mod_row(t, tm, cfg)
    zero = jnp.zeros((HALO_A, D_POOL), F32)
    ext_ref[0:HALO_A, :] = jnp.where(info["is_start"], zero, pp_ref[...])
    ext_ref[HALO_A:HALO_A + tm, :] = pm_ref[...]
    ext_ref[HALO_A + tm:, :] = jnp.where(info["is_end"], zero, pn_ref[...])
    pos = info["pos"] * tm + lax.broadcasted_iota(jnp.int32, (tm, POOL_GROUP_DIM), 0)
    for gi, win in enumerate(POOL_WINDOWS):
        sl = slice(gi * POOL_GROUP_DIM, (gi + 1) * POOL_GROUP_DIM)
        lo = HALO_A - win // 2
        s = ext_ref[lo:lo + tm, sl]
        for j in range(1, win):
            s = s + ext_ref[lo + j:lo + j + tm, sl]
        cnt = (jnp.minimum(pos - win // 2 + win, info["seq_len"]) - jnp.maximum(pos - win // 2, 0)).astype(F32)
        pooled = s / cnt - ext_ref[HALO_A:HALO_A + tm, sl]
        yp_ref[:, sl] = (_dot(pooled.astype(BF16), pw_ref[gi]) * ps_ref[:, sl]).astype(BF16)
    out = _dot(ya_ref[...], woa_ref[...]) + _dot(yp_ref[...], wop_ref[...])
    o_ref[...] = x_ref[...] + gt_ref[pl.ds(r, 1), :] * out


def _even_out(ya, pool_in, x, mod, i, pool_w, pool_scale, wo_a, wo_p, cfg):
    t_tok = x.shape[0]
    tm = TM_MIX
    small = lambda shape: pl.BlockSpec(shape, lambda t: (0,) * len(shape))
    return pl.pallas_call(
        functools.partial(_even_out_kernel, cfg=cfg),
        grid=(t_tok // tm,),
        in_specs=[pl.BlockSpec((tm, D_INNER_A), lambda t: (t, 0))]
        + _halo_specs(D_POOL, tm, HALO_A, t_tok, lambda t: t) + [
            pl.BlockSpec((tm, D_MODEL), lambda t: (t, 0)),
            _mod_spec(i, 5),
            small((len(POOL_WINDOWS), POOL_GROUP_DIM, POOL_GROUP_DIM)), small((1, D_POOL)),
            small((D_INNER_A, D_MODEL)), small((D_POOL, D_MODEL)),
        ],
        out_specs=pl.BlockSpec((tm, D_MODEL), lambda t: (t, 0)),
        out_shape=jax.ShapeDtypeStruct((t_tok, D_MODEL), F32),
        scratch_shapes=[pltpu.VMEM((tm + 2 * HALO_A, D_POOL), F32), pltpu.VMEM((tm, D_POOL), BF16)],
        compiler_params=_params(("parallel",)),
        name="even_out",
    )(ya, pool_in, pool_in, pool_in, x, mod, pool_w, pool_scale, wo_a, wo_p)


def _odd_out_kernel(u_ref, v_ref, gm_ref, gp_ref, gn_ref, x_ref, gt_ref, spw_ref, spb_ref, dww_ref, dwb_ref,
                    cg_ref, cb_ref, woc_ref, wod_ref, o_ref, ext_ref, sh_ref, yc_ref, yd_ref, *, cfg):
    tm = TM_MIX
    t = pl.program_id(0)
    info = _tile_info(t, tm, cfg)
    r = _mod_row(t, tm, cfg)
    hc = D_C // N_HEADS_C
    for ck in range(tm // MLP_CHUNK):
        rows = slice(ck * MLP_CHUNK, (ck + 1) * MLP_CHUNK)
        for h in range(N_HEADS_C):
            sl = slice(h * hc, (h + 1) * hc)
            sv = _dot(spw_ref[h], v_ref[rows, sl]) + spb_ref[:, sl]
            yc_ref[rows, sl] = (u_ref[rows, sl] * sv).astype(BF16)
    zero = jnp.zeros((HALO_D, D_D), F32)
    ext_ref[0:HALO_D, :] = jnp.where(info["is_start"], zero, gp_ref[...])
    ext_ref[HALO_D:HALO_D + tm, :] = gm_ref[...]
    ext_ref[HALO_D + tm:, :] = jnp.where(info["is_end"], zero, gn_ref[...])
    span = tm + 2 * HALO_D - SUBLANES
    base = HALO_D - CONV_D // 2
    rb = MLP_CHUNK
    for j in range(D_D // LANES):
        sl = slice(j * LANES, (j + 1) * LANES)
        for s in range(1, SUBLANES):
            sh_ref[s - 1, :, sl] = ext_ref[s:s + span, sl]
        for rr in range(tm // rb):
            acc = None
            for k in range(CONV_D):
                q8, s = divmod(base + k, SUBLANES)
                r0 = q8 * SUBLANES + rr * rb
                tap = ext_ref[r0:r0 + rb, sl] if s == 0 else sh_ref[s - 1, r0:r0 + rb, sl]
                term = dww_ref[k:k + 1, sl] * tap
                acc = term if acc is None else acc + term
            yd_ref[rr * rb:(rr + 1) * rb, sl] = acc + dwb_ref[:, sl]
    ydn = _silu(_layer_norm(yd_ref[...], cg_ref[...], cb_ref[...])).astype(BF16)
    out = _dot(yc_ref[...], woc_ref[...]) + _dot(ydn, wod_ref[...])
    o_ref[...] = x_ref[...] + gt_ref[pl.ds(r, 1), :] * out


def _odd_out(u, v, glu, x, mod, i, sp_w, sp_b_full, dw_w, dw_b, cn_g, cn_b, wo_c, wo_d, cfg):
    t_tok = x.shape[0]
    tm = TM_MIX
    small = lambda shape: pl.BlockSpec(shape, lambda t: (0,) * len(shape))
    return pl.pallas_call(
        functools.partial(_odd_out_kernel, cfg=cfg),
        grid=(t_tok // tm,),
        in_specs=[pl.BlockSpec((tm, D_C), lambda t: (t, 0)), pl.BlockSpec((tm, D_C), lambda t: (t, 0))]
        + _halo_specs(D_D, tm, HALO_D, t_tok, lambda t: t) + [
            pl.BlockSpec((tm, D_MODEL), lambda t: (t, 0)),
            _mod_spec(i, 5),
            small((N_HEADS_C, MLP_CHUNK, MLP_CHUNK)), small((MLP_CHUNK, D_C)),
            small((CONV_D, D_D)), small((1, D_D)), small((1, D_D)), small((1, D_D)),
            small((D_C, D_MODEL)), small((D_D, D_MODEL)),
        ],
        out_specs=pl.BlockSpec((tm, D_MODEL), lambda t: (t, 0)),
        out_shape=jax.ShapeDtypeStruct((t_tok, D_MODEL), F32),
        scratch_shapes=[pltpu.VMEM((tm + 2 * HALO_D, D_D), F32),
                        pltpu.VMEM((SUBLANES - 1, tm + 2 * HALO_D - SUBLANES, D_D), F32),
                        pltpu.VMEM((tm, D_C), BF16), pltpu.VMEM((tm, D_D), F32)],
        compiler_params=_params(("parallel",)),
        name="odd_out",
    )(u, v, glu, glu, glu, x, mod, sp_w, sp_b_full, dw_w, dw_b, cn_g, cn_b, wo_c, wo_d)


def _position_tables(dec_seq):
    quarter = D_MODEL // 4
    freqs = jnp.exp(-math.log(POS_BASE) * jnp.arange(quarter, dtype=F32) / quarter)
    ang_r = jnp.arange(dec_seq // GRID_W, dtype=F32)[:, None] * freqs
    ang_c = jnp.arange(GRID_W, dtype=F32)[:, None] * freqs
    return (jnp.concatenate([jnp.sin(ang_r), jnp.cos(ang_r)], axis=-1),
            jnp.concatenate([jnp.sin(ang_c), jnp.cos(ang_c)], axis=-1))


def _forward(cfg, x_prompt, x_sample, c, state_ssd, c_ctx, w_mod, b_mod, norm_g, ffn_w_gate, ffn_w_up,
             ffn_w_down, ev_w_in, ev_conv_w, ev_conv_b, ev_dt_bias, ev_a_log, ev_d_skip, ev_ssd_norm_g,
             ev_pool_w, ev_pool_scale, ev_w_out, od_w_in, od_v_ln_g, od_v_ln_b, od_sp_w, od_sp_b, od_dw_w,
             od_dw_b, od_cn_g, od_cn_b, od_w_out, final_norm_g):
    assert DEPTH == 4 and cfg.seq % TM_MIX == 0 and cfg.dec_seq % TM_FFN == 0 and (cfg.batch * cfg.seq) % TM_FFN == 0
    t_p = cfg.batch * cfg.seq
    hp = N_HEADS_A * HEAD_DIM_A
    row_tab, col_tab = _position_tables(cfg.dec_seq)

    cond8 = jnp.concatenate([c_ctx[None, :], c, jnp.zeros((SUBLANES - 1 - cfg.dec_batch, D_MODEL), F32)], axis=0)
    mod = _modulation(cond8, w_mod, b_mod)

    wg, wu, wd = ffn_w_gate.astype(BF16), ffn_w_up.astype(BF16), ffn_w_down.astype(BF16)
    norm_g4 = norm_g.reshape(DEPTH, 3, 1, D_MODEL)
    states = None
    for i in range(DEPTH):
        j = i // 2
        if i == 0:
            x = _ffn_first(x_prompt.reshape(t_p, D_MODEL), x_sample.reshape(-1, D_MODEL), row_tab, col_tab, mod,
                           norm_g4, wg, wu, wd, cfg)
        else:
            x = _ffn(x, mod, i, 0, norm_g4, wg, wu, wd, cfg)
        if i % 2 == 0:
            w = ev_w_in[j]
            o_dt = D_INNER_A + D_XBC
            o_pool = o_dt + 2 * N_HEADS_A
            w_all = jnp.concatenate(
                [w[:, :o_dt], w[:, o_pool:], w[:, o_dt:o_pool],
                 jnp.zeros((D_MODEL, DT_PAD - 2 * N_HEADS_A), F32)], axis=1).astype(BF16)
            z, xst, bc, pool_in, dt = _even_in(x, mod, i, norm_g[i, 1], w_all, ev_conv_w[j],
                                               ev_conv_b[j].reshape(1, D_XBC), cfg)
            lane_b = lambda v: jnp.broadcast_to(v.reshape(2 * N_HEADS_A, 1), (2 * N_HEADS_A, SSD_CHUNK))
            dsk_t = jnp.broadcast_to(ev_d_skip[j][:, None, None], (N_HEADS_A, HEAD_DIM_A, SSD_CHUNK)).reshape(hp, SSD_CHUNK)
            s0 = state_ssd[:, j].reshape(cfg.dec_batch, 2, hp, D_STATE)
            ya, states = _ssd(xst, bc, dt, z, lane_b(ev_dt_bias[j]), lane_b(ev_a_log[j]), dsk_t,
                              ev_ssd_norm_g[j].reshape(1, D_INNER_A), s0, states, j, cfg)
            x = _even_out(ya, pool_in, x, mod, i, ev_pool_w[j].astype(BF16), ev_pool_scale[j].reshape(1, D_POOL),
                          ev_w_out[j, :D_INNER_A].astype(BF16), ev_w_out[j, D_INNER_A:].astype(BF16), cfg)
        else:
            u, v, glu = _odd_in(x, mod, i, norm_g[i, 1], od_w_in[j].astype(BF16), od_v_ln_g[j], od_v_ln_b[j], cfg)
            spb = jnp.broadcast_to(od_sp_b[j].T[:, :, None], (MLP_CHUNK, N_HEADS_C, D_C // N_HEADS_C))
            x = _odd_out(u, v, glu, x, mod, i, od_sp_w[j].astype(BF16), spb.reshape(MLP_CHUNK, D_C), od_dw_w[j],
                         od_dw_b[j].reshape(1, D_D), od_cn_g[j].reshape(1, D_D), od_cn_b[j].reshape(1, D_D),
                         od_w_out[j, :D_C].astype(BF16), od_w_out[j, D_C:].astype(BF16), cfg)
        if i < DEPTH - 1:
            x = _ffn(x, mod, i, 1, norm_g4, wg, wu, wd, cfg)
        else:
            y_p, y_s = _ffn_last(x, mod, norm_g4, wg, wu, wd, final_norm_g, cfg)

    return (y_p.reshape(cfg.batch, cfg.seq, D_MODEL), y_s.reshape(cfg.dec_batch, cfg.dec_seq, D_MODEL),
            states.reshape(cfg.batch, N_EVEN, 2, N_HEADS_A, HEAD_DIM_A, D_STATE))


def kernel(x_prompt, x_sample, c, state_ssd, c_ctx, w_mod, b_mod, norm_g, ffn_w_gate, ffn_w_up, ffn_w_down, ev_w_in, ev_conv_w, ev_conv_b, ev_dt_bias, ev_a_log, ev_d_skip, ev_ssd_norm_g, ev_pool_w, ev_pool_scale, ev_w_out, od_w_in, od_v_ln_g, od_v_ln_b, od_sp_w, od_sp_b, od_dw_w, od_dw_b, od_cn_g, od_cn_b, od_w_out, final_norm_g):
    cfg = Cfg(x_prompt.shape[0], x_prompt.shape[1], x_sample.shape[0], x_sample.shape[1])
    return _forward(cfg, x_prompt, x_sample, c, state_ssd, c_ctx, w_mod, b_mod, norm_g, ffn_w_gate, ffn_w_up,
                    ffn_w_down, ev_w_in, ev_conv_w, ev_conv_b, ev_dt_bias, ev_a_log, ev_d_skip, ev_ssd_norm_g,
                    ev_pool_w, ev_pool_scale, ev_w_out, od_w_in, od_v_ln_g, od_v_ln_b, od_sp_w, od_sp_b, od_dw_w,
                    od_dw_b, od_cn_g, od_cn_b, od_w_out, final_norm_g)
```

```python
import collections
import functools
import math

import jax
import jax.numpy as jnp
from jax import lax
from jax.experimental import pallas as pl
from jax.experimental.pallas import tpu as pltpu

F32 = jnp.float32
BF16 = jnp.bfloat16

D_MODEL = 1024
DEPTH = 4
GRID_W = 64
POS_BASE = 10000.0
EPS = 1e-6
N_MOD = 9
D_FF = 2816
N_HEADS_A = 16
HEAD_DIM_A = 64
D_INNER_A = N_HEADS_A * HEAD_DIM_A
N_GROUPS_A = 2
HEADS_PER_GROUP = N_HEADS_A // N_GROUPS_A
D_STATE = 128
CONV_A = 5
SSD_CHUNK = 128
D_BC = N_GROUPS_A * D_STATE
D_XBC = D_INNER_A + 2 * D_BC
POOL_WINDOWS = (2, 4, 8, 16)
POOL_GROUP_DIM = 128
D_POOL = len(POOL_WINDOWS) * POOL_GROUP_DIM
N_HEADS_C = 8
MLP_CHUNK = 128
D_C = 1024
D_D = 1024
CONV_D = 31
N_EVEN = (DEPTH + 1) // 2

LANES = 128
SUBLANES = 8
DT_PAD = LANES
HALO_A = SUBLANES
HALO_D = 2 * SUBLANES
VMEM_LIMIT = 56 * 1024 * 1024

TM_FFN = 512
TF_FFN = 256
TM_PROJ = 512
TM_MIX = 256
CONV_COLS = 256
PROJ_COLS = 256
Cfg = collections.namedtuple("Cfg", "batch seq dec_batch dec_seq")


def _n_tokens(cfg):
    return cfg.batch * cfg.seq + cfg.dec_batch * cfg.dec_seq


def _tile_info(t, tm, cfg):
    npt = cfg.seq // tm
    nst = cfg.dec_seq // tm
    n_p = cfg.batch * npt
    is_prompt = t < n_p
    ts = jnp.maximum(t - n_p, 0)
    pos = jnp.where(is_prompt, lax.rem(t, npt), lax.rem(ts, nst))
    last = jnp.where(is_prompt, npt - 1, nst - 1)
    info = dict(
        is_prompt=is_prompt,
        is_start=pos == 0,
        is_end=pos == last,
        pos=pos,
        seq_len=jnp.where(is_prompt, cfg.seq, cfg.dec_seq),
    )
    return info


def _mod_row(t, tm, cfg):
    n_p = (cfg.batch * cfg.seq) // tm
    per = cfg.dec_seq // tm
    return jnp.where(t < n_p, 0, 1 + lax.div(jnp.maximum(t - n_p, 0), per))


def _sigmoid(x):
    return 0.5 + 0.5 * jnp.tanh(0.5 * x)


def _silu(x):
    h = 0.5 * x
    return h + h * jnp.tanh(h)


def _softplus(x):
    return jnp.maximum(x, 0.0) + jnp.log1p(jnp.exp(-jnp.abs(x)))


def _rms(x, g):
    return x * lax.rsqrt(jnp.mean(x * x, axis=-1, keepdims=True) + EPS) * g


def _layer_norm(x, g, b):
    xc = x - jnp.mean(x, axis=-1, keepdims=True)
    y = xc * lax.rsqrt(jnp.mean(xc * xc, axis=-1, keepdims=True) + EPS)
    return y * g + b


def _dot(a, b):
    return jnp.dot(a, b, preferred_element_type=F32)


def _dot_nt(a, b):
    return lax.dot_general(a, b, (((1,), (1,)), ((), ())), preferred_element_type=F32)


def _params(sem):
    return pltpu.CompilerParams(dimension_semantics=sem, vmem_limit_bytes=VMEM_LIMIT)


def _mod_kernel(c_ref, w_ref, b_ref, o_ref):
    c = c_ref[...]
    sc = _silu(c).astype(BF16)
    o_ref[...] = _dot(sc, w_ref[...].astype(BF16)) + b_ref[...]


def _modulation(cond8, w_mod, b_mod):
    b4 = b_mod.reshape(DEPTH, N_MOD, 1, D_MODEL)
    return pl.pallas_call(
        _mod_kernel,
        grid=(DEPTH, N_MOD),
        in_specs=[
            pl.BlockSpec((SUBLANES, D_MODEL), lambda i, k: (0, 0)),
            pl.BlockSpec((None, D_MODEL, D_MODEL), lambda i, k: (i, 0, k)),
            pl.BlockSpec((None, None, 1, D_MODEL), lambda i, k: (i, k, 0, 0)),
        ],
        out_specs=pl.BlockSpec((None, None, SUBLANES, D_MODEL), lambda i, k: (i, k, 0, 0)),
        out_shape=jax.ShapeDtypeStruct((DEPTH, N_MOD, SUBLANES, D_MODEL), F32),
        compiler_params=_params(("arbitrary", "arbitrary")),
        name="modulation",
    )(cond8, w_mod, b4)


def _mod_spec(i, k):
    return pl.BlockSpec((None, None, SUBLANES, D_MODEL), lambda t: (i, k, 0, 0))


def _ffn_body(x, r, sh_ref, sc_ref, gt_ref, g_ref, wg_ref, wu_ref, wd_ref, hn_ref, act_ref):
    hn = _rms(x, g_ref[...]) * (1.0 + sc_ref[pl.ds(r, 1), :]) + sh_ref[pl.ds(r, 1), :]
    hn_ref[...] = hn.astype(BF16)
    for f in range(D_FF // TF_FFN):
        sl = slice(f * TF_FFN, (f + 1) * TF_FFN)
        g = _dot(hn_ref[...], wg_ref[:, sl])
        u = _dot(hn_ref[...], wu_ref[:, sl])
        act_ref[:, sl] = (_silu(g) * u).astype(BF16)
    return x + (0.5 * gt_ref[pl.ds(r, 1), :]) * _dot(act_ref[...], wd_ref[...])


def _cast_rows(total, n_steps):
    rows = 2 * SUBLANES
    while rows * n_steps < total or total % rows:
        rows += 2 * SUBLANES
    return rows


def _cast_next(wgn_ref, wun_ref, wdn_ref, wgo_ref, wuo_ref, wdo_ref):
    wgo_ref[...] = wgn_ref[...].astype(BF16)
    wuo_ref[...] = wun_ref[...].astype(BF16)
    wdo_ref[...] = wdn_ref[...].astype(BF16)


def _ffn_kernel(*refs, cfg, first, last, cast):
    refs = list(refs)
    t = pl.program_id(0)
    n_p = (cfg.batch * cfg.seq) // TM_FFN
    is_prompt = t < n_p
    if first:
        xp_ref, xs_ref, rt_ref, ct_ref = refs[:4]
        del refs[:4]
    else:
        x_in_ref = refs.pop(0)
    sh_ref, sc_ref, gt_ref, g_ref, wg_ref, wu_ref, wd_ref = refs[:7]
    del refs[:7]
    if last:
        fg_ref = refs.pop(0)
    if cast:
        nxt_in = refs[:3]
        del refs[:3]
    outs = refs[:2 if last else 1]
    del refs[:len(outs)]
    if cast:
        nxt_out = refs[:3]
        del refs[:3]
    if first:
        x_ref = refs.pop(0)
    hn_ref, act_ref = refs

    if first:
        @pl.when(is_prompt)
        def _():
            x_ref[...] = xp_ref[...]

        @pl.when(jnp.logical_not(is_prompt))
        def _():
            half = D_MODEL // 2
            row0 = lax.rem(jnp.maximum(t - n_p, 0), cfg.dec_seq // TM_FFN) * (TM_FFN // GRID_W)
            for rr in range(TM_FFN // GRID_W):
                rows = slice(rr * GRID_W, (rr + 1) * GRID_W)
                x_ref[rows, 0:half] = xs_ref[rows, 0:half] + rt_ref[pl.ds(row0 + rr, 1), :]
                x_ref[rows, half:D_MODEL] = xs_ref[rows, half:D_MODEL] + ct_ref[...]
        x = x_ref[...]
    else:
        x = x_in_ref[...]

    r = _mod_row(t, TM_FFN, cfg)
    xn = _ffn_body(x, r, sh_ref, sc_ref, gt_ref, g_ref, wg_ref, wu_ref, wd_ref, hn_ref, act_ref)
    if last:
        y = _rms(xn, fg_ref[...])

        @pl.when(is_prompt)
        def _():
            outs[0][...] = y

        @pl.when(jnp.logical_not(is_prompt))
        def _():
            outs[1][...] = y
    else:
        outs[0][...] = xn
    if cast:
        _cast_next(*nxt_in, *nxt_out)


def _resident(shape, index):
    return pl.BlockSpec(shape, lambda t: index, pipeline_mode=pl.Buffered(1))


def _ffn(x, mod, i, which, norm_g4, w_cur, w_next, cfg, pos_tabs=None, final_g=None):
    first, last, cast = pos_tabs is not None, final_g is not None, w_next is not None
    t_p = cfg.batch * cfg.seq
    n_p = t_p // TM_FFN
    t_tok = _n_tokens(cfg)
    n_steps = t_tok // TM_FFN
    k0 = 6 * which
    row = pl.BlockSpec((TM_FFN, D_MODEL), lambda t: (t, 0))
    row_p = pl.BlockSpec((TM_FFN, D_MODEL), lambda t: (jnp.minimum(t, n_p - 1), 0))
    row_s = pl.BlockSpec((TM_FFN, D_MODEL), lambda t: (jnp.maximum(t - n_p, 0), 0))
    half = D_MODEL // 2

    in_specs, args = [], []
    if first:
        in_specs += [row_p, row_s, pl.BlockSpec((cfg.dec_seq // GRID_W, half), lambda t: (0, 0)),
                     pl.BlockSpec((GRID_W, half), lambda t: (0, 0))]
        args += [x[0], x[1], pos_tabs[0], pos_tabs[1]]
    else:
        in_specs.append(row)
        args.append(x)
    in_specs += [_mod_spec(i, k0), _mod_spec(i, k0 + 1), _mod_spec(i, k0 + 2),
                 pl.BlockSpec((None, None, 1, D_MODEL), lambda t: (i, 2 * which, 0, 0)),
                 _resident((D_MODEL, D_FF), (0, 0)), _resident((D_MODEL, D_FF), (0, 0)),
                 _resident((D_FF, D_MODEL), (0, 0))]
    args += [mod, mod, mod, norm_g4, *w_cur]
    if last:
        in_specs.append(pl.BlockSpec((1, D_MODEL), lambda t: (0, 0)))
        args.append(final_g.reshape(1, D_MODEL))
    out_specs = [row_p, row_s] if last else [row]
    out_shape = ([jax.ShapeDtypeStruct((t_p, D_MODEL), F32), jax.ShapeDtypeStruct((t_tok - t_p, D_MODEL), F32)]
                 if last else [jax.ShapeDtypeStruct((t_tok, D_MODEL), F32)])
    if cast:
        wg_n, wu_n, wd_n, i_n, which_n = w_next
        r_in, r_dn = _cast_rows(D_MODEL, n_steps), _cast_rows(D_FF, n_steps)
        blk_in = lambda t: jnp.minimum(t, D_MODEL // r_in - 1)
        blk_dn = lambda t: jnp.minimum(t, D_FF // r_dn - 1)
        in_specs += [pl.BlockSpec((None, None, r_in, D_FF), lambda t: (i_n, which_n, blk_in(t), 0)),
                     pl.BlockSpec((None, None, r_in, D_FF), lambda t: (i_n, which_n, blk_in(t), 0)),
                     pl.BlockSpec((None, None, r_dn, D_MODEL), lambda t: (i_n, which_n, blk_dn(t), 0))]
        args += [wg_n, wu_n, wd_n]
        out_specs += [pl.BlockSpec((r_in, D_FF), lambda t: (blk_in(t), 0)),
                      pl.BlockSpec((r_in, D_FF), lambda t: (blk_in(t), 0)),
                      pl.BlockSpec((r_dn, D_MODEL), lambda t: (blk_dn(t), 0))]
        out_shape += [jax.ShapeDtypeStruct((D_MODEL, D_FF), BF16), jax.ShapeDtypeStruct((D_MODEL, D_FF), BF16),
                      jax.ShapeDtypeStruct((D_FF, D_MODEL), BF16)]
    scratch = [pltpu.VMEM((TM_FFN, D_MODEL), BF16), pltpu.VMEM((TM_FFN, D_FF), BF16)]
    if first:
        scratch.insert(0, pltpu.VMEM((TM_FFN, D_MODEL), F32))
    outs = pl.pallas_call(
        functools.partial(_ffn_kernel, cfg=cfg, first=first, last=last, cast=cast),
        grid=(n_steps,),
        in_specs=in_specs,
        out_specs=out_specs,
        out_shape=out_shape,
        scratch_shapes=scratch,
        compiler_params=_params(("arbitrary",)),
        name="ffn",
    )(*args)
    n_act = 2 if last else 1
    return outs[:n_act], (tuple(outs[n_act:]) if cast else None)


def _norm_mod(x_ref, sh_ref, sc_ref, g_ref, cfg):
    r = _mod_row(pl.program_id(0), TM_PROJ, cfg)
    hn = _rms(x_ref[...], g_ref[...]) * (1.0 + sc_ref[pl.ds(r, 1), :]) + sh_ref[pl.ds(r, 1), :]
    return hn.astype(BF16)


def _even_in_kernel(xm_ref, xp_ref, xn_ref, sh_ref, sc_ref, g_ref, w_ref, cw_ref, cb_ref,
                    z_ref, xst_ref, bc_ref, pool_ref, dt_ref, hn_ref, ext_ref, *, cfg):
    tm, q = TM_MIX, SSD_CHUNK
    t = pl.program_id(0)
    info = _tile_info(t, tm, cfg)
    r = _mod_row(t, tm, cfg)
    scale, shift, g = 1.0 + sc_ref[pl.ds(r, 1), :], sh_ref[pl.ds(r, 1), :], g_ref[...]
    hn_ref[0:tm, :] = _rms(xm_ref[...], g) * scale + shift
    hn_ref[tm:tm + HALO_A, :] = _rms(xp_ref[...], g) * scale + shift
    hn_ref[tm + HALO_A:, :] = _rms(xn_ref[...], g) * scale + shift
    hn = hn_ref[...].astype(BF16)
    o_xbc, o_pool, o_dt = D_INNER_A, D_INNER_A + D_XBC, D_INNER_A + D_XBC + D_POOL
    base = HALO_A - CONV_A // 2
    zero = jnp.zeros((HALO_A, CONV_COLS), F32)
    others = ([(z_ref.at[:, c:c + CONV_COLS], c) for c in range(0, D_INNER_A, CONV_COLS)]
              + [(pool_ref.at[:, c:c + CONV_COLS], o_pool + c) for c in range(0, D_POOL, CONV_COLS)])
    for c0 in range(0, D_XBC, CONV_COLS):
        cols = slice(c0, c0 + CONV_COLS)
        xg = _dot(hn, w_ref[:, o_xbc + c0:o_xbc + c0 + CONV_COLS])
        o_ref, o_col = others[c0 // CONV_COLS]
        o_ref[...] = _dot(hn[0:tm], w_ref[:, o_col:o_col + CONV_COLS])
        ext_ref[0:HALO_A, cols] = jnp.where(info["is_start"], zero, xg[tm:tm + HALO_A])
        ext_ref[HALO_A:HALO_A + tm, cols] = xg[0:tm]
        ext_ref[HALO_A + tm:, cols] = jnp.where(info["is_end"], zero, xg[tm + HALO_A:])
        for ck in range(tm // q):
            for j in range(c0 // LANES, (c0 + CONV_COLS) // LANES):
                sl = slice(j * LANES, (j + 1) * LANES)
                r0 = base + ck * q
                acc = cw_ref[0:1, sl] * ext_ref[r0:r0 + q, sl]
                for k in range(1, CONV_A):
                    acc = acc + cw_ref[k:k + 1, sl] * ext_ref[r0 + k:r0 + k + q, sl]
                y = _silu(acc + cb_ref[:, sl])
                if j < D_INNER_A // LANES:
                    xst_ref[ck, j * LANES:(j + 1) * LANES, :] = y.T
                else:
                    bc_ref[ck * q:(ck + 1) * q, j * LANES - D_INNER_A:(j + 1) * LANES - D_INNER_A] = y.astype(BF16)
    dt_ref[...] = _dot(hn[0:tm], w_ref[:, o_dt:o_dt + DT_PAD])


def _even_in(x, mod, i, g, w, conv_w, conv_b, cfg):
    t_tok = x.shape[0]
    tm, q = TM_MIX, SSD_CHUNK
    n_tot = D_INNER_A + D_XBC + D_POOL + DT_PAD
    small = lambda shape: pl.BlockSpec(shape, lambda t: (0,) * len(shape))
    row_block = lambda width: pl.BlockSpec((tm, width), lambda t: (t, 0))
    return pl.pallas_call(
        functools.partial(_even_in_kernel, cfg=cfg),
        grid=(t_tok // tm,),
        in_specs=_halo_specs(D_MODEL, tm, HALO_A, t_tok, lambda t: t) + [
            _mod_spec(i, 3), _mod_spec(i, 4), small((1, D_MODEL)), small((D_MODEL, n_tot)),
            small((CONV_A, D_XBC)), small((1, D_XBC)),
        ],
        out_specs=[row_block(D_INNER_A),
                   pl.BlockSpec((tm // q, D_INNER_A, q), lambda t: (t, 0, 0)),
                   row_block(2 * D_BC), row_block(D_POOL), row_block(DT_PAD)],
        out_shape=[jax.ShapeDtypeStruct((t_tok, D_INNER_A), F32),
                   jax.ShapeDtypeStruct((t_tok // q, D_INNER_A, q), F32),
                   jax.ShapeDtypeStruct((t_tok, 2 * D_BC), BF16),
                   jax.ShapeDtypeStruct((t_tok, D_POOL), F32),
                   jax.ShapeDtypeStruct((t_tok, DT_PAD), F32)],
        scratch_shapes=[pltpu.VMEM((tm + 2 * HALO_A, D_MODEL), F32), pltpu.VMEM((tm + 2 * HALO_A, D_XBC), F32)],
        compiler_params=_params(("parallel",)),
        name="even_in",
    )(x, x, x, mod, mod, g.reshape(1, D_MODEL), w, conv_w, conv_b)


def _odd_in_kernel(x_ref, sh_ref, sc_ref, g_ref, w_ref, lg_ref, lb_ref, u_ref, v_ref, glu_ref, vg_ref, *, cfg):
    hn = _norm_mod(x_ref, sh_ref, sc_ref, g_ref, cfg)
    for c in range(0, D_C, PROJ_COLS):
        cols = slice(c, c + PROJ_COLS)
        u_ref[:, cols] = jax.nn.gelu(_dot(hn, w_ref[:, c:c + PROJ_COLS]))
        vg_ref[:, cols] = jax.nn.gelu(_dot(hn, w_ref[:, D_C + c:D_C + c + PROJ_COLS]))
        ga = _dot(hn, w_ref[:, 2 * D_C + c:2 * D_C + c + PROJ_COLS])
        gg = _dot(hn, w_ref[:, 2 * D_C + D_D + c:2 * D_C + D_D + c + PROJ_COLS])
        glu_ref[:, cols] = ga * _sigmoid(gg)
    v_ref[...] = _layer_norm(vg_ref[...], lg_ref[...], lb_ref[...]).astype(BF16)


def _odd_in(x, mod, i, g, w, ln_g, ln_b, cfg):
    t_tok = x.shape[0]
    row = lambda: pl.BlockSpec((1, D_MODEL), lambda t: (0, 0))
    return pl.pallas_call(
        functools.partial(_odd_in_kernel, cfg=cfg),
        grid=(t_tok // TM_PROJ,),
        in_specs=[
            pl.BlockSpec((TM_PROJ, D_MODEL), lambda t: (t, 0)),
            _mod_spec(i, 3), _mod_spec(i, 4),
            row(),
            pl.BlockSpec((D_MODEL, 2 * D_C + 2 * D_D), lambda t: (0, 0)),
            row(), row(),
        ],
        out_specs=[pl.BlockSpec((TM_PROJ, D_C), lambda t: (t, 0))] * 3,
        out_shape=[jax.ShapeDtypeStruct((t_tok, D_C), F32),
                   jax.ShapeDtypeStruct((t_tok, D_C), BF16),
                   jax.ShapeDtypeStruct((t_tok, D_D), F32)],
        scratch_shapes=[pltpu.VMEM((TM_PROJ, D_C), F32)],
        compiler_params=_params(("parallel",)),
        name="odd_in",
    )(x, mod, mod, g.reshape(1, D_MODEL), w, ln_g.reshape(1, D_C), ln_b.reshape(1, D_C))


def _dt_terms(dt_ref, dtb_ref, alog_ref, ck, rows):
    q = SSD_CHUNK
    dtt = dt_ref[ck * q:(ck + 1) * q, :].T[rows, :]
    dts = _softplus(dtt + dtb_ref[rows, :])
    return dts, dts * (-jnp.exp(alog_ref[rows, :]))


def _tri(lower_incl):
    ri = lax.broadcasted_iota(jnp.int32, (SSD_CHUNK, SSD_CHUNK), 0)
    ci = lax.broadcasted_iota(jnp.int32, (SSD_CHUNK, SSD_CHUNK), 1)
    return (ri >= ci) if lower_incl else (ri <= ci)


def _cumsum_lanes(v, mask):
    return jnp.dot(v, mask.astype(F32), precision=lax.Precision.HIGHEST, preferred_element_type=F32)


def _group_bc(bc_ref, ck, g):
    rows = slice(ck * SSD_CHUNK, (ck + 1) * SSD_CHUNK)
    b = bc_ref[rows, g * D_STATE:(g + 1) * D_STATE]
    c = bc_ref[rows, D_BC + g * D_STATE:D_BC + (g + 1) * D_STATE]
    return b, c


def _state_update(s_ref, xst_ref, xd_ref, bc_ref, ck, scale, cdec):
    hd = HEAD_DIM_A
    for h in range(N_HEADS_A):
        xd_ref[ck, h * hd:(h + 1) * hd, :] = (xst_ref[ck, h * hd:(h + 1) * hd, :] * scale[h:h + 1, :]).astype(BF16)
    gw = HEADS_PER_GROUP * hd
    for g in range(N_GROUPS_A):
        b, _ = _group_bc(bc_ref, ck, g)
        upd = _dot(xd_ref[ck, g * gw:(g + 1) * gw, :], b)
        for e in range(HEADS_PER_GROUP):
            h = g * HEADS_PER_GROUP + e
            s_ref[h * hd:(h + 1) * hd, :] = (s_ref[h * hd:(h + 1) * hd, :] * cdec[h:h + 1, :]
                                             + upd[e * hd:(e + 1) * hd, :])


def _ssd_fwd_chunk(ck, xst_ref, bc_ref, dt_ref, dtb_ref, alog_ref, dsk_ref, ypt_ref, xd_ref, s_ref):
    q, hd, nh = SSD_CHUNK, HEAD_DIM_A, N_HEADS_A
    dts, dta = _dt_terms(dt_ref, dtb_ref, alog_ref, ck, slice(0, 2 * nh))
    low, upp = _tri(True), _tri(False)
    acs_f = _cumsum_lanes(dta[0:nh], upp)
    rcs_b = _cumsum_lanes(dta[nh:2 * nh], low)
    src = jnp.concatenate([acs_f - jnp.log(dts[0:nh]), rcs_b - jnp.log(dts[nh:2 * nh]),
                           jnp.zeros((LANES - 2 * nh, q), F32)], axis=0).T
    e_acs = jnp.exp(acs_f)
    neg_inf = jnp.float32(-jnp.inf)

    gw = HEADS_PER_GROUP * hd
    for g in range(N_GROUPS_A):
        b, c = _group_bc(bc_ref, ck, g)
        sct = _dot_nt(b, c)
        yoff = _dot_nt(s_ref[g * gw:(g + 1) * gw, :].astype(BF16), c)
        for e in range(HEADS_PER_GROUP):
            h = g * HEADS_PER_GROUP + e
            rows = slice(h * hd, (h + 1) * hd)
            seg_f = acs_f[h:h + 1, :] - src[:, h:h + 1]
            seg_b = rcs_b[h:h + 1, :] - src[:, nh + h:nh + h + 1]
            wt = sct * (jnp.exp(jnp.where(upp, seg_f, neg_inf)) + jnp.exp(jnp.where(low, seg_b, neg_inf)))
            xh = xst_ref[ck, rows, :]
            yd = _dot(xh.astype(BF16), wt.astype(BF16))
            ypt_ref[ck, rows, :] = yd + yoff[e * hd:(e + 1) * hd, :] * e_acs[h:h + 1, :] + dsk_ref[rows, :] * xh

    tot = acs_f[:, q - 1:q]
    scale = dts[0:nh] * jnp.exp(tot - acs_f)
    cdec = jnp.exp(jnp.broadcast_to(tot, (nh, D_STATE)))
    _state_update(s_ref, xst_ref, xd_ref, bc_ref, ck, scale, cdec)


def _ssd_fwd_kernel(xst_ref, bc_ref, dt_ref, dtb_ref, alog_ref, dsk_ref, s0_ref, ypt_ref, sfin_ref, xd_ref, s_ref,
                    *, cfg):
    info = _tile_info(pl.program_id(0), TM_MIX, cfg)

    @pl.when(jnp.logical_and(info["is_start"], info["is_prompt"]))
    def _():
        s_ref[...] = jnp.zeros_like(s_ref)

    @pl.when(jnp.logical_and(info["is_start"], jnp.logical_not(info["is_prompt"])))
    def _():
        s_ref[...] = s0_ref[...]

    for ck in range(TM_MIX // SSD_CHUNK):
        _ssd_fwd_chunk(ck, xst_ref, bc_ref, dt_ref, dtb_ref, alog_ref, dsk_ref, ypt_ref, xd_ref, s_ref)

    @pl.when(jnp.logical_and(info["is_end"], info["is_prompt"]))
    def _():
        sfin_ref[...] = s_ref[...]


def _ssd_bwd_chunk(ck, dt_ref, z_ref, ypt_ref, xst_ref, bc_ref, dtb_ref, alog_ref, yt_ref, y_ref, xd_ref, s_ref):
    q, hd, nh = SSD_CHUNK, HEAD_DIM_A, N_HEADS_A
    dts, dta = _dt_terms(dt_ref, dtb_ref, alog_ref, ck, slice(nh, 2 * nh))
    rcs_b = _cumsum_lanes(dta, _tri(True))
    e_rcs = jnp.exp(rcs_b)

    gw = HEADS_PER_GROUP * hd
    for g in range(N_GROUPS_A):
        _, c = _group_bc(bc_ref, ck, g)
        yoff = _dot_nt(s_ref[g * gw:(g + 1) * gw, :].astype(BF16), c)
        for e in range(HEADS_PER_GROUP):
            h = g * HEADS_PER_GROUP + e
            yt_ref[ck, h * hd:(h + 1) * hd, :] = (ypt_ref[ck, h * hd:(h + 1) * hd, :]
                                                  + yoff[e * hd:(e + 1) * hd, :] * e_rcs[h:h + 1, :])

    rows = slice(ck * q, (ck + 1) * q)
    for j in range(D_INNER_A // LANES):
        sl = slice(j * LANES, (j + 1) * LANES)
        y_ref[rows, sl] = yt_ref[ck, j * LANES:(j + 1) * LANES, :].T * _silu(z_ref[rows, sl])

    tot = rcs_b[:, 0:1]
    scale = dts * jnp.exp(tot - rcs_b)
    cdec = jnp.exp(jnp.broadcast_to(tot, (nh, D_STATE)))
    _state_update(s_ref, xst_ref, xd_ref, bc_ref, ck, scale, cdec)


def _ssd_bwd_kernel(dt_ref, z_ref, ypt_ref, xst_ref, bc_ref, dtb_ref, alog_ref, ng_ref, s0_ref, sf_ref, *rest,
                    cfg, n_tiles, layer):
    if layer == 0:
        ya_ref, so_ref, xd_ref, yt_ref, y_ref, s_ref = rest
    else:
        sprev_ref, ya_ref, so_ref, xd_ref, yt_ref, y_ref, s_ref = rest
    info = _tile_info(n_tiles - 1 - pl.program_id(0), TM_MIX, cfg)

    @pl.when(jnp.logical_and(info["is_end"], info["is_prompt"]))
    def _():
        s_ref[...] = jnp.zeros_like(s_ref)

    @pl.when(jnp.logical_and(info["is_end"], jnp.logical_not(info["is_prompt"])))
    def _():
        s_ref[...] = s0_ref[...]

    for ck in reversed(range(TM_MIX // SSD_CHUNK)):
        _ssd_bwd_chunk(ck, dt_ref, z_ref, ypt_ref, xst_ref, bc_ref, dtb_ref, alog_ref, yt_ref, y_ref, xd_ref, s_ref)
    ya_ref[...] = _rms(y_ref[...], ng_ref[...]).astype(BF16)

    @pl.when(jnp.logical_and(info["is_start"], info["is_prompt"]))
    def _():
        if layer == 0:
            so_ref[0] = sf_ref[...]
            so_ref[1] = s_ref[...]
        else:
            so_ref[0] = sprev_ref[...]
            so_ref[1, 0] = sf_ref[...]
            so_ref[1, 1] = s_ref[...]


def _halo_specs(width, tile, halo, n_rows, chunk_of):
    per = tile // halo
    last = n_rows // halo - 1
    return [
        pl.BlockSpec((tile, width), lambda i: (chunk_of(i), 0)),
        pl.BlockSpec((halo, width), lambda i: (jnp.maximum(chunk_of(i) * per - 1, 0), 0)),
        pl.BlockSpec((halo, width), lambda i: (jnp.minimum((chunk_of(i) + 1) * per, last), 0)),
    ]


def _ssd(xst, bc, dt, z, dtb, alog, dsk_t, norm_g, s0, s_prev, layer, cfg):
    t_tok = bc.shape[0]
    q, tm = SSD_CHUNK, TM_MIX
    cpt = tm // q
    n_tiles = t_tok // tm
    hp = N_HEADS_A * HEAD_DIM_A
    npt = cfg.seq // tm
    nst = cfg.dec_seq // tm
    n_pt = cfg.batch * npt

    def s_seq(t):
        return jnp.clip(lax.div(jnp.maximum(t - n_pt, 0), nst), 0, cfg.dec_batch - 1)

    def p_seq(t):
        return jnp.minimum(lax.div(t, npt), cfg.batch - 1)

    small = lambda shape: pl.BlockSpec(shape, lambda i: (0,) * len(shape))
    head_rows = small((2 * N_HEADS_A, q))

    ypt, sf = pl.pallas_call(
        functools.partial(_ssd_fwd_kernel, cfg=cfg),
        grid=(n_tiles,),
        in_specs=[
            pl.BlockSpec((cpt, hp, q), lambda i: (i, 0, 0)),
            pl.BlockSpec((tm, 2 * D_BC), lambda i: (i, 0)),
            pl.BlockSpec((tm, DT_PAD), lambda i: (i, 0)),
            head_rows, head_rows, small((hp, q)),
            pl.BlockSpec((None, None, hp, D_STATE), lambda i: (s_seq(i), 0, 0, 0)),
        ],
        out_specs=[pl.BlockSpec((cpt, hp, q), lambda i: (i, 0, 0)),
                   pl.BlockSpec((None, hp, D_STATE), lambda i: (p_seq(i), 0, 0))],
        out_shape=[jax.ShapeDtypeStruct((t_tok // q, hp, q), F32),
                   jax.ShapeDtypeStruct((cfg.batch, hp, D_STATE), F32)],
        scratch_shapes=[pltpu.VMEM((cpt, hp, q), BF16), pltpu.VMEM((hp, D_STATE), F32)],
        compiler_params=_params(("arbitrary",)),
        name="ssd_fwd",
    )(xst, bc, dt, dtb, alog, dsk_t, s0)

    rev = lambda i: n_tiles - 1 - i
    in_specs = [
        pl.BlockSpec((tm, DT_PAD), lambda i: (rev(i), 0)),
        pl.BlockSpec((tm, D_INNER_A), lambda i: (rev(i), 0)),
        pl.BlockSpec((cpt, hp, q), lambda i: (rev(i), 0, 0)),
        pl.BlockSpec((cpt, hp, q), lambda i: (rev(i), 0, 0)),
        pl.BlockSpec((tm, 2 * D_BC), lambda i: (rev(i), 0)),
        head_rows, head_rows, small((1, D_INNER_A)),
        pl.BlockSpec((None, None, hp, D_STATE), lambda i: (s_seq(rev(i)), 1, 0, 0)),
        pl.BlockSpec((None, hp, D_STATE), lambda i: (p_seq(rev(i)), 0, 0)),
    ]
    args = [dt, z, ypt, xst, bc, dtb, alog, norm_g, s0, sf]
    if layer == 0:
        so_block, so_shape = (None, 2, hp, D_STATE), (cfg.batch, 2, hp, D_STATE)
        so_index = lambda i: (p_seq(rev(i)), 0, 0, 0)
    else:
        in_specs.append(pl.BlockSpec((None, 2, hp, D_STATE), lambda i: (p_seq(rev(i)), 0, 0, 0)))
        args.append(s_prev)
        so_block, so_shape = (None, 2, 2, hp, D_STATE), (cfg.batch, 2, 2, hp, D_STATE)
        so_index = lambda i: (p_seq(rev(i)), 0, 0, 0, 0)
    ya, so = pl.pallas_call(
        functools.partial(_ssd_bwd_kernel, cfg=cfg, n_tiles=n_tiles, layer=layer),
        grid=(n_tiles,),
        in_specs=in_specs,
        out_specs=[pl.BlockSpec((tm, D_INNER_A), lambda i: (rev(i), 0)), pl.BlockSpec(so_block, so_index)],
        out_shape=[jax.ShapeDtypeStruct((t_tok, D_INNER_A), BF16), jax.ShapeDtypeStruct(so_shape, F32)],
        scratch_shapes=[pltpu.VMEM((cpt, hp, q), BF16), pltpu.VMEM((cpt, hp, q), F32),
                        pltpu.VMEM((tm, D_INNER_A), F32), pltpu.VMEM((hp, D_STATE), F32)],
        compiler_params=_params(("arbitrary",)),
        name="ssd_bwd",
    )(*args)
    return ya, so


def _even_out_kernel(ya_ref, pm_ref, pp_ref, pn_ref, x_ref, gt_ref, pw_ref, ps_ref, woa_ref, wop_ref,
                     o_ref, ext_ref, yp_ref, *, cfg):
    tm = TM_MIX
    t = pl.program_id(0)
    info = _tile_info(t, tm, cfg)
    r = _mod_row(t, tm, cfg)
    zero = jnp.zeros((HALO_A, D_POOL), F32)
    ext_ref[0:HALO_A, :] = jnp.where(info["is_start"], zero, pp_ref[...])
    ext_ref[HALO_A:HALO_A + tm, :] = pm_ref[...]
    ext_ref[HALO_A + tm:, :] = jnp.where(info["is_end"], zero, pn_ref[...])
    pos = info["pos"] * tm + lax.broadcasted_iota(jnp.int32, (tm, POOL_GROUP_DIM), 0)
    oc = D_MODEL // len(POOL_WINDOWS)
    for gi, win in enumerate(POOL_WINDOWS):
        o_ref[:, gi * oc:(gi + 1) * oc] = _dot(ya_ref[...], woa_ref[:, gi * oc:(gi + 1) * oc])
        sl = slice(gi * POOL_GROUP_DIM, (gi + 1) * POOL_GROUP_DIM)
        lo = HALO_A - win // 2
        s = ext_ref[lo:lo + tm, sl]
        for j in range(1, win):
            s = s + ext_ref[lo + j:lo + j + tm, sl]
        cnt = (jnp.minimum(pos - win // 2 + win, info["seq_len"]) - jnp.maximum(pos - win // 2, 0)).astype(F32)
        pooled = s / cnt - ext_ref[HALO_A:HALO_A + tm, sl]
        yp_ref[:, sl] = (_dot(pooled.astype(BF16), pw_ref[gi]) * ps_ref[:, sl]).astype(BF16)
    out = o_ref[...] + _dot(yp_ref[...], wop_ref[...])
    o_ref[...] = x_ref[...] + gt_ref[pl.ds(r, 1), :] * out


def _even_out(ya, pool_in, x, mod, i, pool_w, pool_scale, wo_a, wo_p, cfg):
    t_tok = x.shape[0]
    tm = TM_MIX
    small = lambda shape: pl.BlockSpec(shape, lambda t: (0,) * len(shape))
    return pl.pallas_call(
        functools.partial(_even_out_kernel, cfg=cfg),
        grid=(t_tok // tm,),
        in_specs=[pl.BlockSpec((tm, D_INNER_A), lambda t: (t, 0))]
        + _halo_specs(D_POOL, tm, HALO_A, t_tok, lambda t: t) + [
            pl.BlockSpec((tm, D_MODEL), lambda t: (t, 0)),
            _mod_spec(i, 5),
            small((len(POOL_WINDOWS), POOL_GROUP_DIM, POOL_GROUP_DIM)), small((1, D_POOL)),
            small((D_INNER_A, D_MODEL)), small((D_POOL, D_MODEL)),
        ],
        out_specs=pl.BlockSpec((tm, D_MODEL), lambda t: (t, 0)),
        out_shape=jax.ShapeDtypeStruct((t_tok, D_MODEL), F32),
        scratch_shapes=[pltpu.VMEM((tm + 2 * HALO_A, D_POOL), F32), pltpu.VMEM((tm, D_POOL), BF16)],
        compiler_params=_params(("parallel",)),
        name="even_out",
    )(ya, pool_in, pool_in, pool_in, x, mod, pool_w, pool_scale, wo_a, wo_p)


def _odd_out_kernel(u_ref, v_ref, gm_ref, gp_ref, gn_ref, x_ref, gt_ref, spw_ref, spb_ref, dww_ref, dwb_ref,
                    cg_ref, cb_ref, woc_ref, wod_ref, o_ref, ext_ref, sh_ref, yc_ref, yd_ref, *, cfg):
    tm = TM_MIX
    t = pl.program_id(0)
    info = _tile_info(t, tm, cfg)
    r = _mod_row(t, tm, cfg)
    hc = D_C // N_HEADS_C
    for ck in range(tm // MLP_CHUNK):
        rows = slice(ck * MLP_CHUNK, (ck + 1) * MLP_CHUNK)
        for h in range(N_HEADS_C):
            sl = slice(h * hc, (h + 1) * hc)
            sv = _dot(spw_ref[h], v_ref[rows, sl]) + spb_ref[:, sl]
            yc_ref[rows, sl] = (u_ref[rows, sl] * sv).astype(BF16)
    zero = jnp.zeros((HALO_D, D_D), F32)
    ext_ref[0:HALO_D, :] = jnp.where(info["is_start"], zero, gp_ref[...])
    ext_ref[HALO_D:HALO_D + tm, :] = gm_ref[...]
    ext_ref[HALO_D + tm:, :] = jnp.where(info["is_end"], zero, gn_ref[...])
    span = tm + 2 * HALO_D - SUBLANES
    base = HALO_D - CONV_D // 2
    rb = MLP_CHUNK
    for j in range(D_D // LANES):
        sl = slice(j * LANES, (j + 1) * LANES)
        if (j * LANES) % PROJ_COLS == 0:
            oc = slice(j * LANES, j * LANES + PROJ_COLS)
            o_ref[:, oc] = _dot(yc_ref[...], woc_ref[:, oc])
        for s in range(1, SUBLANES):
            sh_ref[s - 1, :, sl] = ext_ref[s:s + span, sl]
        for rr in range(tm // rb):
            acc = None
            for k in range(CONV_D):
                q8, s = divmod(base + k, SUBLANES)
                r0 = q8 * SUBLANES + rr * rb
                tap = ext_ref[r0:r0 + rb, sl] if s == 0 else sh_ref[s - 1, r0:r0 + rb, sl]
                term = dww_ref[k:k + 1, sl] * tap
                acc = term if acc is None else acc + term
            yd_ref[rr * rb:(rr + 1) * rb, sl] = acc + dwb_ref[:, sl]
    ydn = _silu(_layer_norm(yd_ref[...], cg_ref[...], cb_ref[...])).astype(BF16)
    out = o_ref[...] + _dot(ydn, wod_ref[...])
    o_ref[...] = x_ref[...] + gt_ref[pl.ds(r, 1), :] * out


def _odd_out(u, v, glu, x, mod, i, sp_w, sp_b_full, dw_w, dw_b, cn_g, cn_b, wo_c, wo_d, cfg):
    t_tok = x.shape[0]
    tm = TM_MIX
    small = lambda shape: pl.BlockSpec(shape, lambda t: (0,) * len(shape))
    return pl.pallas_call(
        functools.partial(_odd_out_kernel, cfg=cfg),
        grid=(t_tok // tm,),
        in_specs=[pl.BlockSpec((tm, D_C), lambda t: (t, 0)), pl.BlockSpec((tm, D_C), lambda t: (t, 0))]
        + _halo_specs(D_D, tm, HALO_D, t_tok, lambda t: t) + [
            pl.BlockSpec((tm, D_MODEL), lambda t: (t, 0)),
            _mod_spec(i, 5),
            small((N_HEADS_C, MLP_CHUNK, MLP_CHUNK)), small((MLP_CHUNK, D_C)),
            small((CONV_D, D_D)), small((1, D_D)), small((1, D_D)), small((1, D_D)),
            small((D_C, D_MODEL)), small((D_D, D_MODEL)),
        ],
        out_specs=pl.BlockSpec((tm, D_MODEL), lambda t: (t, 0)),
        out_shape=jax.ShapeDtypeStruct((t_tok, D_MODEL), F32),
        scratch_shapes=[pltpu.VMEM((tm + 2 * HALO_D, D_D), F32),
                        pltpu.VMEM((SUBLANES - 1, tm + 2 * HALO_D - SUBLANES, D_D), F32),
                        pltpu.VMEM((tm, D_C), BF16), pltpu.VMEM((tm, D_D), F32)],
        compiler_params=_params(("parallel",)),
        name="odd_out",
    )(u, v, glu, glu, glu, x, mod, sp_w, sp_b_full, dw_w, dw_b, cn_g, cn_b, wo_c, wo_d)


def _position_tables(dec_seq):
    quarter = D_MODEL // 4
    freqs = jnp.exp(-math.log(POS_BASE) * jnp.arange(quarter, dtype=F32) / quarter)
    ang_r = jnp.arange(dec_seq // GRID_W, dtype=F32)[:, None] * freqs
    ang_c = jnp.arange(GRID_W, dtype=F32)[:, None] * freqs
    return (jnp.concatenate([jnp.sin(ang_r), jnp.cos(ang_r)], axis=-1),
            jnp.concatenate([jnp.sin(ang_c), jnp.cos(ang_c)], axis=-1))


def _forward(cfg, x_prompt, x_sample, c, state_ssd, c_ctx, w_mod, b_mod, norm_g, ffn_w_gate, ffn_w_up,
             ffn_w_down, ev_w_in, ev_conv_w, ev_conv_b, ev_dt_bias, ev_a_log, ev_d_skip, ev_ssd_norm_g,
             ev_pool_w, ev_pool_scale, ev_w_out, od_w_in, od_v_ln_g, od_v_ln_b, od_sp_w, od_sp_b, od_dw_w,
             od_dw_b, od_cn_g, od_cn_b, od_w_out, final_norm_g):
    assert DEPTH == 4 and cfg.seq % TM_MIX == 0 and cfg.dec_seq % TM_FFN == 0 and (cfg.batch * cfg.seq) % TM_FFN == 0
    t_p = cfg.batch * cfg.seq
    hp = N_HEADS_A * HEAD_DIM_A
    row_tab, col_tab = _position_tables(cfg.dec_seq)

    cond8 = jnp.concatenate([c_ctx[None, :], c, jnp.zeros((SUBLANES - 1 - cfg.dec_batch, D_MODEL), F32)], axis=0)
    mod = _modulation(cond8, w_mod, b_mod)

    w_ffn = tuple(w[0, 0].astype(BF16) for w in (ffn_w_gate, ffn_w_up, ffn_w_down))
    f32_ffn = (ffn_w_gate, ffn_w_up, ffn_w_down)
    norm_g4 = norm_g.reshape(DEPTH, 3, 1, D_MODEL)
    states = None
    for i in range(DEPTH):
        j = i // 2
        if i == 0:
            (x,), w_ffn = _ffn((x_prompt.reshape(t_p, D_MODEL), x_sample.reshape(-1, D_MODEL)), mod, i, 0, norm_g4,
                               w_ffn, f32_ffn + (i, 1), cfg, pos_tabs=(row_tab, col_tab))
        else:
            (x,), w_ffn = _ffn(x, mod, i, 0, norm_g4, w_ffn, f32_ffn + (i, 1), cfg)
        if i % 2 == 0:
            w = ev_w_in[j]
            o_dt = D_INNER_A + D_XBC
            o_pool = o_dt + 2 * N_HEADS_A
            w_all = jnp.concatenate(
                [w[:, :o_dt], w[:, o_pool:], w[:, o_dt:o_pool],
                 jnp.zeros((D_MODEL, DT_PAD - 2 * N_HEADS_A), F32)], axis=1).astype(BF16)
            z, xst, bc, pool_in, dt = _even_in(x, mod, i, norm_g[i, 1], w_all, ev_conv_w[j],
                                               ev_conv_b[j].reshape(1, D_XBC), cfg)
            lane_b = lambda v: jnp.broadcast_to(v.reshape(2 * N_HEADS_A, 1), (2 * N_HEADS_A, SSD_CHUNK))
            dsk_t = jnp.broadcast_to(ev_d_skip[j][:, None, None], (N_HEADS_A, HEAD_DIM_A, SSD_CHUNK)).reshape(hp, SSD_CHUNK)
            s0 = state_ssd[:, j].reshape(cfg.dec_batch, 2, hp, D_STATE)
            ya, states = _ssd(xst, bc, dt, z, lane_b(ev_dt_bias[j]), lane_b(ev_a_log[j]), dsk_t,
                              ev_ssd_norm_g[j].reshape(1, D_INNER_A), s0, states, j, cfg)
            x = _even_out(ya, pool_in, x, mod, i, ev_pool_w[j].astype(BF16), ev_pool_scale[j].reshape(1, D_POOL),
                          ev_w_out[j, :D_INNER_A].astype(BF16), ev_w_out[j, D_INNER_A:].astype(BF16), cfg)
        else:
            u, v, glu = _odd_in(x, mod, i, norm_g[i, 1], od_w_in[j].astype(BF16), od_v_ln_g[j], od_v_ln_b[j], cfg)
            spb = jnp.broadcast_to(od_sp_b[j].T[:, :, None], (MLP_CHUNK, N_HEADS_C, D_C // N_HEADS_C))
            x = _odd_out(u, v, glu, x, mod, i, od_sp_w[j].astype(BF16), spb.reshape(MLP_CHUNK, D_C), od_dw_w[j],
                         od_dw_b[j].reshape(1, D_D), od_cn_g[j].reshape(1, D_D), od_cn_b[j].reshape(1, D_D),
                         od_w_out[j, :D_C].astype(BF16), od_w_out[j, D_C:].astype(BF16), cfg)
        if i < DEPTH - 1:
            (x,), w_ffn = _ffn(x, mod, i, 1, norm_g4, w_ffn, f32_ffn + (i + 1, 0), cfg)
        else:
            (y_p, y_s), _ = _ffn(x, mod, i, 1, norm_g4, w_ffn, None, cfg, final_g=final_norm_g)

    return (y_p.reshape(cfg.batch, cfg.seq, D_MODEL), y_s.reshape(cfg.dec_batch, cfg.dec_seq, D_MODEL),
            states.reshape(cfg.batch, N_EVEN, 2, N_HEADS_A, HEAD_DIM_A, D_STATE))


def kernel(x_prompt, x_sample, c, state_ssd, c_ctx, w_mod, b_mod, norm_g, ffn_w_gate, ffn_w_up, ffn_w_down, ev_w_in, ev_conv_w, ev_conv_b, ev_dt_bias, ev_a_log, ev_d_skip, ev_ssd_norm_g, ev_pool_w, ev_pool_scale, ev_w_out, od_w_in, od_v_ln_g, od_v_ln_b, od_sp_w, od_sp_b, od_dw_w, od_dw_b, od_cn_g, od_cn_b, od_w_out, final_norm_g):
    cfg = Cfg(x_prompt.shape[0], x_prompt.shape[1], x_sample.shape[0], x_sample.shape[1])
    return _forward(cfg, x_prompt, x_sample, c, state_ssd, c_ctx, w_mod, b_mod, norm_g, ffn_w_gate, ffn_w_up,
                    ffn_w_down, ev_w_in, ev_conv_w, ev_conv_b, ev_dt_bias, ev_a_log, ev_d_skip, ev_ssd_norm_g,
                    ev_pool_w, ev_pool_scale, ev_w_out, od_w_in, od_v_ln_g, od_v_ln_b, od_sp_w, od_sp_b, od_dw_w,
                    od_dw_b, od_cn_g, od_cn_b, od_w_out, final_norm_g)
```

```python
import collections
import functools
import math

import jax
import jax.numpy as jnp
from jax import lax
from jax.experimental import pallas as pl
from jax.experimental.pallas import tpu as pltpu

F32 = jnp.float32
BF16 = jnp.bfloat16

D_MODEL = 1024
DEPTH = 4
GRID_W = 64
POS_BASE = 10000.0
EPS = 1e-6
N_MOD = 9
D_FF = 2816
N_HEADS_A = 16
HEAD_DIM_A = 64
D_INNER_A = N_HEADS_A * HEAD_DIM_A
N_GROUPS_A = 2
HEADS_PER_GROUP = N_HEADS_A // N_GROUPS_A
D_STATE = 128
CONV_A = 5
SSD_CHUNK = 128
D_BC = N_GROUPS_A * D_STATE
D_XBC = D_INNER_A + 2 * D_BC
POOL_WINDOWS = (2, 4, 8, 16)
POOL_GROUP_DIM = 128
D_POOL = len(POOL_WINDOWS) * POOL_GROUP_DIM
N_HEADS_C = 8
MLP_CHUNK = 128
D_C = 1024
D_D = 1024
CONV_D = 31
N_EVEN = (DEPTH + 1) // 2

LANES = 128
SUBLANES = 8
DT_PAD = LANES
HALO_A = SUBLANES
HALO_D = 2 * SUBLANES
VMEM_LIMIT = 56 * 1024 * 1024

TM_FFN = 512
TF_FFN = 256
TM_PROJ = 512
TM_MIX = 256
CONV_COLS = 256
PROJ_COLS = 256
NSUB = 2
Cfg = collections.namedtuple("Cfg", "batch seq dec_batch dec_seq")


def _n_tokens(cfg):
    return cfg.batch * cfg.seq + cfg.dec_batch * cfg.dec_seq


def _tile_info(t, tm, cfg):
    npt = cfg.seq // tm
    nst = cfg.dec_seq // tm
    n_p = cfg.batch * npt
    is_prompt = t < n_p
    ts = jnp.maximum(t - n_p, 0)
    pos = jnp.where(is_prompt, lax.rem(t, npt), lax.rem(ts, nst))
    last = jnp.where(is_prompt, npt - 1, nst - 1)
    info = dict(
        is_prompt=is_prompt,
        is_start=pos == 0,
        is_end=pos == last,
        pos=pos,
        seq_len=jnp.where(is_prompt, cfg.seq, cfg.dec_seq),
    )
    return info


def _mod_row(t, tm, cfg):
    n_p = (cfg.batch * cfg.seq) // tm
    per = cfg.dec_seq // tm
    return jnp.where(t < n_p, 0, 1 + lax.div(jnp.maximum(t - n_p, 0), per))


def _sigmoid(x):
    return 0.5 + 0.5 * jnp.tanh(0.5 * x)


def _silu(x):
    h = 0.5 * x
    return h + h * jnp.tanh(h)


def _softplus(x):
    return jnp.maximum(x, 0.0) + jnp.log1p(jnp.exp(-jnp.abs(x)))


def _rms(x, g):
    return x * lax.rsqrt(jnp.mean(x * x, axis=-1, keepdims=True) + EPS) * g


def _layer_norm(x, g, b):
    xc = x - jnp.mean(x, axis=-1, keepdims=True)
    y = xc * lax.rsqrt(jnp.mean(xc * xc, axis=-1, keepdims=True) + EPS)
    return y * g + b


def _dot(a, b):
    return jnp.dot(a, b, preferred_element_type=F32)


def _dot_nt(a, b):
    return lax.dot_general(a, b, (((1,), (1,)), ((), ())), preferred_element_type=F32)


def _params(sem):
    return pltpu.CompilerParams(dimension_semantics=sem, vmem_limit_bytes=VMEM_LIMIT)


def _mod_kernel(c_ref, w_ref, b_ref, o_ref):
    c = c_ref[...]
    sc = _silu(c).astype(BF16)
    o_ref[...] = _dot(sc, w_ref[...].astype(BF16)) + b_ref[...]


def _modulation(cond8, w_mod, b_mod):
    b4 = b_mod.reshape(DEPTH, N_MOD, 1, D_MODEL)
    return pl.pallas_call(
        _mod_kernel,
        grid=(DEPTH, N_MOD),
        in_specs=[
            pl.BlockSpec((SUBLANES, D_MODEL), lambda i, k: (0, 0)),
            pl.BlockSpec((None, D_MODEL, D_MODEL), lambda i, k: (i, 0, k)),
            pl.BlockSpec((None, None, 1, D_MODEL), lambda i, k: (i, k, 0, 0)),
        ],
        out_specs=pl.BlockSpec((None, None, SUBLANES, D_MODEL), lambda i, k: (i, k, 0, 0)),
        out_shape=jax.ShapeDtypeStruct((DEPTH, N_MOD, SUBLANES, D_MODEL), F32),
        compiler_params=_params(("arbitrary", "arbitrary")),
        name="modulation",
    )(cond8, w_mod, b4)


def _mod_spec(i, k):
    return pl.BlockSpec((None, None, SUBLANES, D_MODEL), lambda t: (i, k, 0, 0))


def _ffn_body(x, r, sh_ref, sc_ref, gt_ref, g_ref, wg_ref, wu_ref, wd_ref, hn_ref, act_ref):
    hn = _rms(x, g_ref[...]) * (1.0 + sc_ref[pl.ds(r, 1), :]) + sh_ref[pl.ds(r, 1), :]
    hn_ref[...] = hn.astype(BF16)
    for f in range(D_FF // TF_FFN):
        sl = slice(f * TF_FFN, (f + 1) * TF_FFN)
        g = _dot(hn_ref[...], wg_ref[:, sl])
        u = _dot(hn_ref[...], wu_ref[:, sl])
        act_ref[:, sl] = (_silu(g) * u).astype(BF16)
    return x + (0.5 * gt_ref[pl.ds(r, 1), :]) * _dot(act_ref[...], wd_ref[...])


def _cast_rows(total, n_steps):
    rows = 2 * SUBLANES
    while rows * n_steps < total or total % rows:
        rows += 2 * SUBLANES
    return rows


def _cast_next(wgn_ref, wun_ref, wdn_ref, wgo_ref, wuo_ref, wdo_ref):
    wgo_ref[...] = wgn_ref[...].astype(BF16)
    wuo_ref[...] = wun_ref[...].astype(BF16)
    wdo_ref[...] = wdn_ref[...].astype(BF16)


def _ffn_kernel(*refs, cfg, first, last, cast):
    refs = list(refs)
    t = pl.program_id(0)
    n_p = (cfg.batch * cfg.seq) // TM_FFN
    is_prompt = t < n_p
    if first:
        xp_ref, xs_ref, rt_ref, ct_ref = refs[:4]
        del refs[:4]
    else:
        x_in_ref = refs.pop(0)
    sh_ref, sc_ref, gt_ref, g_ref, wg_ref, wu_ref, wd_ref = refs[:7]
    del refs[:7]
    if last:
        fg_ref = refs.pop(0)
    if cast:
        nxt_in = refs[:3]
        del refs[:3]
    outs = refs[:2 if last else 1]
    del refs[:len(outs)]
    if cast:
        nxt_out = refs[:3]
        del refs[:3]
    if first:
        x_ref = refs.pop(0)
    hn_ref, act_ref = refs

    if first:
        @pl.when(is_prompt)
        def _():
            x_ref[...] = xp_ref[...]

        @pl.when(jnp.logical_not(is_prompt))
        def _():
            half = D_MODEL // 2
            row0 = lax.rem(jnp.maximum(t - n_p, 0), cfg.dec_seq // TM_FFN) * (TM_FFN // GRID_W)
            for rr in range(TM_FFN // GRID_W):
                rows = slice(rr * GRID_W, (rr + 1) * GRID_W)
                x_ref[rows, 0:half] = xs_ref[rows, 0:half] + rt_ref[pl.ds(row0 + rr, 1), :]
                x_ref[rows, half:D_MODEL] = xs_ref[rows, half:D_MODEL] + ct_ref[...]
        x = x_ref[...]
    else:
        x = x_in_ref[...]

    r = _mod_row(t, TM_FFN, cfg)
    xn = _ffn_body(x, r, sh_ref, sc_ref, gt_ref, g_ref, wg_ref, wu_ref, wd_ref, hn_ref, act_ref)
    if last:
        y = _rms(xn, fg_ref[...])

        @pl.when(is_prompt)
        def _():
            outs[0][...] = y

        @pl.when(jnp.logical_not(is_prompt))
        def _():
            outs[1][...] = y
    else:
        outs[0][...] = xn
    if cast:
        _cast_next(*nxt_in, *nxt_out)


def _resident(shape, index):
    return pl.BlockSpec(shape, lambda t: index, pipeline_mode=pl.Buffered(1))


def _ffn(x, mod, i, which, norm_g4, w_cur, w_next, cfg, pos_tabs=None, final_g=None):
    first, last, cast = pos_tabs is not None, final_g is not None, w_next is not None
    t_p = cfg.batch * cfg.seq
    n_p = t_p // TM_FFN
    t_tok = _n_tokens(cfg)
    n_steps = t_tok // TM_FFN
    k0 = 6 * which
    row = pl.BlockSpec((TM_FFN, D_MODEL), lambda t: (t, 0))
    row_p = pl.BlockSpec((TM_FFN, D_MODEL), lambda t: (jnp.minimum(t, n_p - 1), 0))
    row_s = pl.BlockSpec((TM_FFN, D_MODEL), lambda t: (jnp.maximum(t - n_p, 0), 0))
    half = D_MODEL // 2

    in_specs, args = [], []
    if first:
        in_specs += [row_p, row_s, pl.BlockSpec((cfg.dec_seq // GRID_W, half), lambda t: (0, 0)),
                     pl.BlockSpec((GRID_W, half), lambda t: (0, 0))]
        args += [x[0], x[1], pos_tabs[0], pos_tabs[1]]
    else:
        in_specs.append(row)
        args.append(x)
    in_specs += [_mod_spec(i, k0), _mod_spec(i, k0 + 1), _mod_spec(i, k0 + 2),
                 pl.BlockSpec((None, None, 1, D_MODEL), lambda t: (i, 2 * which, 0, 0)),
                 _resident((D_MODEL, D_FF), (0, 0)), _resident((D_MODEL, D_FF), (0, 0)),
                 _resident((D_FF, D_MODEL), (0, 0))]
    args += [mod, mod, mod, norm_g4, *w_cur]
    if last:
        in_specs.append(pl.BlockSpec((1, D_MODEL), lambda t: (0, 0)))
        args.append(final_g.reshape(1, D_MODEL))
    out_specs = [row_p, row_s] if last else [row]
    out_shape = ([jax.ShapeDtypeStruct((t_p, D_MODEL), F32), jax.ShapeDtypeStruct((t_tok - t_p, D_MODEL), F32)]
                 if last else [jax.ShapeDtypeStruct((t_tok, D_MODEL), F32)])
    if cast:
        wg_n, wu_n, wd_n, i_n, which_n = w_next
        r_in, r_dn = _cast_rows(D_MODEL, n_steps), _cast_rows(D_FF, n_steps)
        blk_in = lambda t: jnp.minimum(t, D_MODEL // r_in - 1)
        blk_dn = lambda t: jnp.minimum(t, D_FF // r_dn - 1)
        in_specs += [pl.BlockSpec((None, None, r_in, D_FF), lambda t: (i_n, which_n, blk_in(t), 0)),
                     pl.BlockSpec((None, None, r_in, D_FF), lambda t: (i_n, which_n, blk_in(t), 0)),
                     pl.BlockSpec((None, None, r_dn, D_MODEL), lambda t: (i_n, which_n, blk_dn(t), 0))]
        args += [wg_n, wu_n, wd_n]
        out_specs += [pl.BlockSpec((r_in, D_FF), lambda t: (blk_in(t), 0)),
                      pl.BlockSpec((r_in, D_FF), lambda t: (blk_in(t), 0)),
                      pl.BlockSpec((r_dn, D_MODEL), lambda t: (blk_dn(t), 0))]
        out_shape += [jax.ShapeDtypeStruct((D_MODEL, D_FF), BF16), jax.ShapeDtypeStruct((D_MODEL, D_FF), BF16),
                      jax.ShapeDtypeStruct((D_FF, D_MODEL), BF16)]
    scratch = [pltpu.VMEM((TM_FFN, D_MODEL), BF16), pltpu.VMEM((TM_FFN, D_FF), BF16)]
    if first:
        scratch.insert(0, pltpu.VMEM((TM_FFN, D_MODEL), F32))
    outs = pl.pallas_call(
        functools.partial(_ffn_kernel, cfg=cfg, first=first, last=last, cast=cast),
        grid=(n_steps,),
        in_specs=in_specs,
        out_specs=out_specs,
        out_shape=out_shape,
        scratch_shapes=scratch,
        compiler_params=_params(("arbitrary",)),
        name="ffn",
    )(*args)
    n_act = 2 if last else 1
    return outs[:n_act], (tuple(outs[n_act:]) if cast else None)


def _norm_mod(x_ref, sh_ref, sc_ref, g_ref, cfg):
    r = _mod_row(pl.program_id(0), TM_PROJ, cfg)
    hn = _rms(x_ref[...], g_ref[...]) * (1.0 + sc_ref[pl.ds(r, 1), :]) + sh_ref[pl.ds(r, 1), :]
    return hn.astype(BF16)


def _even_in_sub(s, xm_ref, xp_ref, xn_ref, sh_ref, sc_ref, g_ref, w_ref, cw_ref, cb_ref,
                 z_ref, xst_ref, bc_ref, pool_ref, dt_ref, hn_ref, ext_ref, cfg):
    tm, q = TM_MIX, SSD_CHUNK
    t = pl.program_id(0) * NSUB + s
    info = _tile_info(t, tm, cfg)
    r = _mod_row(t, tm, cfg)
    rows = slice(s * tm, (s + 1) * tm)
    before, after = _sub_halo(s, xm_ref, xp_ref, xn_ref, HALO_A)
    scale, shift, g = 1.0 + sc_ref[pl.ds(r, 1), :], sh_ref[pl.ds(r, 1), :], g_ref[...]
    hn_ref[s, 0:tm, :] = _rms(xm_ref[rows, :], g) * scale + shift
    hn_ref[s, tm:tm + HALO_A, :] = _rms(before, g) * scale + shift
    hn_ref[s, tm + HALO_A:, :] = _rms(after, g) * scale + shift
    hn = hn_ref[s].astype(BF16)
    o_xbc, o_pool, o_dt = D_INNER_A, D_INNER_A + D_XBC, D_INNER_A + D_XBC + D_POOL
    base = HALO_A - CONV_A // 2
    zero = jnp.zeros((HALO_A, CONV_COLS), F32)
    others = ([(z_ref.at[rows, c:c + CONV_COLS], c) for c in range(0, D_INNER_A, CONV_COLS)]
              + [(pool_ref.at[rows, c:c + CONV_COLS], o_pool + c) for c in range(0, D_POOL, CONV_COLS)])
    for c0 in range(0, D_XBC, CONV_COLS):
        cols = slice(c0, c0 + CONV_COLS)
        xg = _dot(hn, w_ref[:, o_xbc + c0:o_xbc + c0 + CONV_COLS])
        o_ref, o_col = others[c0 // CONV_COLS]
        o_ref[...] = _dot(hn[0:tm], w_ref[:, o_col:o_col + CONV_COLS])
        ext_ref[s, 0:HALO_A, cols] = jnp.where(info["is_start"], zero, xg[tm:tm + HALO_A])
        ext_ref[s, HALO_A:HALO_A + tm, cols] = xg[0:tm]
        ext_ref[s, HALO_A + tm:, cols] = jnp.where(info["is_end"], zero, xg[tm + HALO_A:])
        for ck in range(tm // q):
            for j in range(c0 // LANES, (c0 + CONV_COLS) // LANES):
                sl = slice(j * LANES, (j + 1) * LANES)
                r0 = base + ck * q
                acc = cw_ref[0:1, sl] * ext_ref[s, r0:r0 + q, sl]
                for k in range(1, CONV_A):
                    acc = acc + cw_ref[k:k + 1, sl] * ext_ref[s, r0 + k:r0 + k + q, sl]
                y = _silu(acc + cb_ref[:, sl])
                if j < D_INNER_A // LANES:
                    xst_ref[s * (tm // q) + ck, j * LANES:(j + 1) * LANES, :] = y.T
                else:
                    bc_ref[s * tm + ck * q:s * tm + (ck + 1) * q,
                           j * LANES - D_INNER_A:(j + 1) * LANES - D_INNER_A] = y.astype(BF16)
    dt_ref[rows, :] = _dot(hn[0:tm], w_ref[:, o_dt:o_dt + DT_PAD])


def _even_in_kernel(*refs, cfg):
    for s in range(NSUB):
        _even_in_sub(s, *refs, cfg)


def _even_in(x, mod, i, g, w, conv_w, conv_b, cfg):
    t_tok = x.shape[0]
    tm, q = NSUB * TM_MIX, SSD_CHUNK
    n_tot = D_INNER_A + D_XBC + D_POOL + DT_PAD
    small = lambda shape: pl.BlockSpec(shape, lambda t: (0,) * len(shape))
    row_block = lambda width: pl.BlockSpec((tm, width), lambda t: (t, 0))
    return pl.pallas_call(
        functools.partial(_even_in_kernel, cfg=cfg),
        grid=(t_tok // tm,),
        in_specs=_halo_specs(D_MODEL, tm, HALO_A, t_tok, lambda t: t) + [
            _mod_spec(i, 3), _mod_spec(i, 4), small((1, D_MODEL)), small((D_MODEL, n_tot)),
            small((CONV_A, D_XBC)), small((1, D_XBC)),
        ],
        out_specs=[row_block(D_INNER_A),
                   pl.BlockSpec((tm // q, D_INNER_A, q), lambda t: (t, 0, 0)),
                   row_block(2 * D_BC), row_block(D_POOL), row_block(DT_PAD)],
        out_shape=[jax.ShapeDtypeStruct((t_tok, D_INNER_A), F32),
                   jax.ShapeDtypeStruct((t_tok // q, D_INNER_A, q), F32),
                   jax.ShapeDtypeStruct((t_tok, 2 * D_BC), BF16),
                   jax.ShapeDtypeStruct((t_tok, D_POOL), F32),
                   jax.ShapeDtypeStruct((t_tok, DT_PAD), F32)],
        scratch_shapes=[pltpu.VMEM((NSUB, TM_MIX + 2 * HALO_A, D_MODEL), F32),
                        pltpu.VMEM((NSUB, TM_MIX + 2 * HALO_A, D_XBC), F32)],
        compiler_params=_params(("parallel",)),
        name="even_in",
    )(x, x, x, mod, mod, g.reshape(1, D_MODEL), w, conv_w, conv_b)


def _odd_in_kernel(x_ref, sh_ref, sc_ref, g_ref, w_ref, lg_ref, lb_ref, u_ref, v_ref, glu_ref, vg_ref, *, cfg):
    hn = _norm_mod(x_ref, sh_ref, sc_ref, g_ref, cfg)
    for c in range(0, D_C, PROJ_COLS):
        cols = slice(c, c + PROJ_COLS)
        u_ref[:, cols] = jax.nn.gelu(_dot(hn, w_ref[:, c:c + PROJ_COLS]))
        vg_ref[:, cols] = jax.nn.gelu(_dot(hn, w_ref[:, D_C + c:D_C + c + PROJ_COLS]))
        ga = _dot(hn, w_ref[:, 2 * D_C + c:2 * D_C + c + PROJ_COLS])
        gg = _dot(hn, w_ref[:, 2 * D_C + D_D + c:2 * D_C + D_D + c + PROJ_COLS])
        glu_ref[:, cols] = ga * _sigmoid(gg)
    v_ref[...] = _layer_norm(vg_ref[...], lg_ref[...], lb_ref[...]).astype(BF16)


def _odd_in(x, mod, i, g, w, ln_g, ln_b, cfg):
    t_tok = x.shape[0]
    row = lambda: pl.BlockSpec((1, D_MODEL), lambda t: (0, 0))
    return pl.pallas_call(
        functools.partial(_odd_in_kernel, cfg=cfg),
        grid=(t_tok // TM_PROJ,),
        in_specs=[
            pl.BlockSpec((TM_PROJ, D_MODEL), lambda t: (t, 0)),
            _mod_spec(i, 3), _mod_spec(i, 4),
            row(),
            pl.BlockSpec((D_MODEL, 2 * D_C + 2 * D_D), lambda t: (0, 0)),
            row(), row(),
        ],
        out_specs=[pl.BlockSpec((TM_PROJ, D_C), lambda t: (t, 0))] * 3,
        out_shape=[jax.ShapeDtypeStruct((t_tok, D_C), F32),
                   jax.ShapeDtypeStruct((t_tok, D_C), BF16),
                   jax.ShapeDtypeStruct((t_tok, D_D), F32)],
        scratch_shapes=[pltpu.VMEM((TM_PROJ, D_C), F32)],
        compiler_params=_params(("parallel",)),
        name="odd_in",
    )(x, mod, mod, g.reshape(1, D_MODEL), w, ln_g.reshape(1, D_C), ln_b.reshape(1, D_C))


def _dt_terms(dt_ref, dtb_ref, alog_ref, ck, rows):
    q = SSD_CHUNK
    dtt = dt_ref[ck * q:(ck + 1) * q, :].T[rows, :]
    dts = _softplus(dtt + dtb_ref[rows, :])
    return dts, dts * (-jnp.exp(alog_ref[rows, :]))


def _tri(lower_incl):
    ri = lax.broadcasted_iota(jnp.int32, (SSD_CHUNK, SSD_CHUNK), 0)
    ci = lax.broadcasted_iota(jnp.int32, (SSD_CHUNK, SSD_CHUNK), 1)
    return (ri >= ci) if lower_incl else (ri <= ci)


def _cumsum_lanes(v, mask):
    return jnp.dot(v, mask.astype(F32), precision=lax.Precision.HIGHEST, preferred_element_type=F32)


def _group_bc(bc_ref, ck, g):
    rows = slice(ck * SSD_CHUNK, (ck + 1) * SSD_CHUNK)
    b = bc_ref[rows, g * D_STATE:(g + 1) * D_STATE]
    c = bc_ref[rows, D_BC + g * D_STATE:D_BC + (g + 1) * D_STATE]
    return b, c


def _state_update(s_ref, xst_ref, xd_ref, bc_ref, ck, scale, cdec):
    hd = HEAD_DIM_A
    for h in range(N_HEADS_A):
        xd_ref[ck, h * hd:(h + 1) * hd, :] = (xst_ref[ck, h * hd:(h + 1) * hd, :] * scale[h:h + 1, :]).astype(BF16)
    gw = HEADS_PER_GROUP * hd
    for g in range(N_GROUPS_A):
        b, _ = _group_bc(bc_ref, ck, g)
        upd = _dot(xd_ref[ck, g * gw:(g + 1) * gw, :], b)
        for e in range(HEADS_PER_GROUP):
            h = g * HEADS_PER_GROUP + e
            s_ref[h * hd:(h + 1) * hd, :] = (s_ref[h * hd:(h + 1) * hd, :] * cdec[h:h + 1, :]
                                             + upd[e * hd:(e + 1) * hd, :])


def _ssd_fwd_chunk(ck, xst_ref, bc_ref, dt_ref, dtb_ref, alog_ref, dsk_ref, ypt_ref, xd_ref, s_ref):
    q, hd, nh = SSD_CHUNK, HEAD_DIM_A, N_HEADS_A
    dts, dta = _dt_terms(dt_ref, dtb_ref, alog_ref, ck, slice(0, 2 * nh))
    low, upp = _tri(True), _tri(False)
    acs_f = _cumsum_lanes(dta[0:nh], upp)
    rcs_b = _cumsum_lanes(dta[nh:2 * nh], low)
    src = jnp.concatenate([acs_f - jnp.log(dts[0:nh]), rcs_b - jnp.log(dts[nh:2 * nh]),
                           jnp.zeros((LANES - 2 * nh, q), F32)], axis=0).T
    e_acs = jnp.exp(acs_f)
    neg_inf = jnp.float32(-jnp.inf)

    gw = HEADS_PER_GROUP * hd
    for g in range(N_GROUPS_A):
        b, c = _group_bc(bc_ref, ck, g)
        sct = _dot_nt(b, c)
        yoff = _dot_nt(s_ref[g * gw:(g + 1) * gw, :].astype(BF16), c)
        for e in range(HEADS_PER_GROUP):
            h = g * HEADS_PER_GROUP + e
            rows = slice(h * hd, (h + 1) * hd)
            seg_f = acs_f[h:h + 1, :] - src[:, h:h + 1]
            seg_b = rcs_b[h:h + 1, :] - src[:, nh + h:nh + h + 1]
            wt = sct * (jnp.exp(jnp.where(upp, seg_f, neg_inf)) + jnp.exp(jnp.where(low, seg_b, neg_inf)))
            xh = xst_ref[ck, rows, :]
            yd = _dot(xh.astype(BF16), wt.astype(BF16))
            ypt_ref[ck, rows, :] = yd + yoff[e * hd:(e + 1) * hd, :] * e_acs[h:h + 1, :] + dsk_ref[rows, :] * xh

    tot = acs_f[:, q - 1:q]
    scale = dts[0:nh] * jnp.exp(tot - acs_f)
    cdec = jnp.exp(jnp.broadcast_to(tot, (nh, D_STATE)))
    _state_update(s_ref, xst_ref, xd_ref, bc_ref, ck, scale, cdec)


def _scan_init(is_prompt, s0_ref):
    return jnp.where(is_prompt, jnp.zeros(s0_ref.shape, F32), s0_ref[...])


def _ssd_fwd_kernel(xst_ref, bc_ref, dt_ref, dtb_ref, alog_ref, dsk_ref, s0_ref, ypt_ref, sfin_ref, xd_ref, snap_ref,
                    s_ref, *, cfg):
    t = pl.program_id(0)
    cps = TM_MIX // SSD_CHUNK

    @pl.when(t == 0)
    def _():
        s_ref[...] = jnp.zeros_like(s_ref)

    for s in range(NSUB):
        info = _tile_info(t * NSUB + s, TM_MIX, cfg)
        s_ref[...] = jnp.where(info["is_start"], _scan_init(info["is_prompt"], s0_ref), s_ref[...])
        for ck in range(s * cps, (s + 1) * cps):
            _ssd_fwd_chunk(ck, xst_ref, bc_ref, dt_ref, dtb_ref, alog_ref, dsk_ref, ypt_ref, xd_ref, s_ref)
        snap_ref[s] = s_ref[...]

    @pl.when(t * NSUB < cfg.batch * (cfg.seq // TM_MIX))
    def _():
        sfin_ref[...] = snap_ref[...]


def _ssd_bwd_chunk(ck, dt_ref, z_ref, ypt_ref, xst_ref, bc_ref, dtb_ref, alog_ref, yt_ref, y_ref, xd_ref, s_ref):
    q, hd, nh = SSD_CHUNK, HEAD_DIM_A, N_HEADS_A
    dts, dta = _dt_terms(dt_ref, dtb_ref, alog_ref, ck, slice(nh, 2 * nh))
    rcs_b = _cumsum_lanes(dta, _tri(True))
    e_rcs = jnp.exp(rcs_b)

    gw = HEADS_PER_GROUP * hd
    for g in range(N_GROUPS_A):
        _, c = _group_bc(bc_ref, ck, g)
        yoff = _dot_nt(s_ref[g * gw:(g + 1) * gw, :].astype(BF16), c)
        for e in range(HEADS_PER_GROUP):
            h = g * HEADS_PER_GROUP + e
            yt_ref[ck, h * hd:(h + 1) * hd, :] = (ypt_ref[ck, h * hd:(h + 1) * hd, :]
                                                  + yoff[e * hd:(e + 1) * hd, :] * e_rcs[h:h + 1, :])

    rows = slice(ck * q, (ck + 1) * q)
    for j in range(D_INNER_A // LANES):
        sl = slice(j * LANES, (j + 1) * LANES)
        y_ref[rows, sl] = yt_ref[ck, j * LANES:(j + 1) * LANES, :].T * _silu(z_ref[rows, sl])

    tot = rcs_b[:, 0:1]
    scale = dts * jnp.exp(tot - rcs_b)
    cdec = jnp.exp(jnp.broadcast_to(tot, (nh, D_STATE)))
    _state_update(s_ref, xst_ref, xd_ref, bc_ref, ck, scale, cdec)


def _ssd_bwd_kernel(dt_ref, z_ref, ypt_ref, xst_ref, bc_ref, dtb_ref, alog_ref, ng_ref, s0_ref, sf_ref, *rest,
                    cfg, n_steps, layer):
    if layer == 0:
        ya_ref, so_ref, xd_ref, yt_ref, y_ref, snap_ref, s_ref = rest
    else:
        sprev_ref, ya_ref, so_ref, xd_ref, yt_ref, y_ref, snap_ref, s_ref = rest
    t = n_steps - 1 - pl.program_id(0)
    cps = TM_MIX // SSD_CHUNK

    @pl.when(pl.program_id(0) == 0)
    def _():
        s_ref[...] = jnp.zeros_like(s_ref)

    for s in reversed(range(NSUB)):
        info = _tile_info(t * NSUB + s, TM_MIX, cfg)
        s_ref[...] = jnp.where(info["is_end"], _scan_init(info["is_prompt"], s0_ref), s_ref[...])
        for ck in reversed(range(s * cps, (s + 1) * cps)):
            _ssd_bwd_chunk(ck, dt_ref, z_ref, ypt_ref, xst_ref, bc_ref, dtb_ref, alog_ref, yt_ref, y_ref, xd_ref, s_ref)
        snap_ref[s] = s_ref[...]
    ya_ref[...] = _rms(y_ref[...], ng_ref[...]).astype(BF16)

    @pl.when(t * NSUB < cfg.batch * (cfg.seq // TM_MIX))
    def _():
        if layer == 0:
            so_ref[:, 0] = sf_ref[...]
            so_ref[:, 1] = snap_ref[...]
        else:
            so_ref[:, 0] = sprev_ref[...]
            so_ref[:, 1, 0] = sf_ref[...]
            so_ref[:, 1, 1] = snap_ref[...]


def _halo_specs(width, tile, halo, n_rows, chunk_of):
    per = tile // halo
    last = n_rows // halo - 1
    return [
        pl.BlockSpec((tile, width), lambda i: (chunk_of(i), 0)),
        pl.BlockSpec((halo, width), lambda i: (jnp.maximum(chunk_of(i) * per - 1, 0), 0)),
        pl.BlockSpec((halo, width), lambda i: (jnp.minimum((chunk_of(i) + 1) * per, last), 0)),
    ]


def _ssd(xst, bc, dt, z, dtb, alog, dsk_t, norm_g, s0, s_prev, layer, cfg):
    t_tok = bc.shape[0]
    q, tm = SSD_CHUNK, NSUB * TM_MIX
    cpt = tm // q
    n_steps = t_tok // tm
    hp = N_HEADS_A * HEAD_DIM_A
    nst = cfg.dec_seq // tm
    n_pt = (cfg.batch * cfg.seq) // tm

    def s_seq(t):
        return jnp.clip(lax.div(jnp.maximum(t - n_pt, 0), nst), 0, cfg.dec_batch - 1)

    def p_blk(t):
        return jnp.minimum(t, n_pt - 1)

    small = lambda shape: pl.BlockSpec(shape, lambda i: (0,) * len(shape))
    head_rows = small((2 * N_HEADS_A, q))
    state_scratch = [pltpu.VMEM((NSUB, hp, D_STATE), F32), pltpu.VMEM((hp, D_STATE), F32)]

    ypt, sf = pl.pallas_call(
        functools.partial(_ssd_fwd_kernel, cfg=cfg),
        grid=(n_steps,),
        in_specs=[
            pl.BlockSpec((cpt, hp, q), lambda i: (i, 0, 0)),
            pl.BlockSpec((tm, 2 * D_BC), lambda i: (i, 0)),
            pl.BlockSpec((tm, DT_PAD), lambda i: (i, 0)),
            head_rows, head_rows, small((hp, q)),
            pl.BlockSpec((None, None, hp, D_STATE), lambda i: (s_seq(i), 0, 0, 0)),
        ],
        out_specs=[pl.BlockSpec((cpt, hp, q), lambda i: (i, 0, 0)),
                   pl.BlockSpec((NSUB, hp, D_STATE), lambda i: (p_blk(i), 0, 0))],
        out_shape=[jax.ShapeDtypeStruct((t_tok // q, hp, q), F32),
                   jax.ShapeDtypeStruct((cfg.batch, hp, D_STATE), F32)],
        scratch_shapes=[pltpu.VMEM((cpt, hp, q), BF16)] + state_scratch,
        compiler_params=_params(("arbitrary",)),
        name="ssd_fwd",
    )(xst, bc, dt, dtb, alog, dsk_t, s0)

    rev = lambda i: n_steps - 1 - i
    in_specs = [
        pl.BlockSpec((tm, DT_PAD), lambda i: (rev(i), 0)),
        pl.BlockSpec((tm, D_INNER_A), lambda i: (rev(i), 0)),
        pl.BlockSpec((cpt, hp, q), lambda i: (rev(i), 0, 0)),
        pl.BlockSpec((cpt, hp, q), lambda i: (rev(i), 0, 0)),
        pl.BlockSpec((tm, 2 * D_BC), lambda i: (rev(i), 0)),
        head_rows, head_rows, small((1, D_INNER_A)),
        pl.BlockSpec((None, None, hp, D_STATE), lambda i: (s_seq(rev(i)), 1, 0, 0)),
        pl.BlockSpec((NSUB, hp, D_STATE), lambda i: (p_blk(rev(i)), 0, 0)),
    ]
    args = [dt, z, ypt, xst, bc, dtb, alog, norm_g, s0, sf]
    if layer == 0:
        so_block, so_shape = (NSUB, 2, hp, D_STATE), (cfg.batch, 2, hp, D_STATE)
        so_index = lambda i: (p_blk(rev(i)), 0, 0, 0)
    else:
        in_specs.append(pl.BlockSpec((NSUB, 2, hp, D_STATE), lambda i: (p_blk(rev(i)), 0, 0, 0)))
        args.append(s_prev)
        so_block, so_shape = (NSUB, 2, 2, hp, D_STATE), (cfg.batch, 2, 2, hp, D_STATE)
        so_index = lambda i: (p_blk(rev(i)), 0, 0, 0, 0)
    ya, so = pl.pallas_call(
        functools.partial(_ssd_bwd_kernel, cfg=cfg, n_steps=n_steps, layer=layer),
        grid=(n_steps,),
        in_specs=in_specs,
        out_specs=[pl.BlockSpec((tm, D_INNER_A), lambda i: (rev(i), 0)), pl.BlockSpec(so_block, so_index)],
        out_shape=[jax.ShapeDtypeStruct((t_tok, D_INNER_A), BF16), jax.ShapeDtypeStruct(so_shape, F32)],
        scratch_shapes=[pltpu.VMEM((cpt, hp, q), BF16), pltpu.VMEM((cpt, hp, q), F32),
                        pltpu.VMEM((tm, D_INNER_A), F32)] + state_scratch,
        compiler_params=_params(("arbitrary",)),
        name="ssd_bwd",
    )(*args)
    return ya, so


def _sub_halo(s, main_ref, prev_ref, next_ref, halo):
    tm = TM_MIX
    before = prev_ref[...] if s == 0 else main_ref[s * tm - halo:s * tm, :]
    after = next_ref[...] if s == NSUB - 1 else main_ref[(s + 1) * tm:(s + 1) * tm + halo, :]
    return before, after


def _even_out_sub(s, ya_ref, pm_ref, pp_ref, pn_ref, x_ref, gt_ref, pw_ref, ps_ref, woa_ref, wop_ref,
                  o_ref, ext_ref, yp_ref, cfg):
    tm = TM_MIX
    t = pl.program_id(0) * NSUB + s
    info = _tile_info(t, tm, cfg)
    r = _mod_row(t, tm, cfg)
    rows = slice(s * tm, (s + 1) * tm)
    before, after = _sub_halo(s, pm_ref, pp_ref, pn_ref, HALO_A)
    zero = jnp.zeros((HALO_A, D_POOL), F32)
    ext_ref[s, 0:HALO_A, :] = jnp.where(info["is_start"], zero, before)
    ext_ref[s, HALO_A:HALO_A + tm, :] = pm_ref[rows, :]
    ext_ref[s, HALO_A + tm:, :] = jnp.where(info["is_end"], zero, after)
    pos = info["pos"] * tm + lax.broadcasted_iota(jnp.int32, (tm, POOL_GROUP_DIM), 0)
    oc = D_MODEL // len(POOL_WINDOWS)
    for gi, win in enumerate(POOL_WINDOWS):
        o_ref[rows, gi * oc:(gi + 1) * oc] = _dot(ya_ref[rows, :], woa_ref[:, gi * oc:(gi + 1) * oc])
        sl = slice(gi * POOL_GROUP_DIM, (gi + 1) * POOL_GROUP_DIM)
        lo = HALO_A - win // 2
        acc = ext_ref[s, lo:lo + tm, sl]
        for j in range(1, win):
            acc = acc + ext_ref[s, lo + j:lo + j + tm, sl]
        cnt = (jnp.minimum(pos - win // 2 + win, info["seq_len"]) - jnp.maximum(pos - win // 2, 0)).astype(F32)
        pooled = acc / cnt - ext_ref[s, HALO_A:HALO_A + tm, sl]
        yp_ref[rows, sl] = (_dot(pooled.astype(BF16), pw_ref[gi]) * ps_ref[:, sl]).astype(BF16)
    out = o_ref[rows, :] + _dot(yp_ref[rows, :], wop_ref[...])
    o_ref[rows, :] = x_ref[rows, :] + gt_ref[pl.ds(r, 1), :] * out


def _even_out_kernel(*refs, cfg):
    for s in range(NSUB):
        _even_out_sub(s, *refs, cfg)


def _even_out(ya, pool_in, x, mod, i, pool_w, pool_scale, wo_a, wo_p, cfg):
    t_tok = x.shape[0]
    tm = NSUB * TM_MIX
    small = lambda shape: pl.BlockSpec(shape, lambda t: (0,) * len(shape))
    return pl.pallas_call(
        functools.partial(_even_out_kernel, cfg=cfg),
        grid=(t_tok // tm,),
        in_specs=[pl.BlockSpec((tm, D_INNER_A), lambda t: (t, 0))]
        + _halo_specs(D_POOL, tm, HALO_A, t_tok, lambda t: t) + [
            pl.BlockSpec((tm, D_MODEL), lambda t: (t, 0)),
            _mod_spec(i, 5),
            small((len(POOL_WINDOWS), POOL_GROUP_DIM, POOL_GROUP_DIM)), small((1, D_POOL)),
            small((D_INNER_A, D_MODEL)), small((D_POOL, D_MODEL)),
        ],
        out_specs=pl.BlockSpec((tm, D_MODEL), lambda t: (t, 0)),
        out_shape=jax.ShapeDtypeStruct((t_tok, D_MODEL), F32),
        scratch_shapes=[pltpu.VMEM((NSUB, TM_MIX + 2 * HALO_A, D_POOL), F32), pltpu.VMEM((tm, D_POOL), BF16)],
        compiler_params=_params(("parallel",)),
        name="even_out",
    )(ya, pool_in, pool_in, pool_in, x, mod, pool_w, pool_scale, wo_a, wo_p)


def _odd_out_sub(s, u_ref, v_ref, gm_ref, gp_ref, gn_ref, x_ref, gt_ref, spw_ref, spb_ref, dww_ref, dwb_ref,
                 cg_ref, cb_ref, woc_ref, wod_ref, o_ref, ext_ref, sh_ref, yc_ref, yd_ref, cfg):
    tm = TM_MIX
    t = pl.program_id(0) * NSUB + s
    info = _tile_info(t, tm, cfg)
    r = _mod_row(t, tm, cfg)
    row0 = s * tm
    tile_rows = slice(row0, row0 + tm)
    hc = D_C // N_HEADS_C
    for ck in range(tm // MLP_CHUNK):
        rows = slice(row0 + ck * MLP_CHUNK, row0 + (ck + 1) * MLP_CHUNK)
        for h in range(N_HEADS_C):
            sl = slice(h * hc, (h + 1) * hc)
            sv = _dot(spw_ref[h], v_ref[rows, sl]) + spb_ref[:, sl]
            yc_ref[rows, sl] = (u_ref[rows, sl] * sv).astype(BF16)
    before, after = _sub_halo(s, gm_ref, gp_ref, gn_ref, HALO_D)
    zero = jnp.zeros((HALO_D, D_D), F32)
    ext_ref[s, 0:HALO_D, :] = jnp.where(info["is_start"], zero, before)
    ext_ref[s, HALO_D:HALO_D + tm, :] = gm_ref[tile_rows, :]
    ext_ref[s, HALO_D + tm:, :] = jnp.where(info["is_end"], zero, after)
    span = tm + 2 * HALO_D - SUBLANES
    base = HALO_D - CONV_D // 2
    rb = MLP_CHUNK
    for j in range(D_D // LANES):
        sl = slice(j * LANES, (j + 1) * LANES)
        if (j * LANES) % PROJ_COLS == 0:
            oc = slice(j * LANES, j * LANES + PROJ_COLS)
            o_ref[tile_rows, oc] = _dot(yc_ref[tile_rows, :], woc_ref[:, oc])
        for sh in range(1, SUBLANES):
            sh_ref[sh - 1, :, sl] = ext_ref[s, sh:sh + span, sl]
        for rr in range(tm // rb):
            acc = None
            for k in range(CONV_D):
                q8, sh = divmod(base + k, SUBLANES)
                r0 = q8 * SUBLANES + rr * rb
                tap = ext_ref[s, r0:r0 + rb, sl] if sh == 0 else sh_ref[sh - 1, r0:r0 + rb, sl]
                term = dww_ref[k:k + 1, sl] * tap
                acc = term if acc is None else acc + term
            yd_ref[row0 + rr * rb:row0 + (rr + 1) * rb, sl] = acc + dwb_ref[:, sl]
    ydn = _silu(_layer_norm(yd_ref[tile_rows, :], cg_ref[...], cb_ref[...])).astype(BF16)
    out = o_ref[tile_rows, :] + _dot(ydn, wod_ref[...])
    o_ref[tile_rows, :] = x_ref[tile_rows, :] + gt_ref[pl.ds(r, 1), :] * out


def _odd_out_kernel(*refs, cfg):
    for s in range(NSUB):
        _odd_out_sub(s, *refs, cfg)


def _odd_out(u, v, glu, x, mod, i, sp_w, sp_b_full, dw_w, dw_b, cn_g, cn_b, wo_c, wo_d, cfg):
    t_tok = x.shape[0]
    tm = NSUB * TM_MIX
    small = lambda shape: pl.BlockSpec(shape, lambda t: (0,) * len(shape))
    return pl.pallas_call(
        functools.partial(_odd_out_kernel, cfg=cfg),
        grid=(t_tok // tm,),
        in_specs=[pl.BlockSpec((tm, D_C), lambda t: (t, 0)), pl.BlockSpec((tm, D_C), lambda t: (t, 0))]
        + _halo_specs(D_D, tm, HALO_D, t_tok, lambda t: t) + [
            pl.BlockSpec((tm, D_MODEL), lambda t: (t, 0)),
            _mod_spec(i, 5),
            small((N_HEADS_C, MLP_CHUNK, MLP_CHUNK)), small((MLP_CHUNK, D_C)),
            small((CONV_D, D_D)), small((1, D_D)), small((1, D_D)), small((1, D_D)),
            small((D_C, D_MODEL)), small((D_D, D_MODEL)),
        ],
        out_specs=pl.BlockSpec((tm, D_MODEL), lambda t: (t, 0)),
        out_shape=jax.ShapeDtypeStruct((t_tok, D_MODEL), F32),
        scratch_shapes=[pltpu.VMEM((NSUB, TM_MIX + 2 * HALO_D, D_D), F32),
                        pltpu.VMEM((SUBLANES - 1, TM_MIX + 2 * HALO_D - SUBLANES, D_D), F32),
                        pltpu.VMEM((tm, D_C), BF16), pltpu.VMEM((tm, D_D), F32)],
        compiler_params=_params(("parallel",)),
        name="odd_out",
    )(u, v, glu, glu, glu, x, mod, sp_w, sp_b_full, dw_w, dw_b, cn_g, cn_b, wo_c, wo_d)


def _position_tables(dec_seq):
    quarter = D_MODEL // 4
    freqs = jnp.exp(-math.log(POS_BASE) * jnp.arange(quarter, dtype=F32) / quarter)
    ang_r = jnp.arange(dec_seq // GRID_W, dtype=F32)[:, None] * freqs
    ang_c = jnp.arange(GRID_W, dtype=F32)[:, None] * freqs
    return (jnp.concatenate([jnp.sin(ang_r), jnp.cos(ang_r)], axis=-1),
            jnp.concatenate([jnp.sin(ang_c), jnp.cos(ang_c)], axis=-1))


def _forward(cfg, x_prompt, x_sample, c, state_ssd, c_ctx, w_mod, b_mod, norm_g, ffn_w_gate, ffn_w_up,
             ffn_w_down, ev_w_in, ev_conv_w, ev_conv_b, ev_dt_bias, ev_a_log, ev_d_skip, ev_ssd_norm_g,
             ev_pool_w, ev_pool_scale, ev_w_out, od_w_in, od_v_ln_g, od_v_ln_b, od_sp_w, od_sp_b, od_dw_w,
             od_dw_b, od_cn_g, od_cn_b, od_w_out, final_norm_g):
    assert DEPTH == 4 and cfg.seq % TM_MIX == 0 and cfg.dec_seq % TM_FFN == 0 and (cfg.batch * cfg.seq) % TM_FFN == 0
    t_p = cfg.batch * cfg.seq
    hp = N_HEADS_A * HEAD_DIM_A
    row_tab, col_tab = _position_tables(cfg.dec_seq)

    cond8 = jnp.concatenate([c_ctx[None, :], c, jnp.zeros((SUBLANES - 1 - cfg.dec_batch, D_MODEL), F32)], axis=0)
    mod = _modulation(cond8, w_mod, b_mod)

    w_ffn = tuple(w[0, 0].astype(BF16) for w in (ffn_w_gate, ffn_w_up, ffn_w_down))
    f32_ffn = (ffn_w_gate, ffn_w_up, ffn_w_down)
    norm_g4 = norm_g.reshape(DEPTH, 3, 1, D_MODEL)
    states = None
    for i in range(DEPTH):
        j = i // 2
        if i == 0:
            (x,), w_ffn = _ffn((x_prompt.reshape(t_p, D_MODEL), x_sample.reshape(-1, D_MODEL)), mod, i, 0, norm_g4,
                               w_ffn, f32_ffn + (i, 1), cfg, pos_tabs=(row_tab, col_tab))
        else:
            (x,), w_ffn = _ffn(x, mod, i, 0, norm_g4, w_ffn, f32_ffn + (i, 1), cfg)
        if i % 2 == 0:
            w = ev_w_in[j]
            o_dt = D_INNER_A + D_XBC
            o_pool = o_dt + 2 * N_HEADS_A
            w_all = jnp.concatenate(
                [w[:, :o_dt], w[:, o_pool:], w[:, o_dt:o_pool],
                 jnp.zeros((D_MODEL, DT_PAD - 2 * N_HEADS_A), F32)], axis=1).astype(BF16)
            z, xst, bc, pool_in, dt = _even_in(x, mod, i, norm_g[i, 1], w_all, ev_conv_w[j],
                                               ev_conv_b[j].reshape(1, D_XBC), cfg)
            lane_b = lambda v: jnp.broadcast_to(v.reshape(2 * N_HEADS_A, 1), (2 * N_HEADS_A, SSD_CHUNK))
            dsk_t = jnp.broadcast_to(ev_d_skip[j][:, None, None], (N_HEADS_A, HEAD_DIM_A, SSD_CHUNK)).reshape(hp, SSD_CHUNK)
            s0 = state_ssd[:, j].reshape(cfg.dec_batch, 2, hp, D_STATE)
            ya, states = _ssd(xst, bc, dt, z, lane_b(ev_dt_bias[j]), lane_b(ev_a_log[j]), dsk_t,
                              ev_ssd_norm_g[j].reshape(1, D_INNER_A), s0, states, j, cfg)
            x = _even_out(ya, pool_in, x, mod, i, ev_pool_w[j].astype(BF16), ev_pool_scale[j].reshape(1, D_POOL),
                          ev_w_out[j, :D_INNER_A].astype(BF16), ev_w_out[j, D_INNER_A:].astype(BF16), cfg)
        else:
            u, v, glu = _odd_in(x, mod, i, norm_g[i, 1], od_w_in[j].astype(BF16), od_v_ln_g[j], od_v_ln_b[j], cfg)
            spb = jnp.broadcast_to(od_sp_b[j].T[:, :, None], (MLP_CHUNK, N_HEADS_C, D_C // N_HEADS_C))
            x = _odd_out(u, v, glu, x, mod, i, od_sp_w[j].astype(BF16), spb.reshape(MLP_CHUNK, D_C), od_dw_w[j],
                         od_dw_b[j].reshape(1, D_D), od_cn_g[j].reshape(1, D_D), od_cn_b[j].reshape(1, D_D),
                         od_w_out[j, :D_C].astype(BF16), od_w_out[j, D_C:].astype(BF16), cfg)
        if i < DEPTH - 1:
            (x,), w_ffn = _ffn(x, mod, i, 1, norm_g4, w_ffn, f32_ffn + (i + 1, 0), cfg)
        else:
            (y_p, y_s), _ = _ffn(x, mod, i, 1, norm_g4, w_ffn, None, cfg, final_g=final_norm_g)

    return (y_p.reshape(cfg.batch, cfg.seq, D_MODEL), y_s.reshape(cfg.dec_batch, cfg.dec_seq, D_MODEL),
            states.reshape(cfg.batch, N_EVEN, 2, N_HEADS_A, HEAD_DIM_A, D_STATE))


def kernel(x_prompt, x_sample, c, state_ssd, c_ctx, w_mod, b_mod, norm_g, ffn_w_gate, ffn_w_up, ffn_w_down, ev_w_in, ev_conv_w, ev_conv_b, ev_dt_bias, ev_a_log, ev_d_skip, ev_ssd_norm_g, ev_pool_w, ev_pool_scale, ev_w_out, od_w_in, od_v_ln_g, od_v_ln_b, od_sp_w, od_sp_b, od_dw_w, od_dw_b, od_cn_g, od_cn_b, od_w_out, final_norm_g):
    cfg = Cfg(x_prompt.shape[0], x_prompt.shape[1], x_sample.shape[0], x_sample.shape[1])
    return _forward(cfg, x_prompt, x_sample, c, state_ssd, c_ctx, w_mod, b_mod, norm_g, ffn_w_gate, ffn_w_up,
                    ffn_w_down, ev_w_in, ev_conv_w, ev_conv_b, ev_dt_bias, ev_a_log, ev_d_skip, ev_ssd_norm_g,
                    ev_pool_w, ev_pool_scale, ev_w_out, od_w_in, od_v_ln_g, od_v_ln_b, od_sp_w, od_sp_b, od_dw_w,
                    od_dw_b, od_cn_g, od_cn_b, od_w_out, final_norm_g)
```

```python
import collections
import functools
import math

import jax
import jax.numpy as jnp
from jax import lax
from jax.experimental import pallas as pl
from jax.experimental.pallas import tpu as pltpu

F32 = jnp.float32
BF16 = jnp.bfloat16

D_MODEL = 1024
DEPTH = 4
GRID_W = 64
POS_BASE = 10000.0
EPS = 1e-6
N_MOD = 9
D_FF = 2816
N_HEADS_A = 16
HEAD_DIM_A = 64
D_INNER_A = N_HEADS_A * HEAD_DIM_A
N_GROUPS_A = 2
HEADS_PER_GROUP = N_HEADS_A // N_GROUPS_A
D_STATE = 128
CONV_A = 5
SSD_CHUNK = 128
D_BC = N_GROUPS_A * D_STATE
D_XBC = D_INNER_A + 2 * D_BC
POOL_WINDOWS = (2, 4, 8, 16)
POOL_GROUP_DIM = 128
D_POOL = len(POOL_WINDOWS) * POOL_GROUP_DIM
N_HEADS_C = 8
MLP_CHUNK = 128
D_C = 1024
D_D = 1024
CONV_D = 31
N_EVEN = (DEPTH + 1) // 2

LANES = 128
SUBLANES = 8
DT_PAD = LANES
HALO_A = SUBLANES
HALO_D = 2 * SUBLANES
VMEM_LIMIT = 56 * 1024 * 1024

TM_FFN = 1024
TM_FFN_FIRST = 512
TF_FFN = 256
TM_PROJ = 512
TM_MIX = 256
CONV_COLS = 256
PROJ_COLS = 256
NSUB = 2
Cfg = collections.namedtuple("Cfg", "batch seq dec_batch dec_seq")


def _n_tokens(cfg):
    return cfg.batch * cfg.seq + cfg.dec_batch * cfg.dec_seq


def _tile_info(t, tm, cfg):
    npt = cfg.seq // tm
    nst = cfg.dec_seq // tm
    n_p = cfg.batch * npt
    is_prompt = t < n_p
    ts = jnp.maximum(t - n_p, 0)
    pos = jnp.where(is_prompt, lax.rem(t, npt), lax.rem(ts, nst))
    last = jnp.where(is_prompt, npt - 1, nst - 1)
    info = dict(
        is_prompt=is_prompt,
        is_start=pos == 0,
        is_end=pos == last,
        pos=pos,
        seq_len=jnp.where(is_prompt, cfg.seq, cfg.dec_seq),
    )
    return info


def _mod_row(t, tm, cfg):
    n_p = (cfg.batch * cfg.seq) // tm
    per = cfg.dec_seq // tm
    return jnp.where(t < n_p, 0, 1 + lax.div(jnp.maximum(t - n_p, 0), per))


def _sigmoid(x):
    return 0.5 + 0.5 * jnp.tanh(0.5 * x)


def _silu(x):
    h = 0.5 * x
    return h + h * jnp.tanh(h)


def _softplus(x):
    return jnp.maximum(x, 0.0) + jnp.log1p(jnp.exp(-jnp.abs(x)))


def _rms(x, g):
    return x * lax.rsqrt(jnp.mean(x * x, axis=-1, keepdims=True) + EPS) * g


def _layer_norm(x, g, b):
    xc = x - jnp.mean(x, axis=-1, keepdims=True)
    y = xc * lax.rsqrt(jnp.mean(xc * xc, axis=-1, keepdims=True) + EPS)
    return y * g + b


def _dot(a, b):
    return jnp.dot(a, b, preferred_element_type=F32)


def _dot_nt(a, b):
    return lax.dot_general(a, b, (((1,), (1,)), ((), ())), preferred_element_type=F32)


def _params(sem):
    return pltpu.CompilerParams(dimension_semantics=sem, vmem_limit_bytes=VMEM_LIMIT)


def _mod_kernel(c_ref, w_ref, b_ref, o_ref):
    c = c_ref[...]
    sc = _silu(c).astype(BF16)
    o_ref[...] = _dot(sc, w_ref[...].astype(BF16)) + b_ref[...]


def _modulation(cond8, w_mod, b_mod):
    b4 = b_mod.reshape(DEPTH, N_MOD, 1, D_MODEL)
    return pl.pallas_call(
        _mod_kernel,
        grid=(DEPTH, N_MOD),
        in_specs=[
            pl.BlockSpec((SUBLANES, D_MODEL), lambda i, k: (0, 0)),
            pl.BlockSpec((None, D_MODEL, D_MODEL), lambda i, k: (i, 0, k)),
            pl.BlockSpec((None, None, 1, D_MODEL), lambda i, k: (i, k, 0, 0)),
        ],
        out_specs=pl.BlockSpec((None, None, SUBLANES, D_MODEL), lambda i, k: (i, k, 0, 0)),
        out_shape=jax.ShapeDtypeStruct((DEPTH, N_MOD, SUBLANES, D_MODEL), F32),
        compiler_params=_params(("arbitrary", "arbitrary")),
        name="modulation",
    )(cond8, w_mod, b4)


def _mod_spec(i, k):
    return pl.BlockSpec((None, None, SUBLANES, D_MODEL), lambda t: (i, k, 0, 0))


def _layer_block(shape, j, row=0):
    return pl.BlockSpec((None,) + tuple(shape), lambda t: (j, row) + (0,) * (len(shape) - 1))


def _norm_g_spec(i, k):
    return pl.BlockSpec((None, None, 1, D_MODEL), lambda t: (i, k, 0, 0))


def _ffn_body(x, r, sh_ref, sc_ref, gt_ref, g_ref, wg_ref, wu_ref, wd_ref, hn_ref, act_ref):
    hn = _rms(x, g_ref[...]) * (1.0 + sc_ref[pl.ds(r, 1), :]) + sh_ref[pl.ds(r, 1), :]
    hn_ref[...] = hn.astype(BF16)
    for f in range(D_FF // TF_FFN):
        sl = slice(f * TF_FFN, (f + 1) * TF_FFN)
        g = _dot(hn_ref[...], wg_ref[:, sl])
        u = _dot(hn_ref[...], wu_ref[:, sl])
        act_ref[:, sl] = (_silu(g) * u).astype(BF16)
    return x + (0.5 * gt_ref[pl.ds(r, 1), :]) * _dot(act_ref[...], wd_ref[...])


def _cast_rows(total, n_steps):
    rows = 2 * SUBLANES
    while rows * n_steps < total or total % rows:
        rows += 2 * SUBLANES
    return rows


def _cast_next(wgn_ref, wun_ref, wdn_ref, wgo_ref, wuo_ref, wdo_ref):
    wgo_ref[...] = wgn_ref[...].astype(BF16)
    wuo_ref[...] = wun_ref[...].astype(BF16)
    wdo_ref[...] = wdn_ref[...].astype(BF16)


def _ffn_kernel(*refs, cfg, tm, first, last, cast):
    refs = list(refs)
    t = pl.program_id(0)
    n_p = (cfg.batch * cfg.seq) // tm
    is_prompt = t < n_p
    if first:
        xp_ref, xs_ref, rt_ref, ct_ref = refs[:4]
        del refs[:4]
    else:
        x_in_ref = refs.pop(0)
    sh_ref, sc_ref, gt_ref, g_ref, wg_ref, wu_ref, wd_ref = refs[:7]
    del refs[:7]
    if last:
        fg_ref = refs.pop(0)
    if cast:
        nxt_in = refs[:3]
        del refs[:3]
    outs = refs[:2 if last else 1]
    del refs[:len(outs)]
    if cast:
        nxt_out = refs[:3]
        del refs[:3]
    if first:
        x_ref = refs.pop(0)
    hn_ref, act_ref = refs

    if first:
        @pl.when(is_prompt)
        def _():
            x_ref[...] = xp_ref[...]

        @pl.when(jnp.logical_not(is_prompt))
        def _():
            half = D_MODEL // 2
            row0 = lax.rem(jnp.maximum(t - n_p, 0), cfg.dec_seq // tm) * (tm // GRID_W)
            for rr in range(tm // GRID_W):
                rows = slice(rr * GRID_W, (rr + 1) * GRID_W)
                x_ref[rows, 0:half] = xs_ref[rows, 0:half] + rt_ref[pl.ds(row0 + rr, 1), :]
                x_ref[rows, half:D_MODEL] = xs_ref[rows, half:D_MODEL] + ct_ref[...]
        x = x_ref[...]
    else:
        x = x_in_ref[...]

    r = _mod_row(t, tm, cfg)
    xn = _ffn_body(x, r, sh_ref, sc_ref, gt_ref, g_ref, wg_ref, wu_ref, wd_ref, hn_ref, act_ref)
    if last:
        y = _rms(xn, fg_ref[...])

        @pl.when(is_prompt)
        def _():
            outs[0][...] = y

        @pl.when(jnp.logical_not(is_prompt))
        def _():
            outs[1][...] = y
    else:
        outs[0][...] = xn
    if cast:
        _cast_next(*nxt_in, *nxt_out)


def _resident(shape, index):
    return pl.BlockSpec(shape, lambda t: index, pipeline_mode=pl.Buffered(1))


def _ffn(x, mod, i, which, norm_g4, w_cur, w_next, cfg, pos_tabs=None, final_g=None):
    first, last, cast = pos_tabs is not None, final_g is not None, w_next is not None
    tm = TM_FFN_FIRST if first else TM_FFN
    t_p = cfg.batch * cfg.seq
    n_p = t_p // tm
    t_tok = _n_tokens(cfg)
    n_steps = t_tok // tm
    k0 = 6 * which
    row = pl.BlockSpec((tm, D_MODEL), lambda t: (t, 0))
    row_p = pl.BlockSpec((tm, D_MODEL), lambda t: (jnp.minimum(t, n_p - 1), 0))
    row_s = pl.BlockSpec((tm, D_MODEL), lambda t: (jnp.maximum(t - n_p, 0), 0))
    half = D_MODEL // 2

    in_specs, args = [], []
    if first:
        in_specs += [row_p, row_s, pl.BlockSpec((cfg.dec_seq // GRID_W, half), lambda t: (0, 0)),
                     pl.BlockSpec((GRID_W, half), lambda t: (0, 0))]
        args += [x[0], x[1], pos_tabs[0], pos_tabs[1]]
    else:
        in_specs.append(row)
        args.append(x)
    in_specs += [_mod_spec(i, k0), _mod_spec(i, k0 + 1), _mod_spec(i, k0 + 2),
                 _norm_g_spec(i, 2 * which),
                 _resident((D_MODEL, D_FF), (0, 0)), _resident((D_MODEL, D_FF), (0, 0)),
                 _resident((D_FF, D_MODEL), (0, 0))]
    args += [mod, mod, mod, norm_g4, *w_cur]
    if last:
        in_specs.append(pl.BlockSpec((1, D_MODEL), lambda t: (0, 0)))
        args.append(final_g.reshape(1, D_MODEL))
    out_specs = [row_p, row_s] if last else [row]
    out_shape = ([jax.ShapeDtypeStruct((t_p, D_MODEL), F32), jax.ShapeDtypeStruct((t_tok - t_p, D_MODEL), F32)]
                 if last else [jax.ShapeDtypeStruct((t_tok, D_MODEL), F32)])
    if cast:
        wg_n, wu_n, wd_n, i_n, which_n = w_next
        r_in, r_dn = _cast_rows(D_MODEL, n_steps), _cast_rows(D_FF, n_steps)
        blk_in = lambda t: jnp.minimum(t, D_MODEL // r_in - 1)
        blk_dn = lambda t: jnp.minimum(t, D_FF // r_dn - 1)
        in_specs += [pl.BlockSpec((None, None, r_in, D_FF), lambda t: (i_n, which_n, blk_in(t), 0)),
                     pl.BlockSpec((None, None, r_in, D_FF), lambda t: (i_n, which_n, blk_in(t), 0)),
                     pl.BlockSpec((None, None, r_dn, D_MODEL), lambda t: (i_n, which_n, blk_dn(t), 0))]
        args += [wg_n, wu_n, wd_n]
        out_specs += [pl.BlockSpec((r_in, D_FF), lambda t: (blk_in(t), 0)),
                      pl.BlockSpec((r_in, D_FF), lambda t: (blk_in(t), 0)),
                      pl.BlockSpec((r_dn, D_MODEL), lambda t: (blk_dn(t), 0))]
        out_shape += [jax.ShapeDtypeStruct((D_MODEL, D_FF), BF16), jax.ShapeDtypeStruct((D_MODEL, D_FF), BF16),
                      jax.ShapeDtypeStruct((D_FF, D_MODEL), BF16)]
    scratch = [pltpu.VMEM((tm, D_MODEL), BF16), pltpu.VMEM((tm, D_FF), BF16)]
    if first:
        scratch.insert(0, pltpu.VMEM((tm, D_MODEL), F32))
    outs = pl.pallas_call(
        functools.partial(_ffn_kernel, cfg=cfg, tm=tm, first=first, last=last, cast=cast),
        grid=(n_steps,),
        in_specs=in_specs,
        out_specs=out_specs,
        out_shape=out_shape,
        scratch_shapes=scratch,
        compiler_params=_params(("arbitrary",)),
        name="ffn",
    )(*args)
    n_act = 2 if last else 1
    return outs[:n_act], (tuple(outs[n_act:]) if cast else None)


def _norm_mod(x_ref, sh_ref, sc_ref, g_ref, cfg):
    r = _mod_row(pl.program_id(0), TM_PROJ, cfg)
    hn = _rms(x_ref[...], g_ref[...]) * (1.0 + sc_ref[pl.ds(r, 1), :]) + sh_ref[pl.ds(r, 1), :]
    return hn.astype(BF16)


def _even_in_sub(s, xm_ref, xp_ref, xn_ref, sh_ref, sc_ref, g_ref, w_ref, cw_ref, cb_ref,
                 z_ref, xst_ref, bc_ref, pool_ref, dt_ref, hn_ref, ext_ref, cfg):
    tm, q = TM_MIX, SSD_CHUNK
    t = pl.program_id(0) * NSUB + s
    info = _tile_info(t, tm, cfg)
    r = _mod_row(t, tm, cfg)
    rows = slice(s * tm, (s + 1) * tm)
    before, after = _sub_halo(s, xm_ref, xp_ref, xn_ref, HALO_A)
    scale, shift, g = 1.0 + sc_ref[pl.ds(r, 1), :], sh_ref[pl.ds(r, 1), :], g_ref[...]
    hn_ref[s, 0:tm, :] = _rms(xm_ref[rows, :], g) * scale + shift
    hn_ref[s, tm:tm + HALO_A, :] = _rms(before, g) * scale + shift
    hn_ref[s, tm + HALO_A:, :] = _rms(after, g) * scale + shift
    hn = hn_ref[s].astype(BF16)
    o_xbc, o_pool, o_dt = D_INNER_A, D_INNER_A + D_XBC, D_INNER_A + D_XBC + D_POOL
    base = HALO_A - CONV_A // 2
    zero = jnp.zeros((HALO_A, CONV_COLS), F32)
    others = ([(z_ref.at[rows, c:c + CONV_COLS], c) for c in range(0, D_INNER_A, CONV_COLS)]
              + [(pool_ref.at[rows, c:c + CONV_COLS], o_pool + c) for c in range(0, D_POOL, CONV_COLS)])
    for c0 in range(0, D_XBC, CONV_COLS):
        cols = slice(c0, c0 + CONV_COLS)
        xg = _dot(hn, w_ref[:, o_xbc + c0:o_xbc + c0 + CONV_COLS])
        o_ref, o_col = others[c0 // CONV_COLS]
        o_ref[...] = _dot(hn[0:tm], w_ref[:, o_col:o_col + CONV_COLS])
        ext_ref[s, 0:HALO_A, cols] = jnp.where(info["is_start"], zero, xg[tm:tm + HALO_A])
        ext_ref[s, HALO_A:HALO_A + tm, cols] = xg[0:tm]
        ext_ref[s, HALO_A + tm:, cols] = jnp.where(info["is_end"], zero, xg[tm + HALO_A:])
        for ck in range(tm // q):
            for j in range(c0 // LANES, (c0 + CONV_COLS) // LANES):
                sl = slice(j * LANES, (j + 1) * LANES)
                r0 = base + ck * q
                acc = cw_ref[0:1, sl] * ext_ref[s, r0:r0 + q, sl]
                for k in range(1, CONV_A):
                    acc = acc + cw_ref[k:k + 1, sl] * ext_ref[s, r0 + k:r0 + k + q, sl]
                y = _silu(acc + cb_ref[:, sl])
                if j < D_INNER_A // LANES:
                    xst_ref[s * (tm // q) + ck, j * LANES:(j + 1) * LANES, :] = y.T
                else:
                    bc_ref[s * tm + ck * q:s * tm + (ck + 1) * q,
                           j * LANES - D_INNER_A:(j + 1) * LANES - D_INNER_A] = y.astype(BF16)
    dt_ref[rows, :] = _dot(hn[0:tm], w_ref[:, o_dt:o_dt + DT_PAD])


def _even_in_kernel(*refs, cfg):
    for s in range(NSUB):
        _even_in_sub(s, *refs, cfg)


def _even_in(x, mod, i, j, norm_g4, w, conv_w, conv_b, cfg):
    t_tok = x.shape[0]
    tm, q = NSUB * TM_MIX, SSD_CHUNK
    n_tot = D_INNER_A + D_XBC + D_POOL + DT_PAD
    small = lambda shape: pl.BlockSpec(shape, lambda t: (0,) * len(shape))
    row_block = lambda width: pl.BlockSpec((tm, width), lambda t: (t, 0))
    return pl.pallas_call(
        functools.partial(_even_in_kernel, cfg=cfg),
        grid=(t_tok // tm,),
        in_specs=_halo_specs(D_MODEL, tm, HALO_A, t_tok, lambda t: t) + [
            _mod_spec(i, 3), _mod_spec(i, 4), _norm_g_spec(i, 1), small((D_MODEL, n_tot)),
            _layer_block((CONV_A, D_XBC), j), _layer_block((1, D_XBC), j),
        ],
        out_specs=[row_block(D_INNER_A),
                   pl.BlockSpec((tm // q, D_INNER_A, q), lambda t: (t, 0, 0)),
                   row_block(2 * D_BC), row_block(D_POOL), row_block(DT_PAD)],
        out_shape=[jax.ShapeDtypeStruct((t_tok, D_INNER_A), F32),
                   jax.ShapeDtypeStruct((t_tok // q, D_INNER_A, q), F32),
                   jax.ShapeDtypeStruct((t_tok, 2 * D_BC), BF16),
                   jax.ShapeDtypeStruct((t_tok, D_POOL), F32),
                   jax.ShapeDtypeStruct((t_tok, DT_PAD), F32)],
        scratch_shapes=[pltpu.VMEM((NSUB, TM_MIX + 2 * HALO_A, D_MODEL), F32),
                        pltpu.VMEM((NSUB, TM_MIX + 2 * HALO_A, D_XBC), F32)],
        compiler_params=_params(("parallel",)),
        name="even_in",
    )(x, x, x, mod, mod, norm_g4, w, conv_w, conv_b)


def _odd_in_kernel(x_ref, sh_ref, sc_ref, g_ref, w_ref, lg_ref, lb_ref, u_ref, v_ref, glu_ref, vg_ref, *, cfg):
    hn = _norm_mod(x_ref, sh_ref, sc_ref, g_ref, cfg)
    for c in range(0, D_C, PROJ_COLS):
        cols = slice(c, c + PROJ_COLS)
        u_ref[:, cols] = jax.nn.gelu(_dot(hn, w_ref[:, c:c + PROJ_COLS]))
        vg_ref[:, cols] = jax.nn.gelu(_dot(hn, w_ref[:, D_C + c:D_C + c + PROJ_COLS]))
        ga = _dot(hn, w_ref[:, 2 * D_C + c:2 * D_C + c + PROJ_COLS])
        gg = _dot(hn, w_ref[:, 2 * D_C + D_D + c:2 * D_C + D_D + c + PROJ_COLS])
        glu_ref[:, cols] = ga * _sigmoid(gg)
    v_ref[...] = _layer_norm(vg_ref[...], lg_ref[...], lb_ref[...]).astype(BF16)


def _odd_in(x, mod, i, j, norm_g4, w, ln_g, ln_b, cfg):
    t_tok = x.shape[0]
    return pl.pallas_call(
        functools.partial(_odd_in_kernel, cfg=cfg),
        grid=(t_tok // TM_PROJ,),
        in_specs=[
            pl.BlockSpec((TM_PROJ, D_MODEL), lambda t: (t, 0)),
            _mod_spec(i, 3), _mod_spec(i, 4),
            _norm_g_spec(i, 1),
            _layer_block((D_MODEL, 2 * D_C + 2 * D_D), j),
            _layer_block((1, D_C), j), _layer_block((1, D_C), j),
        ],
        out_specs=[pl.BlockSpec((TM_PROJ, D_C), lambda t: (t, 0))] * 3,
        out_shape=[jax.ShapeDtypeStruct((t_tok, D_C), F32),
                   jax.ShapeDtypeStruct((t_tok, D_C), BF16),
                   jax.ShapeDtypeStruct((t_tok, D_D), F32)],
        scratch_shapes=[pltpu.VMEM((TM_PROJ, D_C), F32)],
        compiler_params=_params(("parallel",)),
        name="odd_in",
    )(x, mod, mod, norm_g4, w, ln_g, ln_b)


def _dt_terms(dt_ref, dtb_ref, alog_ref, ck, rows):
    q = SSD_CHUNK
    dtt = dt_ref[ck * q:(ck + 1) * q, :].T[rows, :]
    dts = _softplus(dtt + dtb_ref[rows, :])
    return dts, dts * (-jnp.exp(alog_ref[rows, :]))


def _tri(lower_incl):
    ri = lax.broadcasted_iota(jnp.int32, (SSD_CHUNK, SSD_CHUNK), 0)
    ci = lax.broadcasted_iota(jnp.int32, (SSD_CHUNK, SSD_CHUNK), 1)
    return (ri >= ci) if lower_incl else (ri <= ci)


def _cumsum_lanes(v, mask):
    return jnp.dot(v, mask.astype(F32), precision=lax.Precision.HIGHEST, preferred_element_type=F32)


def _group_bc(bc_ref, ck, g):
    rows = slice(ck * SSD_CHUNK, (ck + 1) * SSD_CHUNK)
    b = bc_ref[rows, g * D_STATE:(g + 1) * D_STATE]
    c = bc_ref[rows, D_BC + g * D_STATE:D_BC + (g + 1) * D_STATE]
    return b, c


def _state_update(s_ref, xst_ref, xd_ref, bc_ref, ck, scale, cdec):
    hd = HEAD_DIM_A
    for h in range(N_HEADS_A):
        xd_ref[ck, h * hd:(h + 1) * hd, :] = (xst_ref[ck, h * hd:(h + 1) * hd, :] * scale[h:h + 1, :]).astype(BF16)
    gw = HEADS_PER_GROUP * hd
    for g in range(N_GROUPS_A):
        b, _ = _group_bc(bc_ref, ck, g)
        upd = _dot(xd_ref[ck, g * gw:(g + 1) * gw, :], b)
        for e in range(HEADS_PER_GROUP):
            h = g * HEADS_PER_GROUP + e
            s_ref[h * hd:(h + 1) * hd, :] = (s_ref[h * hd:(h + 1) * hd, :] * cdec[h:h + 1, :]
                                             + upd[e * hd:(e + 1) * hd, :])


def _ssd_fwd_chunk(ck, xst_ref, bc_ref, dt_ref, dtb_ref, alog_ref, dsk_ref, ypt_ref, xd_ref, s_ref):
    q, hd, nh = SSD_CHUNK, HEAD_DIM_A, N_HEADS_A
    dts, dta = _dt_terms(dt_ref, dtb_ref, alog_ref, ck, slice(0, 2 * nh))
    low, upp = _tri(True), _tri(False)
    acs_f = _cumsum_lanes(dta[0:nh], upp)
    rcs_b = _cumsum_lanes(dta[nh:2 * nh], low)
    src = jnp.concatenate([acs_f - jnp.log(dts[0:nh]), rcs_b - jnp.log(dts[nh:2 * nh]),
                           jnp.zeros((LANES - 2 * nh, q), F32)], axis=0).T
    e_acs = jnp.exp(acs_f)
    neg_inf = jnp.float32(-jnp.inf)

    gw = HEADS_PER_GROUP * hd
    for g in range(N_GROUPS_A):
        b, c = _group_bc(bc_ref, ck, g)
        sct = _dot_nt(b, c)
        yoff = _dot_nt(s_ref[g * gw:(g + 1) * gw, :].astype(BF16), c)
        for e in range(HEADS_PER_GROUP):
            h = g * HEADS_PER_GROUP + e
            rows = slice(h * hd, (h + 1) * hd)
            seg_f = acs_f[h:h + 1, :] - src[:, h:h + 1]
            seg_b = rcs_b[h:h + 1, :] - src[:, nh + h:nh + h + 1]
            wt = sct * (jnp.exp(jnp.where(upp, seg_f, neg_inf)) + jnp.exp(jnp.where(low, seg_b, neg_inf)))
            xh = xst_ref[ck, rows, :]
            yd = _dot(xh.astype(BF16), wt.astype(BF16))
            ypt_ref[ck, rows, :] = yd + yoff[e * hd:(e + 1) * hd, :] * e_acs[h:h + 1, :] + dsk_ref[rows, :] * xh

    tot = acs_f[:, q - 1:q]
    scale = dts[0:nh] * jnp.exp(tot - acs_f)
    cdec = jnp.exp(jnp.broadcast_to(tot, (nh, D_STATE)))
    _state_update(s_ref, xst_ref, xd_ref, bc_ref, ck, scale, cdec)


def _scan_init(is_prompt, s0_ref):
    return jnp.where(is_prompt, jnp.zeros(s0_ref.shape, F32), s0_ref[...])


def _ssd_fwd_kernel(xst_ref, bc_ref, dt_ref, dtb_ref, alog_ref, dsk_ref, s0_ref, ypt_ref, sfin_ref, xd_ref, snap_ref,
                    s_ref, *, cfg):
    t = pl.program_id(0)
    cps = TM_MIX // SSD_CHUNK

    @pl.when(t == 0)
    def _():
        s_ref[...] = jnp.zeros_like(s_ref)

    for s in range(NSUB):
        info = _tile_info(t * NSUB + s, TM_MIX, cfg)
        s_ref[...] = jnp.where(info["is_start"], _scan_init(info["is_prompt"], s0_ref), s_ref[...])
        for ck in range(s * cps, (s + 1) * cps):
            _ssd_fwd_chunk(ck, xst_ref, bc_ref, dt_ref, dtb_ref, alog_ref, dsk_ref, ypt_ref, xd_ref, s_ref)
        snap_ref[s] = s_ref[...]

    @pl.when(t * NSUB < cfg.batch * (cfg.seq // TM_MIX))
    def _():
        sfin_ref[...] = snap_ref[...]


def _ssd_bwd_chunk(ck, dt_ref, z_ref, ypt_ref, xst_ref, bc_ref, dtb_ref, alog_ref, yt_ref, y_ref, xd_ref, s_ref):
    q, hd, nh = SSD_CHUNK, HEAD_DIM_A, N_HEADS_A
    dts, dta = _dt_terms(dt_ref, dtb_ref, alog_ref, ck, slice(nh, 2 * nh))
    rcs_b = _cumsum_lanes(dta, _tri(True))
    e_rcs = jnp.exp(rcs_b)

    gw = HEADS_PER_GROUP * hd
    for g in range(N_GROUPS_A):
        _, c = _group_bc(bc_ref, ck, g)
        yoff = _dot_nt(s_ref[g * gw:(g + 1) * gw, :].astype(BF16), c)
        for e in range(HEADS_PER_GROUP):
            h = g * HEADS_PER_GROUP + e
            yt_ref[ck, h * hd:(h + 1) * hd, :] = (ypt_ref[ck, h * hd:(h + 1) * hd, :]
                                                  + yoff[e * hd:(e + 1) * hd, :] * e_rcs[h:h + 1, :])

    rows = slice(ck * q, (ck + 1) * q)
    for j in range(D_INNER_A // LANES):
        sl = slice(j * LANES, (j + 1) * LANES)
        y_ref[rows, sl] = yt_ref[ck, j * LANES:(j + 1) * LANES, :].T * _silu(z_ref[rows, sl])

    tot = rcs_b[:, 0:1]
    scale = dts * jnp.exp(tot - rcs_b)
    cdec = jnp.exp(jnp.broadcast_to(tot, (nh, D_STATE)))
    _state_update(s_ref, xst_ref, xd_ref, bc_ref, ck, scale, cdec)


def _ssd_bwd_kernel(dt_ref, z_ref, ypt_ref, xst_ref, bc_ref, dtb_ref, alog_ref, ng_ref, s0_ref, sf_ref, *rest,
                    cfg, n_steps, layer):
    if layer == 0:
        ya_ref, so_ref, xd_ref, yt_ref, y_ref, snap_ref, s_ref = rest
    else:
        sprev_ref, ya_ref, so_ref, xd_ref, yt_ref, y_ref, snap_ref, s_ref = rest
    t = n_steps - 1 - pl.program_id(0)
    cps = TM_MIX // SSD_CHUNK

    @pl.when(pl.program_id(0) == 0)
    def _():
        s_ref[...] = jnp.zeros_like(s_ref)

    for s in reversed(range(NSUB)):
        info = _tile_info(t * NSUB + s, TM_MIX, cfg)
        s_ref[...] = jnp.where(info["is_end"], _scan_init(info["is_prompt"], s0_ref), s_ref[...])
        for ck in reversed(range(s * cps, (s + 1) * cps)):
            _ssd_bwd_chunk(ck, dt_ref, z_ref, ypt_ref, xst_ref, bc_ref, dtb_ref, alog_ref, yt_ref, y_ref, xd_ref, s_ref)
        snap_ref[s] = s_ref[...]
    ya_ref[...] = _rms(y_ref[...], ng_ref[...]).astype(BF16)

    @pl.when(t * NSUB < cfg.batch * (cfg.seq // TM_MIX))
    def _():
        if layer == 0:
            so_ref[:, 0] = sf_ref[...]
            so_ref[:, 1] = snap_ref[...]
        else:
            so_ref[:, 0] = sprev_ref[...]
            so_ref[:, 1, 0] = sf_ref[...]
            so_ref[:, 1, 1] = snap_ref[...]


def _halo_specs(width, tile, halo, n_rows, chunk_of):
    per = tile // halo
    last = n_rows // halo - 1
    return [
        pl.BlockSpec((tile, width), lambda i: (chunk_of(i), 0)),
        pl.BlockSpec((halo, width), lambda i: (jnp.maximum(chunk_of(i) * per - 1, 0), 0)),
        pl.BlockSpec((halo, width), lambda i: (jnp.minimum((chunk_of(i) + 1) * per, last), 0)),
    ]


def _ssd(xst, bc, dt, z, dtb, alog, dsk_t, norm_g, s0, s_prev, layer, cfg):
    t_tok = bc.shape[0]
    q, tm = SSD_CHUNK, NSUB * TM_MIX
    cpt = tm // q
    n_steps = t_tok // tm
    hp = N_HEADS_A * HEAD_DIM_A
    nst = cfg.dec_seq // tm
    n_pt = (cfg.batch * cfg.seq) // tm

    def s_seq(t):
        return jnp.clip(lax.div(jnp.maximum(t - n_pt, 0), nst), 0, cfg.dec_batch - 1)

    def p_blk(t):
        return jnp.minimum(t, n_pt - 1)

    small = lambda shape: pl.BlockSpec(shape, lambda i: (0,) * len(shape))
    head_rows = _layer_block((2 * N_HEADS_A, q), layer)
    state_scratch = [pltpu.VMEM((NSUB, hp, D_STATE), F32), pltpu.VMEM((hp, D_STATE), F32)]

    ypt, sf = pl.pallas_call(
        functools.partial(_ssd_fwd_kernel, cfg=cfg),
        grid=(n_steps,),
        in_specs=[
            pl.BlockSpec((cpt, hp, q), lambda i: (i, 0, 0)),
            pl.BlockSpec((tm, 2 * D_BC), lambda i: (i, 0)),
            pl.BlockSpec((tm, DT_PAD), lambda i: (i, 0)),
            head_rows, head_rows, _layer_block((hp, q), layer),
            pl.BlockSpec((None, None, None, hp, D_STATE), lambda i: (s_seq(i), layer, 0, 0, 0)),
        ],
        out_specs=[pl.BlockSpec((cpt, hp, q), lambda i: (i, 0, 0)),
                   pl.BlockSpec((NSUB, hp, D_STATE), lambda i: (p_blk(i), 0, 0))],
        out_shape=[jax.ShapeDtypeStruct((t_tok // q, hp, q), F32),
                   jax.ShapeDtypeStruct((cfg.batch, hp, D_STATE), F32)],
        scratch_shapes=[pltpu.VMEM((cpt, hp, q), BF16)] + state_scratch,
        compiler_params=_params(("arbitrary",)),
        name="ssd_fwd",
    )(xst, bc, dt, dtb, alog, dsk_t, s0)

    rev = lambda i: n_steps - 1 - i
    in_specs = [
        pl.BlockSpec((tm, DT_PAD), lambda i: (rev(i), 0)),
        pl.BlockSpec((tm, D_INNER_A), lambda i: (rev(i), 0)),
        pl.BlockSpec((cpt, hp, q), lambda i: (rev(i), 0, 0)),
        pl.BlockSpec((cpt, hp, q), lambda i: (rev(i), 0, 0)),
        pl.BlockSpec((tm, 2 * D_BC), lambda i: (rev(i), 0)),
        head_rows, head_rows, _layer_block((1, D_INNER_A), layer),
        pl.BlockSpec((None, None, None, hp, D_STATE), lambda i: (s_seq(rev(i)), layer, 1, 0, 0)),
        pl.BlockSpec((NSUB, hp, D_STATE), lambda i: (p_blk(rev(i)), 0, 0)),
    ]
    args = [dt, z, ypt, xst, bc, dtb, alog, norm_g, s0, sf]
    if layer == 0:
        so_block, so_shape = (NSUB, 2, hp, D_STATE), (cfg.batch, 2, hp, D_STATE)
        so_index = lambda i: (p_blk(rev(i)), 0, 0, 0)
    else:
        in_specs.append(pl.BlockSpec((NSUB, 2, hp, D_STATE), lambda i: (p_blk(rev(i)), 0, 0, 0)))
        args.append(s_prev)
        so_block, so_shape = (NSUB, 2, 2, hp, D_STATE), (cfg.batch, 2, 2, hp, D_STATE)
        so_index = lambda i: (p_blk(rev(i)), 0, 0, 0, 0)
    ya, so = pl.pallas_call(
        functools.partial(_ssd_bwd_kernel, cfg=cfg, n_steps=n_steps, layer=layer),
        grid=(n_steps,),
        in_specs=in_specs,
        out_specs=[pl.BlockSpec((tm, D_INNER_A), lambda i: (rev(i), 0)), pl.BlockSpec(so_block, so_index)],
        out_shape=[jax.ShapeDtypeStruct((t_tok, D_INNER_A), BF16), jax.ShapeDtypeStruct(so_shape, F32)],
        scratch_shapes=[pltpu.VMEM((cpt, hp, q), BF16), pltpu.VMEM((cpt, hp, q), F32),
                        pltpu.VMEM((tm, D_INNER_A), F32)] + state_scratch,
        compiler_params=_params(("arbitrary",)),
        name="ssd_bwd",
    )(*args)
    return ya, so


def _sub_halo(s, main_ref, prev_ref, next_ref, halo):
    tm = TM_MIX
    before = prev_ref[...] if s == 0 else main_ref[s * tm - halo:s * tm, :]
    after = next_ref[...] if s == NSUB - 1 else main_ref[(s + 1) * tm:(s + 1) * tm + halo, :]
    return before, after


def _even_out_sub(s, ya_ref, pm_ref, pp_ref, pn_ref, x_ref, gt_ref, pw_ref, ps_ref, woa_ref, wop_ref,
                  o_ref, ext_ref, yp_ref, cfg):
    tm = TM_MIX
    t = pl.program_id(0) * NSUB + s
    info = _tile_info(t, tm, cfg)
    r = _mod_row(t, tm, cfg)
    rows = slice(s * tm, (s + 1) * tm)
    before, after = _sub_halo(s, pm_ref, pp_ref, pn_ref, HALO_A)
    zero = jnp.zeros((HALO_A, D_POOL), F32)
    ext_ref[s, 0:HALO_A, :] = jnp.where(info["is_start"], zero, before)
    ext_ref[s, HALO_A:HALO_A + tm, :] = pm_ref[rows, :]
    ext_ref[s, HALO_A + tm:, :] = jnp.where(info["is_end"], zero, after)
    pos = info["pos"] * tm + lax.broadcasted_iota(jnp.int32, (tm, POOL_GROUP_DIM), 0)
    oc = D_MODEL // len(POOL_WINDOWS)
    for gi, win in enumerate(POOL_WINDOWS):
        o_ref[rows, gi * oc:(gi + 1) * oc] = _dot(ya_ref[rows, :], woa_ref[:, gi * oc:(gi + 1) * oc])
        sl = slice(gi * POOL_GROUP_DIM, (gi + 1) * POOL_GROUP_DIM)
        lo = HALO_A - win // 2
        acc = ext_ref[s, lo:lo + tm, sl]
        for j in range(1, win):
            acc = acc + ext_ref[s, lo + j:lo + j + tm, sl]
        cnt = (jnp.minimum(pos - win // 2 + win, info["seq_len"]) - jnp.maximum(pos - win // 2, 0)).astype(F32)
        pooled = acc / cnt - ext_ref[s, HALO_A:HALO_A + tm, sl]
        yp_ref[rows, sl] = (_dot(pooled.astype(BF16), pw_ref[gi]) * ps_ref[:, sl]).astype(BF16)
    out = o_ref[rows, :] + _dot(yp_ref[rows, :], wop_ref[...])
    o_ref[rows, :] = x_ref[rows, :] + gt_ref[pl.ds(r, 1), :] * out


def _even_out_kernel(*refs, cfg):
    for s in range(NSUB):
        _even_out_sub(s, *refs, cfg)


def _even_out(ya, pool_in, x, mod, i, j, pool_w, pool_scale, w_out, cfg):
    t_tok = x.shape[0]
    tm = NSUB * TM_MIX
    small = lambda shape: pl.BlockSpec(shape, lambda t: (0,) * len(shape))
    return pl.pallas_call(
        functools.partial(_even_out_kernel, cfg=cfg),
        grid=(t_tok // tm,),
        in_specs=[pl.BlockSpec((tm, D_INNER_A), lambda t: (t, 0))]
        + _halo_specs(D_POOL, tm, HALO_A, t_tok, lambda t: t) + [
            pl.BlockSpec((tm, D_MODEL), lambda t: (t, 0)),
            _mod_spec(i, 5),
            _layer_block((len(POOL_WINDOWS), POOL_GROUP_DIM, POOL_GROUP_DIM), j), _layer_block((1, D_POOL), j),
            _layer_block((D_INNER_A, D_MODEL), j), _layer_block((D_POOL, D_MODEL), j, D_INNER_A // D_POOL),
        ],
        out_specs=pl.BlockSpec((tm, D_MODEL), lambda t: (t, 0)),
        out_shape=jax.ShapeDtypeStruct((t_tok, D_MODEL), F32),
        scratch_shapes=[pltpu.VMEM((NSUB, TM_MIX + 2 * HALO_A, D_POOL), F32), pltpu.VMEM((tm, D_POOL), BF16)],
        compiler_params=_params(("parallel",)),
        name="even_out",
    )(ya, pool_in, pool_in, pool_in, x, mod, pool_w, pool_scale, w_out, w_out)


def _odd_out_sub(s, u_ref, v_ref, gm_ref, gp_ref, gn_ref, x_ref, gt_ref, spw_ref, spb_ref, dww_ref, dwb_ref,
                 cg_ref, cb_ref, woc_ref, wod_ref, o_ref, ext_ref, sh_ref, yc_ref, yd_ref, cfg):
    tm = TM_MIX
    t = pl.program_id(0) * NSUB + s
    info = _tile_info(t, tm, cfg)
    r = _mod_row(t, tm, cfg)
    row0 = s * tm
    tile_rows = slice(row0, row0 + tm)
    hc = D_C // N_HEADS_C
    for ck in range(tm // MLP_CHUNK):
        rows = slice(row0 + ck * MLP_CHUNK, row0 + (ck + 1) * MLP_CHUNK)
        for h in range(N_HEADS_C):
            sl = slice(h * hc, (h + 1) * hc)
            sv = _dot(spw_ref[h], v_ref[rows, sl]) + spb_ref[:, sl]
            yc_ref[rows, sl] = (u_ref[rows, sl] * sv).astype(BF16)
    before, after = _sub_halo(s, gm_ref, gp_ref, gn_ref, HALO_D)
    zero = jnp.zeros((HALO_D, D_D), F32)
    ext_ref[s, 0:HALO_D, :] = jnp.where(info["is_start"], zero, before)
    ext_ref[s, HALO_D:HALO_D + tm, :] = gm_ref[tile_rows, :]
    ext_ref[s, HALO_D + tm:, :] = jnp.where(info["is_end"], zero, after)
    span = tm + 2 * HALO_D - SUBLANES
    base = HALO_D - CONV_D // 2
    rb = MLP_CHUNK
    for j in range(D_D // LANES):
        sl = slice(j * LANES, (j + 1) * LANES)
        if (j * LANES) % PROJ_COLS == 0:
            oc = slice(j * LANES, j * LANES + PROJ_COLS)
            o_ref[tile_rows, oc] = _dot(yc_ref[tile_rows, :], woc_ref[:, oc])
        for sh in range(1, SUBLANES):
            sh_ref[sh - 1, :, sl] = ext_ref[s, sh:sh + span, sl]
        for rr in range(tm // rb):
            acc = None
            for k in range(CONV_D):
                q8, sh = divmod(base + k, SUBLANES)
                r0 = q8 * SUBLANES + rr * rb
                tap = ext_ref[s, r0:r0 + rb, sl] if sh == 0 else sh_ref[sh - 1, r0:r0 + rb, sl]
                term = dww_ref[k:k + 1, sl] * tap
                acc = term if acc is None else acc + term
            yd_ref[row0 + rr * rb:row0 + (rr + 1) * rb, sl] = acc + dwb_ref[:, sl]
    ydn = _silu(_layer_norm(yd_ref[tile_rows, :], cg_ref[...], cb_ref[...])).astype(BF16)
    out = o_ref[tile_rows, :] + _dot(ydn, wod_ref[...])
    o_ref[tile_rows, :] = x_ref[tile_rows, :] + gt_ref[pl.ds(r, 1), :] * out


def _odd_out_kernel(*refs, cfg):
    for s in range(NSUB):
        _odd_out_sub(s, *refs, cfg)


def _odd_out(u, v, glu, x, mod, i, j, sp_w, sp_b_full, dw_w, dw_b, cn_g, cn_b, w_out, cfg):
    t_tok = x.shape[0]
    tm = NSUB * TM_MIX
    small = lambda shape: pl.BlockSpec(shape, lambda t: (0,) * len(shape))
    return pl.pallas_call(
        functools.partial(_odd_out_kernel, cfg=cfg),
        grid=(t_tok // tm,),
        in_specs=[pl.BlockSpec((tm, D_C), lambda t: (t, 0)), pl.BlockSpec((tm, D_C), lambda t: (t, 0))]
        + _halo_specs(D_D, tm, HALO_D, t_tok, lambda t: t) + [
            pl.BlockSpec((tm, D_MODEL), lambda t: (t, 0)),
            _mod_spec(i, 5),
            _layer_block((N_HEADS_C, MLP_CHUNK, MLP_CHUNK), j), _layer_block((MLP_CHUNK, D_C), j),
            _layer_block((CONV_D, D_D), j), _layer_block((1, D_D), j), _layer_block((1, D_D), j),
            _layer_block((1, D_D), j),
            _layer_block((D_C, D_MODEL), j), _layer_block((D_D, D_MODEL), j, 1),
        ],
        out_specs=pl.BlockSpec((tm, D_MODEL), lambda t: (t, 0)),
        out_shape=jax.ShapeDtypeStruct((t_tok, D_MODEL), F32),
        scratch_shapes=[pltpu.VMEM((NSUB, TM_MIX + 2 * HALO_D, D_D), F32),
                        pltpu.VMEM((SUBLANES - 1, TM_MIX + 2 * HALO_D - SUBLANES, D_D), F32),
                        pltpu.VMEM((tm, D_C), BF16), pltpu.VMEM((tm, D_D), F32)],
        compiler_params=_params(("parallel",)),
        name="odd_out",
    )(u, v, glu, glu, glu, x, mod, sp_w, sp_b_full, dw_w, dw_b, cn_g, cn_b, w_out, w_out)


def _position_tables(dec_seq):
    quarter = D_MODEL // 4
    freqs = jnp.exp(-math.log(POS_BASE) * jnp.arange(quarter, dtype=F32) / quarter)
    ang_r = jnp.arange(dec_seq // GRID_W, dtype=F32)[:, None] * freqs
    ang_c = jnp.arange(GRID_W, dtype=F32)[:, None] * freqs
    return (jnp.concatenate([jnp.sin(ang_r), jnp.cos(ang_r)], axis=-1),
            jnp.concatenate([jnp.sin(ang_c), jnp.cos(ang_c)], axis=-1))


def _forward(cfg, x_prompt, x_sample, c, state_ssd, c_ctx, w_mod, b_mod, norm_g, ffn_w_gate, ffn_w_up,
             ffn_w_down, ev_w_in, ev_conv_w, ev_conv_b, ev_dt_bias, ev_a_log, ev_d_skip, ev_ssd_norm_g,
             ev_pool_w, ev_pool_scale, ev_w_out, od_w_in, od_v_ln_g, od_v_ln_b, od_sp_w, od_sp_b, od_dw_w,
             od_dw_b, od_cn_g, od_cn_b, od_w_out, final_norm_g):
    assert DEPTH == 4 and cfg.seq % TM_MIX == 0 and cfg.dec_seq % TM_FFN == 0 and (cfg.batch * cfg.seq) % TM_FFN == 0
    t_p = cfg.batch * cfg.seq
    hp = N_HEADS_A * HEAD_DIM_A
    row_tab, col_tab = _position_tables(cfg.dec_seq)

    cond8 = jnp.concatenate([c_ctx[None, :], c, jnp.zeros((SUBLANES - 1 - cfg.dec_batch, D_MODEL), F32)], axis=0)
    mod = _modulation(cond8, w_mod, b_mod)

    w_ffn = tuple(w[0, 0].astype(BF16) for w in (ffn_w_gate, ffn_w_up, ffn_w_down))
    f32_ffn = (ffn_w_gate, ffn_w_up, ffn_w_down)
    norm_g4 = norm_g.reshape(DEPTH, 3, 1, D_MODEL)
    n_odd = od_w_in.shape[0]
    lane_b = lambda v: jnp.broadcast_to(v.reshape(N_EVEN, 2 * N_HEADS_A, 1), (N_EVEN, 2 * N_HEADS_A, SSD_CHUNK))
    ev = dict(
        conv_b=ev_conv_b.reshape(N_EVEN, 1, D_XBC),
        dt_bias=lane_b(ev_dt_bias), a_log=lane_b(ev_a_log),
        d_skip=jnp.broadcast_to(ev_d_skip[:, :, None, None],
                                (N_EVEN, N_HEADS_A, HEAD_DIM_A, SSD_CHUNK)).reshape(N_EVEN, hp, SSD_CHUNK),
        norm_g=ev_ssd_norm_g.reshape(N_EVEN, 1, D_INNER_A),
        pool_w=ev_pool_w.astype(BF16), pool_scale=ev_pool_scale.reshape(N_EVEN, 1, D_POOL),
        w_out=ev_w_out.astype(BF16),
    )
    od = dict(
        w_in=od_w_in.astype(BF16), ln_g=od_v_ln_g.reshape(n_odd, 1, D_C), ln_b=od_v_ln_b.reshape(n_odd, 1, D_C),
        sp_w=od_sp_w.astype(BF16),
        sp_b=jnp.broadcast_to(jnp.swapaxes(od_sp_b, 1, 2)[:, :, :, None],
                              (n_odd, MLP_CHUNK, N_HEADS_C, D_C // N_HEADS_C)).reshape(n_odd, MLP_CHUNK, D_C),
        dw_b=od_dw_b.reshape(n_odd, 1, D_D), cn_g=od_cn_g.reshape(n_odd, 1, D_D), cn_b=od_cn_b.reshape(n_odd, 1, D_D),
        w_out=od_w_out.astype(BF16),
    )
    s0 = state_ssd.reshape(cfg.dec_batch, N_EVEN, 2, hp, D_STATE)
    states = None
    for i in range(DEPTH):
        j = i // 2
        if i == 0:
            (x,), w_ffn = _ffn((x_prompt.reshape(t_p, D_MODEL), x_sample.reshape(-1, D_MODEL)), mod, i, 0, norm_g4,
                               w_ffn, f32_ffn + (i, 1), cfg, pos_tabs=(row_tab, col_tab))
        else:
            (x,), w_ffn = _ffn(x, mod, i, 0, norm_g4, w_ffn, f32_ffn + (i, 1), cfg)
        if i % 2 == 0:
            w = ev_w_in[j]
            o_dt = D_INNER_A + D_XBC
            o_pool = o_dt + 2 * N_HEADS_A
            w_all = jnp.concatenate(
                [w[:, :o_dt], w[:, o_pool:], w[:, o_dt:o_pool],
                 jnp.zeros((D_MODEL, DT_PAD - 2 * N_HEADS_A), F32)], axis=1).astype(BF16)
            z, xst, bc, pool_in, dt = _even_in(x, mod, i, j, norm_g4, w_all, ev_conv_w, ev["conv_b"], cfg)
            ya, states = _ssd(xst, bc, dt, z, ev["dt_bias"], ev["a_log"], ev["d_skip"], ev["norm_g"], s0, states, j, cfg)
            x = _even_out(ya, pool_in, x, mod, i, j, ev["pool_w"], ev["pool_scale"], ev["w_out"], cfg)
        else:
            u, v, glu = _odd_in(x, mod, i, j, norm_g4, od["w_in"], od["ln_g"], od["ln_b"], cfg)
            x = _odd_out(u, v, glu, x, mod, i, j, od["sp_w"], od["sp_b"], od_dw_w, od["dw_b"], od["cn_g"], od["cn_b"],
                         od["w_out"], cfg)
        if i < DEPTH - 1:
            (x,), w_ffn = _ffn(x, mod, i, 1, norm_g4, w_ffn, f32_ffn + (i + 1, 0), cfg)
        else:
            (y_p, y_s), _ = _ffn(x, mod, i, 1, norm_g4, w_ffn, None, cfg, final_g=final_norm_g)

    return (y_p.reshape(cfg.batch, cfg.seq, D_MODEL), y_s.reshape(cfg.dec_batch, cfg.dec_seq, D_MODEL),
            states.reshape(cfg.batch, N_EVEN, 2, N_HEADS_A, HEAD_DIM_A, D_STATE))


def kernel(x_prompt, x_sample, c, state_ssd, c_ctx, w_mod, b_mod, norm_g, ffn_w_gate, ffn_w_up, ffn_w_down, ev_w_in, ev_conv_w, ev_conv_b, ev_dt_bias, ev_a_log, ev_d_skip, ev_ssd_norm_g, ev_pool_w, ev_pool_scale, ev_w_out, od_w_in, od_v_ln_g, od_v_ln_b, od_sp_w, od_sp_b, od_dw_w, od_dw_b, od_cn_g, od_cn_b, od_w_out, final_norm_g):
    cfg = Cfg(x_prompt.shape[0], x_prompt.shape[1], x_sample.shape[0], x_sample.shape[1])
    return _forward(cfg, x_prompt, x_sample, c, state_ssd, c_ctx, w_mod, b_mod, norm_g, ffn_w_gate, ffn_w_up,
                    ffn_w_down, ev_w_in, ev_conv_w, ev_conv_b, ev_dt_bias, ev_a_log, ev_d_skip, ev_ssd_norm_g,
                    ev_pool_w, ev_pool_scale, ev_w_out, od_w_in, od_v_ln_g, od_v_ln_b, od_sp_w, od_sp_b, od_dw_w,
                    od_dw_b, od_cn_g, od_cn_b, od_w_out, final_norm_g)
```

```python
import collections
import functools
import math

import jax
import jax.numpy as jnp
from jax import lax
from jax.experimental import pallas as pl
from jax.experimental.pallas import tpu as pltpu

F32 = jnp.float32
BF16 = jnp.bfloat16

D_MODEL = 1024
DEPTH = 4
GRID_W = 64
POS_BASE = 10000.0
EPS = 1e-6
N_MOD = 9
D_FF = 2816
N_HEADS_A = 16
HEAD_DIM_A = 64
D_INNER_A = N_HEADS_A * HEAD_DIM_A
N_GROUPS_A = 2
HEADS_PER_GROUP = N_HEADS_A // N_GROUPS_A
D_STATE = 128
CONV_A = 5
SSD_CHUNK = 128
D_BC = N_GROUPS_A * D_STATE
D_XBC = D_INNER_A + 2 * D_BC
POOL_WINDOWS = (2, 4, 8, 16)
POOL_GROUP_DIM = 128
D_POOL = len(POOL_WINDOWS) * POOL_GROUP_DIM
N_HEADS_C = 8
MLP_CHUNK = 128
D_C = 1024
D_D = 1024
CONV_D = 31
N_EVEN = (DEPTH + 1) // 2

LANES = 128
SUBLANES = 8
DT_PAD = LANES
HALO_A = SUBLANES
HALO_D = 2 * SUBLANES
VMEM_LIMIT = 56 * 1024 * 1024

TM_FFN = 1024
TM_FFN_FIRST = 512
TF_FFN = 256
TM_PROJ = 512
TM_MIX = 256
CONV_COLS = 256
PROJ_COLS = 256
NSUB = 2
Cfg = collections.namedtuple("Cfg", "batch seq dec_batch dec_seq")


def _n_tokens(cfg):
    return cfg.batch * cfg.seq + cfg.dec_batch * cfg.dec_seq


def _tile_info(t, tm, cfg):
    npt = cfg.seq // tm
    nst = cfg.dec_seq // tm
    n_p = cfg.batch * npt
    is_prompt = t < n_p
    ts = jnp.maximum(t - n_p, 0)
    pos = jnp.where(is_prompt, lax.rem(t, npt), lax.rem(ts, nst))
    last = jnp.where(is_prompt, npt - 1, nst - 1)
    info = dict(
        is_prompt=is_prompt,
        is_start=pos == 0,
        is_end=pos == last,
        pos=pos,
        seq_len=jnp.where(is_prompt, cfg.seq, cfg.dec_seq),
    )
    return info


def _mod_row(t, tm, cfg):
    n_p = (cfg.batch * cfg.seq) // tm
    per = cfg.dec_seq // tm
    return jnp.where(t < n_p, 0, 1 + lax.div(jnp.maximum(t - n_p, 0), per))


def _sigmoid(x):
    return 0.5 + 0.5 * jnp.tanh(0.5 * x)


def _silu(x):
    h = 0.5 * x
    return h + h * jnp.tanh(h)


def _softplus(x):
    return jnp.maximum(x, 0.0) + jnp.log1p(jnp.exp(-jnp.abs(x)))


def _rms(x, g):
    return x * lax.rsqrt(jnp.mean(x * x, axis=-1, keepdims=True) + EPS) * g


def _layer_norm(x, g, b):
    xc = x - jnp.mean(x, axis=-1, keepdims=True)
    y = xc * lax.rsqrt(jnp.mean(xc * xc, axis=-1, keepdims=True) + EPS)
    return y * g + b


def _dot(a, b):
    return jnp.dot(a, b, preferred_element_type=F32)


def _dot_nt(a, b):
    return lax.dot_general(a, b, (((1,), (1,)), ((), ())), preferred_element_type=F32)


def _params(sem):
    return pltpu.CompilerParams(dimension_semantics=sem, vmem_limit_bytes=VMEM_LIMIT)


def _mod_kernel(c_ref, w_ref, b_ref, o_ref):
    c = c_ref[...]
    sc = _silu(c).astype(BF16)
    o_ref[...] = _dot(sc, w_ref[...].astype(BF16)) + b_ref[...]


def _modulation(cond8, w_mod, b_mod):
    b4 = b_mod.reshape(DEPTH, N_MOD, 1, D_MODEL)
    return pl.pallas_call(
        _mod_kernel,
        grid=(DEPTH, N_MOD),
        in_specs=[
            pl.BlockSpec((SUBLANES, D_MODEL), lambda i, k: (0, 0)),
            pl.BlockSpec((None, D_MODEL, D_MODEL), lambda i, k: (i, 0, k)),
            pl.BlockSpec((None, None, 1, D_MODEL), lambda i, k: (i, k, 0, 0)),
        ],
        out_specs=pl.BlockSpec((None, None, SUBLANES, D_MODEL), lambda i, k: (i, k, 0, 0)),
        out_shape=jax.ShapeDtypeStruct((DEPTH, N_MOD, SUBLANES, D_MODEL), F32),
        compiler_params=_params(("arbitrary", "arbitrary")),
        name="modulation",
    )(cond8, w_mod, b4)


def _mod_spec(i, k):
    return pl.BlockSpec((None, None, SUBLANES, D_MODEL), lambda t: (i, k, 0, 0))


def _layer_block(shape, j, row=0):
    return pl.BlockSpec((None,) + tuple(shape), lambda t: (j, row) + (0,) * (len(shape) - 1))


def _norm_g_spec(i, k):
    return pl.BlockSpec((None, None, 1, D_MODEL), lambda t: (i, k, 0, 0))


def _ffn_body(x, r, sh_ref, sc_ref, gt_ref, g_ref, wg_ref, wu_ref, wd_ref, hn_ref, act_ref):
    hn = _rms(x, g_ref[...]) * (1.0 + sc_ref[pl.ds(r, 1), :]) + sh_ref[pl.ds(r, 1), :]
    hn_ref[...] = hn.astype(BF16)
    for f in range(D_FF // TF_FFN):
        sl = slice(f * TF_FFN, (f + 1) * TF_FFN)
        g = _dot(hn_ref[...], wg_ref[:, sl])
        u = _dot(hn_ref[...], wu_ref[:, sl])
        act_ref[:, sl] = (_silu(g) * u).astype(BF16)
    return x + (0.5 * gt_ref[pl.ds(r, 1), :]) * _dot(act_ref[...], wd_ref[...])


def _cast_rows(total, n_steps):
    rows = 2 * SUBLANES
    while rows * n_steps < total or total % rows:
        rows += 2 * SUBLANES
    return rows


def _cast_next(wgn_ref, wun_ref, wdn_ref, wgo_ref, wuo_ref, wdo_ref):
    wgo_ref[...] = wgn_ref[...].astype(BF16)
    wuo_ref[...] = wun_ref[...].astype(BF16)
    wdo_ref[...] = wdn_ref[...].astype(BF16)


def _ffn_kernel(*refs, cfg, tm, first, last, cast):
    refs = list(refs)
    t = pl.program_id(0)
    n_p = (cfg.batch * cfg.seq) // tm
    is_prompt = t < n_p
    if first:
        xp_ref, xs_ref, rt_ref, ct_ref = refs[:4]
        del refs[:4]
    else:
        x_in_ref = refs.pop(0)
    sh_ref, sc_ref, gt_ref, g_ref, wg_ref, wu_ref, wd_ref = refs[:7]
    del refs[:7]
    if last:
        fg_ref = refs.pop(0)
    if cast:
        nxt_in = refs[:3]
        del refs[:3]
    outs = refs[:2 if last else 1]
    del refs[:len(outs)]
    if cast:
        nxt_out = refs[:3]
        del refs[:3]
    if first:
        x_ref = refs.pop(0)
    hn_ref, act_ref = refs

    if first:
        @pl.when(is_prompt)
        def _():
            x_ref[...] = xp_ref[...]

        @pl.when(jnp.logical_not(is_prompt))
        def _():
            half = D_MODEL // 2
            row0 = lax.rem(jnp.maximum(t - n_p, 0), cfg.dec_seq // tm) * (tm // GRID_W)
            for rr in range(tm // GRID_W):
                rows = slice(rr * GRID_W, (rr + 1) * GRID_W)
                x_ref[rows, 0:half] = xs_ref[rows, 0:half] + rt_ref[pl.ds(row0 + rr, 1), :]
                x_ref[rows, half:D_MODEL] = xs_ref[rows, half:D_MODEL] + ct_ref[...]
        x = x_ref[...]
    else:
        x = x_in_ref[...]

    r = _mod_row(t, tm, cfg)
    xn = _ffn_body(x, r, sh_ref, sc_ref, gt_ref, g_ref, wg_ref, wu_ref, wd_ref, hn_ref, act_ref)
    if last:
        y = _rms(xn, fg_ref[...])

        @pl.when(is_prompt)
        def _():
            outs[0][...] = y

        @pl.when(jnp.logical_not(is_prompt))
        def _():
            outs[1][...] = y
    else:
        outs[0][...] = xn
    if cast:
        _cast_next(*nxt_in, *nxt_out)


def _resident(shape, index):
    return pl.BlockSpec(shape, lambda t: index, pipeline_mode=pl.Buffered(1))


def _ffn(x, mod, i, which, norm_g4, w_cur, w_next, cfg, pos_tabs=None, final_g=None):
    first, last, cast = pos_tabs is not None, final_g is not None, w_next is not None
    tm = TM_FFN_FIRST if first else TM_FFN
    t_p = cfg.batch * cfg.seq
    n_p = t_p // tm
    t_tok = _n_tokens(cfg)
    n_steps = t_tok // tm
    k0 = 6 * which
    row = pl.BlockSpec((tm, D_MODEL), lambda t: (t, 0))
    row_p = pl.BlockSpec((tm, D_MODEL), lambda t: (jnp.minimum(t, n_p - 1), 0))
    row_s = pl.BlockSpec((tm, D_MODEL), lambda t: (jnp.maximum(t - n_p, 0), 0))
    half = D_MODEL // 2

    in_specs, args = [], []
    if first:
        in_specs += [row_p, row_s, pl.BlockSpec((cfg.dec_seq // GRID_W, half), lambda t: (0, 0)),
                     pl.BlockSpec((GRID_W, half), lambda t: (0, 0))]
        args += [x[0], x[1], pos_tabs[0], pos_tabs[1]]
    else:
        in_specs.append(row)
        args.append(x)
    in_specs += [_mod_spec(i, k0), _mod_spec(i, k0 + 1), _mod_spec(i, k0 + 2),
                 _norm_g_spec(i, 2 * which),
                 _resident((D_MODEL, D_FF), (0, 0)), _resident((D_MODEL, D_FF), (0, 0)),
                 _resident((D_FF, D_MODEL), (0, 0))]
    args += [mod, mod, mod, norm_g4, *w_cur]
    if last:
        in_specs.append(pl.BlockSpec((1, D_MODEL), lambda t: (0, 0)))
        args.append(final_g.reshape(1, D_MODEL))
    out_specs = [row_p, row_s] if last else [row]
    out_shape = ([jax.ShapeDtypeStruct((t_p, D_MODEL), F32), jax.ShapeDtypeStruct((t_tok - t_p, D_MODEL), F32)]
                 if last else [jax.ShapeDtypeStruct((t_tok, D_MODEL), F32)])
    if cast:
        wg_n, wu_n, wd_n, i_n, which_n = w_next
        r_in, r_dn = _cast_rows(D_MODEL, n_steps), _cast_rows(D_FF, n_steps)
        blk_in = lambda t: jnp.minimum(t, D_MODEL // r_in - 1)
        blk_dn = lambda t: jnp.minimum(t, D_FF // r_dn - 1)
        in_specs += [pl.BlockSpec((None, None, r_in, D_FF), lambda t: (i_n, which_n, blk_in(t), 0)),
                     pl.BlockSpec((None, None, r_in, D_FF), lambda t: (i_n, which_n, blk_in(t), 0)),
                     pl.BlockSpec((None, None, r_dn, D_MODEL), lambda t: (i_n, which_n, blk_dn(t), 0))]
        args += [wg_n, wu_n, wd_n]
        out_specs += [pl.BlockSpec((r_in, D_FF), lambda t: (blk_in(t), 0)),
                      pl.BlockSpec((r_in, D_FF), lambda t: (blk_in(t), 0)),
                      pl.BlockSpec((r_dn, D_MODEL), lambda t: (blk_dn(t), 0))]
        out_shape += [jax.ShapeDtypeStruct((D_MODEL, D_FF), BF16), jax.ShapeDtypeStruct((D_MODEL, D_FF), BF16),
                      jax.ShapeDtypeStruct((D_FF, D_MODEL), BF16)]
    scratch = [pltpu.VMEM((tm, D_MODEL), BF16), pltpu.VMEM((tm, D_FF), BF16)]
    if first:
        scratch.insert(0, pltpu.VMEM((tm, D_MODEL), F32))
    outs = pl.pallas_call(
        functools.partial(_ffn_kernel, cfg=cfg, tm=tm, first=first, last=last, cast=cast),
        grid=(n_steps,),
        in_specs=in_specs,
        out_specs=out_specs,
        out_shape=out_shape,
        scratch_shapes=scratch,
        compiler_params=_params(("arbitrary",)),
        name="ffn",
    )(*args)
    n_act = 2 if last else 1
    return outs[:n_act], (tuple(outs[n_act:]) if cast else None)


def _norm_mod(x_ref, sh_ref, sc_ref, g_ref, cfg):
    r = _mod_row(pl.program_id(0), TM_PROJ, cfg)
    hn = _rms(x_ref[...], g_ref[...]) * (1.0 + sc_ref[pl.ds(r, 1), :]) + sh_ref[pl.ds(r, 1), :]
    return hn.astype(BF16)


def _even_in_sub(s, xm_ref, xp_ref, xn_ref, sh_ref, sc_ref, g_ref, w_ref, wp_ref, wdt_ref, cw_ref, cb_ref,
                 z_ref, xst_ref, bc_ref, pool_ref, dt_ref, hn_ref, ext_ref, cfg):
    tm, q = TM_MIX, SSD_CHUNK
    t = pl.program_id(0) * NSUB + s
    info = _tile_info(t, tm, cfg)
    r = _mod_row(t, tm, cfg)
    rows = slice(s * tm, (s + 1) * tm)
    before, after = _sub_halo(s, xm_ref, xp_ref, xn_ref, HALO_A)
    scale, shift, g = 1.0 + sc_ref[pl.ds(r, 1), :], sh_ref[pl.ds(r, 1), :], g_ref[...]
    hn_ref[s, 0:tm, :] = _rms(xm_ref[rows, :], g) * scale + shift
    hn_ref[s, tm:tm + HALO_A, :] = _rms(before, g) * scale + shift
    hn_ref[s, tm + HALO_A:, :] = _rms(after, g) * scale + shift
    hn = hn_ref[s].astype(BF16)
    o_xbc = D_INNER_A
    base = HALO_A - CONV_A // 2
    zero = jnp.zeros((HALO_A, CONV_COLS), F32)
    others = ([(z_ref.at[rows, c:c + CONV_COLS], w_ref, c) for c in range(0, D_INNER_A, CONV_COLS)]
              + [(pool_ref.at[rows, c:c + CONV_COLS], wp_ref, c) for c in range(0, D_POOL, CONV_COLS)])
    for c0 in range(0, D_XBC, CONV_COLS):
        cols = slice(c0, c0 + CONV_COLS)
        xg = _dot(hn, w_ref[:, o_xbc + c0:o_xbc + c0 + CONV_COLS])
        o_ref, ow_ref, o_col = others[c0 // CONV_COLS]
        o_ref[...] = _dot(hn[0:tm], ow_ref[:, o_col:o_col + CONV_COLS])
        ext_ref[s, 0:HALO_A, cols] = jnp.where(info["is_start"], zero, xg[tm:tm + HALO_A])
        ext_ref[s, HALO_A:HALO_A + tm, cols] = xg[0:tm]
        ext_ref[s, HALO_A + tm:, cols] = jnp.where(info["is_end"], zero, xg[tm + HALO_A:])
        for ck in range(tm // q):
            for j in range(c0 // LANES, (c0 + CONV_COLS) // LANES):
                sl = slice(j * LANES, (j + 1) * LANES)
                r0 = base + ck * q
                acc = cw_ref[0:1, sl] * ext_ref[s, r0:r0 + q, sl]
                for k in range(1, CONV_A):
                    acc = acc + cw_ref[k:k + 1, sl] * ext_ref[s, r0 + k:r0 + k + q, sl]
                y = _silu(acc + cb_ref[:, sl])
                if j < D_INNER_A // LANES:
                    xst_ref[s * (tm // q) + ck, j * LANES:(j + 1) * LANES, :] = y.T
                else:
                    bc_ref[s * tm + ck * q:s * tm + (ck + 1) * q,
                           j * LANES - D_INNER_A:(j + 1) * LANES - D_INNER_A] = y.astype(BF16)
    dt_ref[rows, :] = _dot(hn[0:tm], wdt_ref[...])


def _even_in_kernel(*refs, cfg):
    for s in range(NSUB):
        _even_in_sub(s, *refs, cfg)


def _even_in(x, mod, i, j, norm_g4, w, w_pool, w_dt, conv_w, conv_b, cfg):
    t_tok = x.shape[0]
    tm, q = NSUB * TM_MIX, SSD_CHUNK
    row_block = lambda width: pl.BlockSpec((tm, width), lambda t: (t, 0))
    return pl.pallas_call(
        functools.partial(_even_in_kernel, cfg=cfg),
        grid=(t_tok // tm,),
        in_specs=_halo_specs(D_MODEL, tm, HALO_A, t_tok, lambda t: t) + [
            _mod_spec(i, 3), _mod_spec(i, 4), _norm_g_spec(i, 1),
            _layer_block((D_MODEL, D_INNER_A + D_XBC), j), _layer_block((D_MODEL, D_POOL), j),
            _layer_block((D_MODEL, DT_PAD), j),
            _layer_block((CONV_A, D_XBC), j), _layer_block((1, D_XBC), j),
        ],
        out_specs=[row_block(D_INNER_A),
                   pl.BlockSpec((tm // q, D_INNER_A, q), lambda t: (t, 0, 0)),
                   row_block(2 * D_BC), row_block(D_POOL), row_block(DT_PAD)],
        out_shape=[jax.ShapeDtypeStruct((t_tok, D_INNER_A), F32),
                   jax.ShapeDtypeStruct((t_tok // q, D_INNER_A, q), F32),
                   jax.ShapeDtypeStruct((t_tok, 2 * D_BC), BF16),
                   jax.ShapeDtypeStruct((t_tok, D_POOL), F32),
                   jax.ShapeDtypeStruct((t_tok, DT_PAD), F32)],
        scratch_shapes=[pltpu.VMEM((NSUB, TM_MIX + 2 * HALO_A, D_MODEL), F32),
                        pltpu.VMEM((NSUB, TM_MIX + 2 * HALO_A, D_XBC), F32)],
        compiler_params=_params(("parallel",)),
        name="even_in",
    )(x, x, x, mod, mod, norm_g4, w, w_pool, w_dt, conv_w, conv_b)


def _odd_in_kernel(x_ref, sh_ref, sc_ref, g_ref, w_ref, lg_ref, lb_ref, u_ref, v_ref, glu_ref, vg_ref, *, cfg):
    hn = _norm_mod(x_ref, sh_ref, sc_ref, g_ref, cfg)
    for c in range(0, D_C, PROJ_COLS):
        cols = slice(c, c + PROJ_COLS)
        u_ref[:, cols] = jax.nn.gelu(_dot(hn, w_ref[:, c:c + PROJ_COLS]))
        vg_ref[:, cols] = jax.nn.gelu(_dot(hn, w_ref[:, D_C + c:D_C + c + PROJ_COLS]))
        ga = _dot(hn, w_ref[:, 2 * D_C + c:2 * D_C + c + PROJ_COLS])
        gg = _dot(hn, w_ref[:, 2 * D_C + D_D + c:2 * D_C + D_D + c + PROJ_COLS])
        glu_ref[:, cols] = ga * _sigmoid(gg)
    v_ref[...] = _layer_norm(vg_ref[...], lg_ref[...], lb_ref[...]).astype(BF16)


def _odd_in(x, mod, i, j, norm_g4, w, ln_g, ln_b, cfg):
    t_tok = x.shape[0]
    return pl.pallas_call(
        functools.partial(_odd_in_kernel, cfg=cfg),
        grid=(t_tok // TM_PROJ,),
        in_specs=[
            pl.BlockSpec((TM_PROJ, D_MODEL), lambda t: (t, 0)),
            _mod_spec(i, 3), _mod_spec(i, 4),
            _norm_g_spec(i, 1),
            _layer_block((D_MODEL, 2 * D_C + 2 * D_D), j),
            _layer_block((1, D_C), j), _layer_block((1, D_C), j),
        ],
        out_specs=[pl.BlockSpec((TM_PROJ, D_C), lambda t: (t, 0))] * 3,
        out_shape=[jax.ShapeDtypeStruct((t_tok, D_C), F32),
                   jax.ShapeDtypeStruct((t_tok, D_C), BF16),
                   jax.ShapeDtypeStruct((t_tok, D_D), F32)],
        scratch_shapes=[pltpu.VMEM((TM_PROJ, D_C), F32)],
        compiler_params=_params(("parallel",)),
        name="odd_in",
    )(x, mod, mod, norm_g4, w, ln_g, ln_b)


def _dt_terms(dt_ref, dtb_ref, alog_ref, ck, rows):
    q = SSD_CHUNK
    dtt = dt_ref[ck * q:(ck + 1) * q, :].T[rows, :]
    dts = _softplus(dtt + dtb_ref[rows, :])
    return dts, dts * (-jnp.exp(alog_ref[rows, :]))


def _tri(lower_incl):
    ri = lax.broadcasted_iota(jnp.int32, (SSD_CHUNK, SSD_CHUNK), 0)
    ci = lax.broadcasted_iota(jnp.int32, (SSD_CHUNK, SSD_CHUNK), 1)
    return (ri >= ci) if lower_incl else (ri <= ci)


def _cumsum_lanes(v, mask):
    return jnp.dot(v, mask.astype(F32), precision=lax.Precision.HIGHEST, preferred_element_type=F32)


def _group_bc(bc_ref, ck, g):
    rows = slice(ck * SSD_CHUNK, (ck + 1) * SSD_CHUNK)
    b = bc_ref[rows, g * D_STATE:(g + 1) * D_STATE]
    c = bc_ref[rows, D_BC + g * D_STATE:D_BC + (g + 1) * D_STATE]
    return b, c


def _state_update(s_ref, xst_ref, xd_ref, bc_ref, ck, scale, cdec):
    hd = HEAD_DIM_A
    for h in range(N_HEADS_A):
        xd_ref[ck, h * hd:(h + 1) * hd, :] = (xst_ref[ck, h * hd:(h + 1) * hd, :] * scale[h:h + 1, :]).astype(BF16)
    gw = HEADS_PER_GROUP * hd
    for g in range(N_GROUPS_A):
        b, _ = _group_bc(bc_ref, ck, g)
        upd = _dot(xd_ref[ck, g * gw:(g + 1) * gw, :], b)
        for e in range(HEADS_PER_GROUP):
            h = g * HEADS_PER_GROUP + e
            s_ref[h * hd:(h + 1) * hd, :] = (s_ref[h * hd:(h + 1) * hd, :] * cdec[h:h + 1, :]
                                             + upd[e * hd:(e + 1) * hd, :])


def _ssd_fwd_chunk(ck, xst_ref, bc_ref, dt_ref, dtb_ref, alog_ref, dsk_ref, ypt_ref, xd_ref, s_ref):
    q, hd, nh = SSD_CHUNK, HEAD_DIM_A, N_HEADS_A
    dts, dta = _dt_terms(dt_ref, dtb_ref, alog_ref, ck, slice(0, 2 * nh))
    low, upp = _tri(True), _tri(False)
    acs_f = _cumsum_lanes(dta[0:nh], upp)
    rcs_b = _cumsum_lanes(dta[nh:2 * nh], low)
    pad = jnp.zeros((LANES - nh, q), F32)
    src_f = jnp.concatenate([acs_f - jnp.log(dts[0:nh]), pad], axis=0).T
    src_b = jnp.concatenate([rcs_b - jnp.log(dts[nh:2 * nh]), pad], axis=0).T
    e_acs = jnp.exp(acs_f)
    neg_inf = jnp.float32(-jnp.inf)

    gw = HEADS_PER_GROUP * hd
    for g in range(N_GROUPS_A):
        b, c = _group_bc(bc_ref, ck, g)
        sct = _dot_nt(b, c)
        yoff = _dot_nt(s_ref[g * gw:(g + 1) * gw, :].astype(BF16), c)
        for e in range(HEADS_PER_GROUP):
            h = g * HEADS_PER_GROUP + e
            rows = slice(h * hd, (h + 1) * hd)
            seg_f = acs_f[h:h + 1, :] - src_f[:, h:h + 1]
            seg_b = rcs_b[h:h + 1, :] - src_b[:, h:h + 1]
            wt = sct * (jnp.exp(jnp.where(upp, seg_f, neg_inf)) + jnp.exp(jnp.where(low, seg_b, neg_inf)))
            xh = xst_ref[ck, rows, :]
            yd = _dot(xh.astype(BF16), wt.astype(BF16))
            ypt_ref[ck, rows, :] = yd + yoff[e * hd:(e + 1) * hd, :] * e_acs[h:h + 1, :] + dsk_ref[rows, :] * xh

    tot = acs_f[:, q - 1:q]
    scale = dts[0:nh] * jnp.exp(tot - acs_f)
    cdec = jnp.exp(jnp.broadcast_to(tot, (nh, D_STATE)))
    _state_update(s_ref, xst_ref, xd_ref, bc_ref, ck, scale, cdec)


def _scan_init(is_prompt, s0_ref):
    return jnp.where(is_prompt, jnp.zeros(s0_ref.shape, F32), s0_ref[...])


def _ssd_fwd_kernel(xst_ref, bc_ref, dt_ref, dtb_ref, alog_ref, dsk_ref, s0_ref, ypt_ref, sfin_ref, xd_ref, snap_ref,
                    s_ref, *, cfg):
    t = pl.program_id(0)
    cps = TM_MIX // SSD_CHUNK

    @pl.when(t == 0)
    def _():
        s_ref[...] = jnp.zeros_like(s_ref)

    for s in range(NSUB):
        info = _tile_info(t * NSUB + s, TM_MIX, cfg)
        s_ref[...] = jnp.where(info["is_start"], _scan_init(info["is_prompt"], s0_ref), s_ref[...])
        for ck in range(s * cps, (s + 1) * cps):
            _ssd_fwd_chunk(ck, xst_ref, bc_ref, dt_ref, dtb_ref, alog_ref, dsk_ref, ypt_ref, xd_ref, s_ref)
        snap_ref[s] = s_ref[...]

    @pl.when(t * NSUB < cfg.batch * (cfg.seq // TM_MIX))
    def _():
        sfin_ref[...] = snap_ref[...]


def _ssd_bwd_chunk(ck, dt_ref, z_ref, ypt_ref, xst_ref, bc_ref, dtb_ref, alog_ref, yt_ref, y_ref, xd_ref, s_ref):
    q, hd, nh = SSD_CHUNK, HEAD_DIM_A, N_HEADS_A
    dts, dta = _dt_terms(dt_ref, dtb_ref, alog_ref, ck, slice(nh, 2 * nh))
    rcs_b = _cumsum_lanes(dta, _tri(True))
    e_rcs = jnp.exp(rcs_b)

    gw = HEADS_PER_GROUP * hd
    for g in range(N_GROUPS_A):
        _, c = _group_bc(bc_ref, ck, g)
        yoff = _dot_nt(s_ref[g * gw:(g + 1) * gw, :].astype(BF16), c)
        for e in range(HEADS_PER_GROUP):
            h = g * HEADS_PER_GROUP + e
            yt_ref[ck, h * hd:(h + 1) * hd, :] = (ypt_ref[ck, h * hd:(h + 1) * hd, :]
                                                  + yoff[e * hd:(e + 1) * hd, :] * e_rcs[h:h + 1, :])

    rows = slice(ck * q, (ck + 1) * q)
    for j in range(D_INNER_A // LANES):
        sl = slice(j * LANES, (j + 1) * LANES)
        y_ref[rows, sl] = yt_ref[ck, j * LANES:(j + 1) * LANES, :].T * _silu(z_ref[rows, sl])

    tot = rcs_b[:, 0:1]
    scale = dts * jnp.exp(tot - rcs_b)
    cdec = jnp.exp(jnp.broadcast_to(tot, (nh, D_STATE)))
    _state_update(s_ref, xst_ref, xd_ref, bc_ref, ck, scale, cdec)


def _ssd_bwd_kernel(dt_ref, z_ref, ypt_ref, xst_ref, bc_ref, dtb_ref, alog_ref, ng_ref, s0_ref, sf_ref, *rest,
                    cfg, n_steps, layer):
    if layer == 0:
        ya_ref, so_ref, xd_ref, yt_ref, y_ref, snap_ref, s_ref = rest
    else:
        sprev_ref, ya_ref, so_ref, xd_ref, yt_ref, y_ref, snap_ref, s_ref = rest
    t = n_steps - 1 - pl.program_id(0)
    cps = TM_MIX // SSD_CHUNK

    @pl.when(pl.program_id(0) == 0)
    def _():
        s_ref[...] = jnp.zeros_like(s_ref)

    for s in reversed(range(NSUB)):
        info = _tile_info(t * NSUB + s, TM_MIX, cfg)
        s_ref[...] = jnp.where(info["is_end"], _scan_init(info["is_prompt"], s0_ref), s_ref[...])
        for ck in reversed(range(s * cps, (s + 1) * cps)):
            _ssd_bwd_chunk(ck, dt_ref, z_ref, ypt_ref, xst_ref, bc_ref, dtb_ref, alog_ref, yt_ref, y_ref, xd_ref, s_ref)
        snap_ref[s] = s_ref[...]
    ya_ref[...] = _rms(y_ref[...], ng_ref[...]).astype(BF16)

    @pl.when(t * NSUB < cfg.batch * (cfg.seq // TM_MIX))
    def _():
        if layer == 0:
            so_ref[:, 0] = sf_ref[...]
            so_ref[:, 1] = snap_ref[...]
        else:
            so_ref[:, 0] = sprev_ref[...]
            so_ref[:, 1, 0] = sf_ref[...]
            so_ref[:, 1, 1] = snap_ref[...]


def _halo_specs(width, tile, halo, n_rows, chunk_of):
    per = tile // halo
    last = n_rows // halo - 1
    return [
        pl.BlockSpec((tile, width), lambda i: (chunk_of(i), 0)),
        pl.BlockSpec((halo, width), lambda i: (jnp.maximum(chunk_of(i) * per - 1, 0), 0)),
        pl.BlockSpec((halo, width), lambda i: (jnp.minimum((chunk_of(i) + 1) * per, last), 0)),
    ]


def _ssd(xst, bc, dt, z, dtb, alog, dsk_t, norm_g, s0, s_prev, layer, cfg):
    t_tok = bc.shape[0]
    q, tm = SSD_CHUNK, NSUB * TM_MIX
    cpt = tm // q
    n_steps = t_tok // tm
    hp = N_HEADS_A * HEAD_DIM_A
    nst = cfg.dec_seq // tm
    n_pt = (cfg.batch * cfg.seq) // tm

    def s_seq(t):
        return jnp.clip(lax.div(jnp.maximum(t - n_pt, 0), nst), 0, cfg.dec_batch - 1)

    def p_blk(t):
        return jnp.minimum(t, n_pt - 1)

    small = lambda shape: pl.BlockSpec(shape, lambda i: (0,) * len(shape))
    head_rows = _layer_block((2 * N_HEADS_A, q), layer)
    state_scratch = [pltpu.VMEM((NSUB, hp, D_STATE), F32), pltpu.VMEM((hp, D_STATE), F32)]

    ypt, sf = pl.pallas_call(
        functools.partial(_ssd_fwd_kernel, cfg=cfg),
        grid=(n_steps,),
        in_specs=[
            pl.BlockSpec((cpt, hp, q), lambda i: (i, 0, 0)),
            pl.BlockSpec((tm, 2 * D_BC), lambda i: (i, 0)),
            pl.BlockSpec((tm, DT_PAD), lambda i: (i, 0)),
            head_rows, head_rows, _layer_block((hp, q), layer),
            pl.BlockSpec((None, None, None, hp, D_STATE), lambda i: (s_seq(i), layer, 0, 0, 0)),
        ],
        out_specs=[pl.BlockSpec((cpt, hp, q), lambda i: (i, 0, 0)),
                   pl.BlockSpec((NSUB, hp, D_STATE), lambda i: (p_blk(i), 0, 0))],
        out_shape=[jax.ShapeDtypeStruct((t_tok // q, hp, q), F32),
                   jax.ShapeDtypeStruct((cfg.batch, hp, D_STATE), F32)],
        scratch_shapes=[pltpu.VMEM((cpt, hp, q), BF16)] + state_scratch,
        compiler_params=_params(("arbitrary",)),
        name="ssd_fwd",
    )(xst, bc, dt, dtb, alog, dsk_t, s0)

    rev = lambda i: n_steps - 1 - i
    in_specs = [
        pl.BlockSpec((tm, DT_PAD), lambda i: (rev(i), 0)),
        pl.BlockSpec((tm, D_INNER_A), lambda i: (rev(i), 0)),
        pl.BlockSpec((cpt, hp, q), lambda i: (rev(i), 0, 0)),
        pl.BlockSpec((cpt, hp, q), lambda i: (rev(i), 0, 0)),
        pl.BlockSpec((tm, 2 * D_BC), lambda i: (rev(i), 0)),
        head_rows, head_rows, _layer_block((1, D_INNER_A), layer),
        pl.BlockSpec((None, None, None, hp, D_STATE), lambda i: (s_seq(rev(i)), layer, 1, 0, 0)),
        pl.BlockSpec((NSUB, hp, D_STATE), lambda i: (p_blk(rev(i)), 0, 0)),
    ]
    args = [dt, z, ypt, xst, bc, dtb, alog, norm_g, s0, sf]
    if layer == 0:
        so_block, so_shape = (NSUB, 2, hp, D_STATE), (cfg.batch, 2, hp, D_STATE)
        so_index = lambda i: (p_blk(rev(i)), 0, 0, 0)
    else:
        in_specs.append(pl.BlockSpec((NSUB, 2, hp, D_STATE), lambda i: (p_blk(rev(i)), 0, 0, 0)))
        args.append(s_prev)
        so_block, so_shape = (NSUB, 2, 2, hp, D_STATE), (cfg.batch, 2, 2, hp, D_STATE)
        so_index = lambda i: (p_blk(rev(i)), 0, 0, 0, 0)
    ya, so = pl.pallas_call(
        functools.partial(_ssd_bwd_kernel, cfg=cfg, n_steps=n_steps, layer=layer),
        grid=(n_steps,),
        in_specs=in_specs,
        out_specs=[pl.BlockSpec((tm, D_INNER_A), lambda i: (rev(i), 0)), pl.BlockSpec(so_block, so_index)],
        out_shape=[jax.ShapeDtypeStruct((t_tok, D_INNER_A), BF16), jax.ShapeDtypeStruct(so_shape, F32)],
        scratch_shapes=[pltpu.VMEM((cpt, hp, q), BF16), pltpu.VMEM((cpt, hp, q), F32),
                        pltpu.VMEM((tm, D_INNER_A), F32)] + state_scratch,
        compiler_params=_params(("arbitrary",)),
        name="ssd_bwd",
    )(*args)
    return ya, so


def _sub_halo(s, main_ref, prev_ref, next_ref, halo):
    tm = TM_MIX
    before = prev_ref[...] if s == 0 else main_ref[s * tm - halo:s * tm, :]
    after = next_ref[...] if s == NSUB - 1 else main_ref[(s + 1) * tm:(s + 1) * tm + halo, :]
    return before, after


def _even_out_sub(s, ya_ref, pm_ref, pp_ref, pn_ref, x_ref, gt_ref, pw_ref, ps_ref, woa_ref, wop_ref,
                  o_ref, ext_ref, yp_ref, cfg):
    tm = TM_MIX
    t = pl.program_id(0) * NSUB + s
    info = _tile_info(t, tm, cfg)
    r = _mod_row(t, tm, cfg)
    rows = slice(s * tm, (s + 1) * tm)
    before, after = _sub_halo(s, pm_ref, pp_ref, pn_ref, HALO_A)
    zero = jnp.zeros((HALO_A, D_POOL), F32)
    ext_ref[s, 0:HALO_A, :] = jnp.where(info["is_start"], zero, before)
    ext_ref[s, HALO_A:HALO_A + tm, :] = pm_ref[rows, :]
    ext_ref[s, HALO_A + tm:, :] = jnp.where(info["is_end"], zero, after)
    pos = info["pos"] * tm + lax.broadcasted_iota(jnp.int32, (tm, POOL_GROUP_DIM), 0)
    oc = D_MODEL // len(POOL_WINDOWS)
    for gi, win in enumerate(POOL_WINDOWS):
        o_ref[rows, gi * oc:(gi + 1) * oc] = _dot(ya_ref[rows, :], woa_ref[:, gi * oc:(gi + 1) * oc])
        sl = slice(gi * POOL_GROUP_DIM, (gi + 1) * POOL_GROUP_DIM)
        lo = HALO_A - win // 2
        acc = ext_ref[s, lo:lo + tm, sl]
        for j in range(1, win):
            acc = acc + ext_ref[s, lo + j:lo + j + tm, sl]
        cnt = (jnp.minimum(pos - win // 2 + win, info["seq_len"]) - jnp.maximum(pos - win // 2, 0)).astype(F32)
        pooled = acc / cnt - ext_ref[s, HALO_A:HALO_A + tm, sl]
        yp_ref[rows, sl] = (_dot(pooled.astype(BF16), pw_ref[gi]) * ps_ref[:, sl]).astype(BF16)
    out = o_ref[rows, :] + _dot(yp_ref[rows, :], wop_ref[...])
    o_ref[rows, :] = x_ref[rows, :] + gt_ref[pl.ds(r, 1), :] * out


def _even_out_kernel(*refs, cfg):
    for s in range(NSUB):
        _even_out_sub(s, *refs, cfg)


def _even_out(ya, pool_in, x, mod, i, j, pool_w, pool_scale, w_out, cfg):
    t_tok = x.shape[0]
    tm = NSUB * TM_MIX
    small = lambda shape: pl.BlockSpec(shape, lambda t: (0,) * len(shape))
    return pl.pallas_call(
        functools.partial(_even_out_kernel, cfg=cfg),
        grid=(t_tok // tm,),
        in_specs=[pl.BlockSpec((tm, D_INNER_A), lambda t: (t, 0))]
        + _halo_specs(D_POOL, tm, HALO_A, t_tok, lambda t: t) + [
            pl.BlockSpec((tm, D_MODEL), lambda t: (t, 0)),
            _mod_spec(i, 5),
            _layer_block((len(POOL_WINDOWS), POOL_GROUP_DIM, POOL_GROUP_DIM), j), _layer_block((1, D_POOL), j),
            _layer_block((D_INNER_A, D_MODEL), j), _layer_block((D_POOL, D_MODEL), j, D_INNER_A // D_POOL),
        ],
        out_specs=pl.BlockSpec((tm, D_MODEL), lambda t: (t, 0)),
        out_shape=jax.ShapeDtypeStruct((t_tok, D_MODEL), F32),
        scratch_shapes=[pltpu.VMEM((NSUB, TM_MIX + 2 * HALO_A, D_POOL), F32), pltpu.VMEM((tm, D_POOL), BF16)],
        compiler_params=_params(("parallel",)),
        name="even_out",
    )(ya, pool_in, pool_in, pool_in, x, mod, pool_w, pool_scale, w_out, w_out)


def _odd_out_sub(s, u_ref, v_ref, gm_ref, gp_ref, gn_ref, x_ref, gt_ref, spw_ref, spb_ref, dww_ref, dwb_ref,
                 cg_ref, cb_ref, woc_ref, wod_ref, o_ref, ext_ref, sh_ref, yc_ref, yd_ref, cfg):
    tm = TM_MIX
    t = pl.program_id(0) * NSUB + s
    info = _tile_info(t, tm, cfg)
    r = _mod_row(t, tm, cfg)
    row0 = s * tm
    tile_rows = slice(row0, row0 + tm)
    hc = D_C // N_HEADS_C
    for ck in range(tm // MLP_CHUNK):
        rows = slice(row0 + ck * MLP_CHUNK, row0 + (ck + 1) * MLP_CHUNK)
        for h in range(N_HEADS_C):
            sl = slice(h * hc, (h + 1) * hc)
            sv = _dot(spw_ref[h], v_ref[rows, sl]) + spb_ref[:, sl]
            yc_ref[rows, sl] = (u_ref[rows, sl] * sv).astype(BF16)
    before, after = _sub_halo(s, gm_ref, gp_ref, gn_ref, HALO_D)
    zero = jnp.zeros((HALO_D, D_D), F32)
    ext_ref[s, 0:HALO_D, :] = jnp.where(info["is_start"], zero, before)
    ext_ref[s, HALO_D:HALO_D + tm, :] = gm_ref[tile_rows, :]
    ext_ref[s, HALO_D + tm:, :] = jnp.where(info["is_end"], zero, after)
    span = tm + 2 * HALO_D - SUBLANES
    base = HALO_D - CONV_D // 2
    rb = MLP_CHUNK
    for j in range(D_D // LANES):
        sl = slice(j * LANES, (j + 1) * LANES)
        if (j * LANES) % PROJ_COLS == 0:
            oc = slice(j * LANES, j * LANES + PROJ_COLS)
            o_ref[tile_rows, oc] = _dot(yc_ref[tile_rows, :], woc_ref[:, oc])
        for sh in range(1, SUBLANES):
            sh_ref[sh - 1, :, sl] = ext_ref[s, sh:sh + span, sl]
        for rr in range(tm // rb):
            acc = None
            for k in range(CONV_D):
                q8, sh = divmod(base + k, SUBLANES)
                r0 = q8 * SUBLANES + rr * rb
                tap = ext_ref[s, r0:r0 + rb, sl] if sh == 0 else sh_ref[sh - 1, r0:r0 + rb, sl]
                term = dww_ref[k:k + 1, sl] * tap
                acc = term if acc is None else acc + term
            yd_ref[row0 + rr * rb:row0 + (rr + 1) * rb, sl] = acc + dwb_ref[:, sl]
    ydn = _silu(_layer_norm(yd_ref[tile_rows, :], cg_ref[...], cb_ref[...])).astype(BF16)
    out = o_ref[tile_rows, :] + _dot(ydn, wod_ref[...])
    o_ref[tile_rows, :] = x_ref[tile_rows, :] + gt_ref[pl.ds(r, 1), :] * out


def _odd_out_kernel(*refs, cfg):
    for s in range(NSUB):
        _odd_out_sub(s, *refs, cfg)


def _odd_out(u, v, glu, x, mod, i, j, sp_w, sp_b_full, dw_w, dw_b, cn_g, cn_b, w_out, cfg):
    t_tok = x.shape[0]
    tm = NSUB * TM_MIX
    small = lambda shape: pl.BlockSpec(shape, lambda t: (0,) * len(shape))
    return pl.pallas_call(
        functools.partial(_odd_out_kernel, cfg=cfg),
        grid=(t_tok // tm,),
        in_specs=[pl.BlockSpec((tm, D_C), lambda t: (t, 0)), pl.BlockSpec((tm, D_C), lambda t: (t, 0))]
        + _halo_specs(D_D, tm, HALO_D, t_tok, lambda t: t) + [
            pl.BlockSpec((tm, D_MODEL), lambda t: (t, 0)),
            _mod_spec(i, 5),
            _layer_block((N_HEADS_C, MLP_CHUNK, MLP_CHUNK), j), _layer_block((MLP_CHUNK, D_C), j),
            _layer_block((CONV_D, D_D), j), _layer_block((1, D_D), j), _layer_block((1, D_D), j),
            _layer_block((1, D_D), j),
            _layer_block((D_C, D_MODEL), j), _layer_block((D_D, D_MODEL), j, 1),
        ],
        out_specs=pl.BlockSpec((tm, D_MODEL), lambda t: (t, 0)),
        out_shape=jax.ShapeDtypeStruct((t_tok, D_MODEL), F32),
        scratch_shapes=[pltpu.VMEM((NSUB, TM_MIX + 2 * HALO_D, D_D), F32),
                        pltpu.VMEM((SUBLANES - 1, TM_MIX + 2 * HALO_D - SUBLANES, D_D), F32),
                        pltpu.VMEM((tm, D_C), BF16), pltpu.VMEM((tm, D_D), F32)],
        compiler_params=_params(("parallel",)),
        name="odd_out",
    )(u, v, glu, glu, glu, x, mod, sp_w, sp_b_full, dw_w, dw_b, cn_g, cn_b, w_out, w_out)


def _position_tables(dec_seq):
    quarter = D_MODEL // 4
    freqs = jnp.exp(-math.log(POS_BASE) * jnp.arange(quarter, dtype=F32) / quarter)
    ang_r = jnp.arange(dec_seq // GRID_W, dtype=F32)[:, None] * freqs
    ang_c = jnp.arange(GRID_W, dtype=F32)[:, None] * freqs
    return (jnp.concatenate([jnp.sin(ang_r), jnp.cos(ang_r)], axis=-1),
            jnp.concatenate([jnp.sin(ang_c), jnp.cos(ang_c)], axis=-1))


def _forward(cfg, x_prompt, x_sample, c, state_ssd, c_ctx, w_mod, b_mod, norm_g, ffn_w_gate, ffn_w_up,
             ffn_w_down, ev_w_in, ev_conv_w, ev_conv_b, ev_dt_bias, ev_a_log, ev_d_skip, ev_ssd_norm_g,
             ev_pool_w, ev_pool_scale, ev_w_out, od_w_in, od_v_ln_g, od_v_ln_b, od_sp_w, od_sp_b, od_dw_w,
             od_dw_b, od_cn_g, od_cn_b, od_w_out, final_norm_g):
    assert DEPTH == 4 and cfg.seq % TM_MIX == 0 and cfg.dec_seq % TM_FFN == 0 and (cfg.batch * cfg.seq) % TM_FFN == 0
    t_p = cfg.batch * cfg.seq
    hp = N_HEADS_A * HEAD_DIM_A
    row_tab, col_tab = _position_tables(cfg.dec_seq)

    cond8 = jnp.concatenate([c_ctx[None, :], c, jnp.zeros((SUBLANES - 1 - cfg.dec_batch, D_MODEL), F32)], axis=0)
    mod = _modulation(cond8, w_mod, b_mod)

    w_ffn = tuple(w[0, 0].astype(BF16) for w in (ffn_w_gate, ffn_w_up, ffn_w_down))
    f32_ffn = (ffn_w_gate, ffn_w_up, ffn_w_down)
    norm_g4 = norm_g.reshape(DEPTH, 3, 1, D_MODEL)
    n_odd = od_w_in.shape[0]
    lane_b = lambda v: jnp.broadcast_to(v.reshape(N_EVEN, 2 * N_HEADS_A, 1), (N_EVEN, 2 * N_HEADS_A, SSD_CHUNK))
    w_in = ev_w_in.astype(BF16)
    o_dt = D_INNER_A + D_XBC
    o_pool = o_dt + 2 * N_HEADS_A
    ev = dict(
        w_in=w_in, w_pool=w_in[:, :, o_pool:],
        w_dt=jnp.pad(w_in[:, :, o_dt:o_pool], ((0, 0), (0, 0), (0, DT_PAD - 2 * N_HEADS_A))),
        conv_b=ev_conv_b.reshape(N_EVEN, 1, D_XBC),
        dt_bias=lane_b(ev_dt_bias), a_log=lane_b(ev_a_log),
        d_skip=jnp.broadcast_to(ev_d_skip[:, :, None, None],
                                (N_EVEN, N_HEADS_A, HEAD_DIM_A, SSD_CHUNK)).reshape(N_EVEN, hp, SSD_CHUNK),
        norm_g=ev_ssd_norm_g.reshape(N_EVEN, 1, D_INNER_A),
        pool_w=ev_pool_w.astype(BF16), pool_scale=ev_pool_scale.reshape(N_EVEN, 1, D_POOL),
        w_out=ev_w_out.astype(BF16),
    )
    od = dict(
        w_in=od_w_in.astype(BF16), ln_g=od_v_ln_g.reshape(n_odd, 1, D_C), ln_b=od_v_ln_b.reshape(n_odd, 1, D_C),
        sp_w=od_sp_w.astype(BF16),
        sp_b=jnp.broadcast_to(jnp.swapaxes(od_sp_b, 1, 2)[:, :, :, None],
                              (n_odd, MLP_CHUNK, N_HEADS_C, D_C // N_HEADS_C)).reshape(n_odd, MLP_CHUNK, D_C),
        dw_b=od_dw_b.reshape(n_odd, 1, D_D), cn_g=od_cn_g.reshape(n_odd, 1, D_D), cn_b=od_cn_b.reshape(n_odd, 1, D_D),
        w_out=od_w_out.astype(BF16),
    )
    s0 = state_ssd.reshape(cfg.dec_batch, N_EVEN, 2, hp, D_STATE)
    states = None
    for i in range(DEPTH):
        j = i // 2
        if i == 0:
            (x,), w_ffn = _ffn((x_prompt.reshape(t_p, D_MODEL), x_sample.reshape(-1, D_MODEL)), mod, i, 0, norm_g4,
                               w_ffn, f32_ffn + (i, 1), cfg, pos_tabs=(row_tab, col_tab))
        else:
            (x,), w_ffn = _ffn(x, mod, i, 0, norm_g4, w_ffn, f32_ffn + (i, 1), cfg)
        if i % 2 == 0:
            z, xst, bc, pool_in, dt = _even_in(x, mod, i, j, norm_g4, ev["w_in"], ev["w_pool"], ev["w_dt"], ev_conv_w,
                                               ev["conv_b"], cfg)
            ya, states = _ssd(xst, bc, dt, z, ev["dt_bias"], ev["a_log"], ev["d_skip"], ev["norm_g"], s0, states, j, cfg)
            x = _even_out(ya, pool_in, x, mod, i, j, ev["pool_w"], ev["pool_scale"], ev["w_out"], cfg)
        else:
            u, v, glu = _odd_in(x, mod, i, j, norm_g4, od["w_in"], od["ln_g"], od["ln_b"], cfg)
            x = _odd_out(u, v, glu, x, mod, i, j, od["sp_w"], od["sp_b"], od_dw_w, od["dw_b"], od["cn_g"], od["cn_b"],
                         od["w_out"], cfg)
        if i < DEPTH - 1:
            (x,), w_ffn = _ffn(x, mod, i, 1, norm_g4, w_ffn, f32_ffn + (i + 1, 0), cfg)
        else:
            (y_p, y_s), _ = _ffn(x, mod, i, 1, norm_g4, w_ffn, None, cfg, final_g=final_norm_g)

    return (y_p.reshape(cfg.batch, cfg.seq, D_MODEL), y_s.reshape(cfg.dec_batch, cfg.dec_seq, D_MODEL),
            states.reshape(cfg.batch, N_EVEN, 2, N_HEADS_A, HEAD_DIM_A, D_STATE))


def kernel(x_prompt, x_sample, c, state_ssd, c_ctx, w_mod, b_mod, norm_g, ffn_w_gate, ffn_w_up, ffn_w_down, ev_w_in, ev_conv_w, ev_conv_b, ev_dt_bias, ev_a_log, ev_d_skip, ev_ssd_norm_g, ev_pool_w, ev_pool_scale, ev_w_out, od_w_in, od_v_ln_g, od_v_ln_b, od_sp_w, od_sp_b, od_dw_w, od_dw_b, od_cn_g, od_cn_b, od_w_out, final_norm_g):
    cfg = Cfg(x_prompt.shape[0], x_prompt.shape[1], x_sample.shape[0], x_sample.shape[1])
    return _forward(cfg, x_prompt, x_sample, c, state_ssd, c_ctx, w_mod, b_mod, norm_g, ffn_w_gate, ffn_w_up,
                    ffn_w_down, ev_w_in, ev_conv_w, ev_conv_b, ev_dt_bias, ev_a_log, ev_d_skip, ev_ssd_norm_g,
                    ev_pool_w, ev_pool_scale, ev_w_out, od_w_in, od_v_ln_g, od_v_ln_b, od_sp_w, od_sp_b, od_dw_w,
                    od_dw_b, od_cn_g, od_cn_b, od_w_out, final_norm_g)
```

```python
import collections
import functools
import math

import jax
import jax.numpy as jnp
from jax import lax
from jax.experimental import pallas as pl
from jax.experimental.pallas import tpu as pltpu

F32 = jnp.float32
BF16 = jnp.bfloat16

D_MODEL = 1024
DEPTH = 4
GRID_W = 64
POS_BASE = 10000.0
EPS = 1e-6
N_MOD = 9
D_FF = 2816
N_HEADS_A = 16
HEAD_DIM_A = 64
D_INNER_A = N_HEADS_A * HEAD_DIM_A
N_GROUPS_A = 2
HEADS_PER_GROUP = N_HEADS_A // N_GROUPS_A
D_STATE = 128
CONV_A = 5
SSD_CHUNK = 128
D_BC = N_GROUPS_A * D_STATE
D_XBC = D_INNER_A + 2 * D_BC
POOL_WINDOWS = (2, 4, 8, 16)
POOL_GROUP_DIM = 128
D_POOL = len(POOL_WINDOWS) * POOL_GROUP_DIM
N_HEADS_C = 8
MLP_CHUNK = 128
D_C = 1024
D_D = 1024
CONV_D = 31
N_EVEN = (DEPTH + 1) // 2

LANES = 128
SUBLANES = 8
DT_PAD = LANES
HALO_A = SUBLANES
HALO_D = 2 * SUBLANES
VMEM_LIMIT = 56 * 1024 * 1024

TM_FFN = 1024
TM_FFN_FIRST = 512
TF_FFN = 256
TM_PROJ = 512
TM_MIX = 256
CONV_COLS = 256
PROJ_COLS = 256
NSUB = 2
Cfg = collections.namedtuple("Cfg", "batch seq dec_batch dec_seq")


def _n_tokens(cfg):
    return cfg.batch * cfg.seq + cfg.dec_batch * cfg.dec_seq


def _tile_info(t, tm, cfg):
    npt = cfg.seq // tm
    nst = cfg.dec_seq // tm
    n_p = cfg.batch * npt
    is_prompt = t < n_p
    ts = jnp.maximum(t - n_p, 0)
    pos = jnp.where(is_prompt, lax.rem(t, npt), lax.rem(ts, nst))
    last = jnp.where(is_prompt, npt - 1, nst - 1)
    info = dict(
        is_prompt=is_prompt,
        is_start=pos == 0,
        is_end=pos == last,
        pos=pos,
        seq_len=jnp.where(is_prompt, cfg.seq, cfg.dec_seq),
    )
    return info


def _mod_row(t, tm, cfg):
    n_p = (cfg.batch * cfg.seq) // tm
    per = cfg.dec_seq // tm
    return jnp.where(t < n_p, 0, 1 + lax.div(jnp.maximum(t - n_p, 0), per))


def _sigmoid(x):
    return 0.5 + 0.5 * jnp.tanh(0.5 * x)


def _silu(x):
    h = 0.5 * x
    return h + h * jnp.tanh(h)


def _softplus(x):
    return jnp.maximum(x, 0.0) + jnp.log1p(jnp.exp(-jnp.abs(x)))


def _rms(x, g):
    return x * lax.rsqrt(jnp.mean(x * x, axis=-1, keepdims=True) + EPS) * g


def _layer_norm(x, g, b):
    xc = x - jnp.mean(x, axis=-1, keepdims=True)
    y = xc * lax.rsqrt(jnp.mean(xc * xc, axis=-1, keepdims=True) + EPS)
    return y * g + b


def _dot(a, b):
    return jnp.dot(a, b, preferred_element_type=F32)


def _dot_nt(a, b):
    return lax.dot_general(a, b, (((1,), (1,)), ((), ())), preferred_element_type=F32)


def _params(sem, n_in=0, cast_inputs=()):
    fusion = [k in cast_inputs for k in range(n_in)] if cast_inputs else None
    return pltpu.CompilerParams(dimension_semantics=sem, vmem_limit_bytes=VMEM_LIMIT, allow_input_fusion=fusion)


def _mod_kernel(c_ref, w_ref, b_ref, o_ref):
    c = c_ref[...]
    sc = _silu(c).astype(BF16)
    o_ref[...] = _dot(sc, w_ref[...].astype(BF16)) + b_ref[...]


def _modulation(cond8, w_mod, b_mod):
    b4 = b_mod.reshape(DEPTH, N_MOD, 1, D_MODEL)
    return pl.pallas_call(
        _mod_kernel,
        grid=(DEPTH, N_MOD),
        in_specs=[
            pl.BlockSpec((SUBLANES, D_MODEL), lambda i, k: (0, 0)),
            pl.BlockSpec((None, D_MODEL, D_MODEL), lambda i, k: (i, 0, k)),
            pl.BlockSpec((None, None, 1, D_MODEL), lambda i, k: (i, k, 0, 0)),
        ],
        out_specs=pl.BlockSpec((None, None, SUBLANES, D_MODEL), lambda i, k: (i, k, 0, 0)),
        out_shape=jax.ShapeDtypeStruct((DEPTH, N_MOD, SUBLANES, D_MODEL), F32),
        compiler_params=_params(("arbitrary", "arbitrary")),
        name="modulation",
    )(cond8, w_mod, b4)


def _mod_spec(i, k):
    return pl.BlockSpec((None, None, SUBLANES, D_MODEL), lambda t: (i, k, 0, 0))


def _layer_block(shape, j, row=0):
    return pl.BlockSpec((None,) + tuple(shape), lambda t: (j, row) + (0,) * (len(shape) - 1))


def _norm_g_spec(i, k):
    return pl.BlockSpec((None, None, 1, D_MODEL), lambda t: (i, k, 0, 0))


def _ffn_body(x, r, sh_ref, sc_ref, gt_ref, g_ref, wg_ref, wu_ref, wd_ref, hn_ref, act_ref):
    hn = _rms(x, g_ref[...]) * (1.0 + sc_ref[pl.ds(r, 1), :]) + sh_ref[pl.ds(r, 1), :]
    hn_ref[...] = hn.astype(BF16)
    for f in range(D_FF // TF_FFN):
        sl = slice(f * TF_FFN, (f + 1) * TF_FFN)
        g = _dot(hn_ref[...], wg_ref[:, sl])
        u = _dot(hn_ref[...], wu_ref[:, sl])
        act_ref[:, sl] = (_silu(g) * u).astype(BF16)
    return x + (0.5 * gt_ref[pl.ds(r, 1), :]) * _dot(act_ref[...], wd_ref[...])


def _cast_rows(total, n_steps):
    rows = 2 * SUBLANES
    while rows * n_steps < total or total % rows:
        rows += 2 * SUBLANES
    return rows


def _cast_next(wgn_ref, wun_ref, wdn_ref, wgo_ref, wuo_ref, wdo_ref):
    wgo_ref[...] = wgn_ref[...].astype(BF16)
    wuo_ref[...] = wun_ref[...].astype(BF16)
    wdo_ref[...] = wdn_ref[...].astype(BF16)


def _ffn_kernel(*refs, cfg, tm, first, last, cast):
    refs = list(refs)
    t = pl.program_id(0)
    n_p = (cfg.batch * cfg.seq) // tm
    is_prompt = t < n_p
    if first:
        xp_ref, xs_ref, rt_ref, ct_ref = refs[:4]
        del refs[:4]
    else:
        x_in_ref = refs.pop(0)
    sh_ref, sc_ref, gt_ref, g_ref, wg_ref, wu_ref, wd_ref = refs[:7]
    del refs[:7]
    if last:
        fg_ref = refs.pop(0)
    if cast:
        nxt_in = refs[:3]
        del refs[:3]
    outs = refs[:2 if last else 1]
    del refs[:len(outs)]
    if cast:
        nxt_out = refs[:3]
        del refs[:3]
    if first:
        x_ref = refs.pop(0)
    hn_ref, act_ref = refs

    if first:
        @pl.when(is_prompt)
        def _():
            x_ref[...] = xp_ref[...]

        @pl.when(jnp.logical_not(is_prompt))
        def _():
            half = D_MODEL // 2
            row0 = lax.rem(jnp.maximum(t - n_p, 0), cfg.dec_seq // tm) * (tm // GRID_W)
            for rr in range(tm // GRID_W):
                rows = slice(rr * GRID_W, (rr + 1) * GRID_W)
                x_ref[rows, 0:half] = xs_ref[rows, 0:half] + rt_ref[pl.ds(row0 + rr, 1), :]
                x_ref[rows, half:D_MODEL] = xs_ref[rows, half:D_MODEL] + ct_ref[...]
        x = x_ref[...]
    else:
        x = x_in_ref[...]

    r = _mod_row(t, tm, cfg)
    xn = _ffn_body(x, r, sh_ref, sc_ref, gt_ref, g_ref, wg_ref, wu_ref, wd_ref, hn_ref, act_ref)
    if last:
        y = _rms(xn, fg_ref[...])

        @pl.when(is_prompt)
        def _():
            outs[0][...] = y

        @pl.when(jnp.logical_not(is_prompt))
        def _():
            outs[1][...] = y
    else:
        outs[0][...] = xn
    if cast:
        _cast_next(*nxt_in, *nxt_out)


def _resident(shape, index):
    return pl.BlockSpec(shape, lambda t: index, pipeline_mode=pl.Buffered(1))


def _ffn(x, mod, i, which, norm_g4, w_cur, w_next, cfg, pos_tabs=None, final_g=None):
    first, last, cast = pos_tabs is not None, final_g is not None, w_next is not None
    tm = TM_FFN_FIRST if first else TM_FFN
    t_p = cfg.batch * cfg.seq
    n_p = t_p // tm
    t_tok = _n_tokens(cfg)
    n_steps = t_tok // tm
    k0 = 6 * which
    row = pl.BlockSpec((tm, D_MODEL), lambda t: (t, 0))
    row_p = pl.BlockSpec((tm, D_MODEL), lambda t: (jnp.minimum(t, n_p - 1), 0))
    row_s = pl.BlockSpec((tm, D_MODEL), lambda t: (jnp.maximum(t - n_p, 0), 0))
    half = D_MODEL // 2

    in_specs, args = [], []
    if first:
        in_specs += [row_p, row_s, pl.BlockSpec((cfg.dec_seq // GRID_W, half), lambda t: (0, 0)),
                     pl.BlockSpec((GRID_W, half), lambda t: (0, 0))]
        args += [x[0], x[1], pos_tabs[0], pos_tabs[1]]
    else:
        in_specs.append(row)
        args.append(x)
    in_specs += [_mod_spec(i, k0), _mod_spec(i, k0 + 1), _mod_spec(i, k0 + 2),
                 _norm_g_spec(i, 2 * which),
                 _resident((D_MODEL, D_FF), (0, 0)), _resident((D_MODEL, D_FF), (0, 0)),
                 _resident((D_FF, D_MODEL), (0, 0))]
    args += [mod, mod, mod, norm_g4, *w_cur]
    if last:
        in_specs.append(pl.BlockSpec((1, D_MODEL), lambda t: (0, 0)))
        args.append(final_g.reshape(1, D_MODEL))
    out_specs = [row_p, row_s] if last else [row]
    out_shape = ([jax.ShapeDtypeStruct((t_p, D_MODEL), F32), jax.ShapeDtypeStruct((t_tok - t_p, D_MODEL), F32)]
                 if last else [jax.ShapeDtypeStruct((t_tok, D_MODEL), F32)])
    if cast:
        wg_n, wu_n, wd_n, i_n, which_n = w_next
        r_in, r_dn = _cast_rows(D_MODEL, n_steps), _cast_rows(D_FF, n_steps)
        blk_in = lambda t: jnp.minimum(t, D_MODEL // r_in - 1)
        blk_dn = lambda t: jnp.minimum(t, D_FF // r_dn - 1)
        in_specs += [pl.BlockSpec((None, None, r_in, D_FF), lambda t: (i_n, which_n, blk_in(t), 0)),
                     pl.BlockSpec((None, None, r_in, D_FF), lambda t: (i_n, which_n, blk_in(t), 0)),
                     pl.BlockSpec((None, None, r_dn, D_MODEL), lambda t: (i_n, which_n, blk_dn(t), 0))]
        args += [wg_n, wu_n, wd_n]
        out_specs += [pl.BlockSpec((r_in, D_FF), lambda t: (blk_in(t), 0)),
                      pl.BlockSpec((r_in, D_FF), lambda t: (blk_in(t), 0)),
                      pl.BlockSpec((r_dn, D_MODEL), lambda t: (blk_dn(t), 0))]
        out_shape += [jax.ShapeDtypeStruct((D_MODEL, D_FF), BF16), jax.ShapeDtypeStruct((D_MODEL, D_FF), BF16),
                      jax.ShapeDtypeStruct((D_FF, D_MODEL), BF16)]
    scratch = [pltpu.VMEM((tm, D_MODEL), BF16), pltpu.VMEM((tm, D_FF), BF16)]
    if first:
        scratch.insert(0, pltpu.VMEM((tm, D_MODEL), F32))
    outs = pl.pallas_call(
        functools.partial(_ffn_kernel, cfg=cfg, tm=tm, first=first, last=last, cast=cast),
        grid=(n_steps,),
        in_specs=in_specs,
        out_specs=out_specs,
        out_shape=out_shape,
        scratch_shapes=scratch,
        compiler_params=_params(("arbitrary",)),
        name="ffn",
    )(*args)
    n_act = 2 if last else 1
    return outs[:n_act], (tuple(outs[n_act:]) if cast else None)


def _norm_mod(x_ref, sh_ref, sc_ref, g_ref, cfg):
    r = _mod_row(pl.program_id(0), TM_PROJ, cfg)
    hn = _rms(x_ref[...], g_ref[...]) * (1.0 + sc_ref[pl.ds(r, 1), :]) + sh_ref[pl.ds(r, 1), :]
    return hn.astype(BF16)


def _even_in_sub(s, xm_ref, xp_ref, xn_ref, sh_ref, sc_ref, g_ref, w_ref, wp_ref, wdt_ref, cw_ref, cb_ref,
                 z_ref, xst_ref, bc_ref, pool_ref, dt_ref, hn_ref, ext_ref, cfg):
    tm, q = TM_MIX, SSD_CHUNK
    t = pl.program_id(0) * NSUB + s
    info = _tile_info(t, tm, cfg)
    r = _mod_row(t, tm, cfg)
    rows = slice(s * tm, (s + 1) * tm)
    before, after = _sub_halo(s, xm_ref, xp_ref, xn_ref, HALO_A)
    scale, shift, g = 1.0 + sc_ref[pl.ds(r, 1), :], sh_ref[pl.ds(r, 1), :], g_ref[...]
    hn_ref[s, 0:tm, :] = _rms(xm_ref[rows, :], g) * scale + shift
    hn_ref[s, tm:tm + HALO_A, :] = _rms(before, g) * scale + shift
    hn_ref[s, tm + HALO_A:, :] = _rms(after, g) * scale + shift
    hn = hn_ref[s].astype(BF16)
    o_xbc = D_INNER_A
    base = HALO_A - CONV_A // 2
    zero = jnp.zeros((HALO_A, CONV_COLS), F32)
    others = ([(z_ref.at[rows, c:c + CONV_COLS], w_ref, c) for c in range(0, D_INNER_A, CONV_COLS)]
              + [(pool_ref.at[rows, c:c + CONV_COLS], wp_ref, c) for c in range(0, D_POOL, CONV_COLS)])
    for c0 in range(0, D_XBC, CONV_COLS):
        cols = slice(c0, c0 + CONV_COLS)
        xg = _dot(hn, w_ref[:, o_xbc + c0:o_xbc + c0 + CONV_COLS])
        o_ref, ow_ref, o_col = others[c0 // CONV_COLS]
        o_ref[...] = _dot(hn[0:tm], ow_ref[:, o_col:o_col + CONV_COLS])
        ext_ref[s, 0:HALO_A, cols] = jnp.where(info["is_start"], zero, xg[tm:tm + HALO_A])
        ext_ref[s, HALO_A:HALO_A + tm, cols] = xg[0:tm]
        ext_ref[s, HALO_A + tm:, cols] = jnp.where(info["is_end"], zero, xg[tm + HALO_A:])
        for ck in range(tm // q):
            for j in range(c0 // LANES, (c0 + CONV_COLS) // LANES):
                sl = slice(j * LANES, (j + 1) * LANES)
                r0 = base + ck * q
                acc = cw_ref[0:1, sl] * ext_ref[s, r0:r0 + q, sl]
                for k in range(1, CONV_A):
                    acc = acc + cw_ref[k:k + 1, sl] * ext_ref[s, r0 + k:r0 + k + q, sl]
                y = _silu(acc + cb_ref[:, sl])
                if j < D_INNER_A // LANES:
                    xst_ref[s * (tm // q) + ck, j * LANES:(j + 1) * LANES, :] = y.T
                else:
                    bc_ref[s * tm + ck * q:s * tm + (ck + 1) * q,
                           j * LANES - D_INNER_A:(j + 1) * LANES - D_INNER_A] = y.astype(BF16)
    dt_ref[rows, :] = _dot(hn[0:tm], wdt_ref[...])


def _even_in_kernel(*refs, cfg):
    for s in range(NSUB):
        _even_in_sub(s, *refs, cfg)


def _even_in(x, mod, i, j, norm_g4, w, w_pool, w_dt, conv_w, conv_b, cfg):
    t_tok = x.shape[0]
    tm, q = NSUB * TM_MIX, SSD_CHUNK
    row_block = lambda width: pl.BlockSpec((tm, width), lambda t: (t, 0))
    return pl.pallas_call(
        functools.partial(_even_in_kernel, cfg=cfg),
        grid=(t_tok // tm,),
        in_specs=_halo_specs(D_MODEL, tm, HALO_A, t_tok, lambda t: t) + [
            _mod_spec(i, 3), _mod_spec(i, 4), _norm_g_spec(i, 1),
            _layer_block((D_MODEL, D_INNER_A + D_XBC), j), _layer_block((D_MODEL, D_POOL), j),
            _layer_block((D_MODEL, DT_PAD), j),
            _layer_block((CONV_A, D_XBC), j), _layer_block((1, D_XBC), j),
        ],
        out_specs=[row_block(D_INNER_A),
                   pl.BlockSpec((tm // q, D_INNER_A, q), lambda t: (t, 0, 0)),
                   row_block(2 * D_BC), row_block(D_POOL), row_block(DT_PAD)],
        out_shape=[jax.ShapeDtypeStruct((t_tok, D_INNER_A), F32),
                   jax.ShapeDtypeStruct((t_tok // q, D_INNER_A, q), F32),
                   jax.ShapeDtypeStruct((t_tok, 2 * D_BC), BF16),
                   jax.ShapeDtypeStruct((t_tok, D_POOL), F32),
                   jax.ShapeDtypeStruct((t_tok, DT_PAD), F32)],
        scratch_shapes=[pltpu.VMEM((NSUB, TM_MIX + 2 * HALO_A, D_MODEL), F32),
                        pltpu.VMEM((NSUB, TM_MIX + 2 * HALO_A, D_XBC), F32)],
        compiler_params=_params(("parallel",)),
        name="even_in",
    )(x, x, x, mod, mod, norm_g4, w, w_pool, w_dt, conv_w, conv_b)


def _odd_in_kernel(x_ref, sh_ref, sc_ref, g_ref, w_ref, lg_ref, lb_ref, u_ref, v_ref, glu_ref, vg_ref, *, cfg):
    hn = _norm_mod(x_ref, sh_ref, sc_ref, g_ref, cfg)
    for c in range(0, D_C, PROJ_COLS):
        cols = slice(c, c + PROJ_COLS)
        u_ref[:, cols] = jax.nn.gelu(_dot(hn, w_ref[:, c:c + PROJ_COLS]))
        vg_ref[:, cols] = jax.nn.gelu(_dot(hn, w_ref[:, D_C + c:D_C + c + PROJ_COLS]))
        ga = _dot(hn, w_ref[:, 2 * D_C + c:2 * D_C + c + PROJ_COLS])
        gg = _dot(hn, w_ref[:, 2 * D_C + D_D + c:2 * D_C + D_D + c + PROJ_COLS])
        glu_ref[:, cols] = ga * _sigmoid(gg)
    v_ref[...] = _layer_norm(vg_ref[...], lg_ref[...], lb_ref[...]).astype(BF16)


def _odd_in(x, mod, i, j, norm_g4, w, ln_g, ln_b, cfg):
    t_tok = x.shape[0]
    return pl.pallas_call(
        functools.partial(_odd_in_kernel, cfg=cfg),
        grid=(t_tok // TM_PROJ,),
        in_specs=[
            pl.BlockSpec((TM_PROJ, D_MODEL), lambda t: (t, 0)),
            _mod_spec(i, 3), _mod_spec(i, 4),
            _norm_g_spec(i, 1),
            _layer_block((D_MODEL, 2 * D_C + 2 * D_D), j),
            _layer_block((1, D_C), j), _layer_block((1, D_C), j),
        ],
        out_specs=[pl.BlockSpec((TM_PROJ, D_C), lambda t: (t, 0))] * 3,
        out_shape=[jax.ShapeDtypeStruct((t_tok, D_C), F32),
                   jax.ShapeDtypeStruct((t_tok, D_C), BF16),
                   jax.ShapeDtypeStruct((t_tok, D_D), F32)],
        scratch_shapes=[pltpu.VMEM((TM_PROJ, D_C), F32)],
        compiler_params=_params(("parallel",), n_in=7, cast_inputs=(4,)),
        name="odd_in",
    )(x, mod, mod, norm_g4, w, ln_g, ln_b)


def _dt_terms(dt_ref, dtb_ref, alog_ref, ck, rows):
    q = SSD_CHUNK
    dtt = dt_ref[ck * q:(ck + 1) * q, :].T[rows, :]
    dts = _softplus(dtt + dtb_ref[rows, :])
    return dts, dts * (-jnp.exp(alog_ref[rows, :]))


def _tri(lower_incl):
    ri = lax.broadcasted_iota(jnp.int32, (SSD_CHUNK, SSD_CHUNK), 0)
    ci = lax.broadcasted_iota(jnp.int32, (SSD_CHUNK, SSD_CHUNK), 1)
    return (ri >= ci) if lower_incl else (ri <= ci)


def _cumsum_lanes(v, mask):
    return jnp.dot(v, mask.astype(F32), precision=lax.Precision.HIGHEST, preferred_element_type=F32)


def _group_bc(bc_ref, ck, g):
    rows = slice(ck * SSD_CHUNK, (ck + 1) * SSD_CHUNK)
    b = bc_ref[rows, g * D_STATE:(g + 1) * D_STATE]
    c = bc_ref[rows, D_BC + g * D_STATE:D_BC + (g + 1) * D_STATE]
    return b, c


def _state_update(s_ref, xst_ref, xd_ref, bc_ref, ck, scale, cdec):
    hd = HEAD_DIM_A
    for h in range(N_HEADS_A):
        xd_ref[ck, h * hd:(h + 1) * hd, :] = (xst_ref[ck, h * hd:(h + 1) * hd, :] * scale[h:h + 1, :]).astype(BF16)
    gw = HEADS_PER_GROUP * hd
    for g in range(N_GROUPS_A):
        b, _ = _group_bc(bc_ref, ck, g)
        upd = _dot(xd_ref[ck, g * gw:(g + 1) * gw, :], b)
        for e in range(HEADS_PER_GROUP):
            h = g * HEADS_PER_GROUP + e
            s_ref[h * hd:(h + 1) * hd, :] = (s_ref[h * hd:(h + 1) * hd, :] * cdec[h:h + 1, :]
                                             + upd[e * hd:(e + 1) * hd, :])


def _ssd_fwd_chunk(ck, xst_ref, bc_ref, dt_ref, dtb_ref, alog_ref, dsk_ref, ypt_ref, xd_ref, s_ref):
    q, hd, nh = SSD_CHUNK, HEAD_DIM_A, N_HEADS_A
    dts, dta = _dt_terms(dt_ref, dtb_ref, alog_ref, ck, slice(0, 2 * nh))
    low, upp = _tri(True), _tri(False)
    acs_f = _cumsum_lanes(dta[0:nh], upp)
    rcs_b = _cumsum_lanes(dta[nh:2 * nh], low)
    pad = jnp.zeros((LANES - nh, q), F32)
    src_f = jnp.concatenate([acs_f - jnp.log(dts[0:nh]), pad], axis=0).T
    src_b = jnp.concatenate([rcs_b - jnp.log(dts[nh:2 * nh]), pad], axis=0).T
    e_acs = jnp.exp(acs_f)
    neg_inf = jnp.float32(-jnp.inf)

    gw = HEADS_PER_GROUP * hd
    for g in range(N_GROUPS_A):
        b, c = _group_bc(bc_ref, ck, g)
        sct = _dot_nt(b, c)
        yoff = _dot_nt(s_ref[g * gw:(g + 1) * gw, :].astype(BF16), c)
        for e in range(HEADS_PER_GROUP):
            h = g * HEADS_PER_GROUP + e
            rows = slice(h * hd, (h + 1) * hd)
            seg_f = acs_f[h:h + 1, :] - src_f[:, h:h + 1]
            seg_b = rcs_b[h:h + 1, :] - src_b[:, h:h + 1]
            wt = sct * (jnp.exp(jnp.where(upp, seg_f, neg_inf)) + jnp.exp(jnp.where(low, seg_b, neg_inf)))
            xh = xst_ref[ck, rows, :]
            yd = _dot(xh.astype(BF16), wt.astype(BF16))
            ypt_ref[ck, rows, :] = yd + yoff[e * hd:(e + 1) * hd, :] * e_acs[h:h + 1, :] + dsk_ref[rows, :] * xh

    tot = acs_f[:, q - 1:q]
    scale = dts[0:nh] * jnp.exp(tot - acs_f)
    cdec = jnp.exp(jnp.broadcast_to(tot, (nh, D_STATE)))
    _state_update(s_ref, xst_ref, xd_ref, bc_ref, ck, scale, cdec)


def _scan_init(is_prompt, s0_ref):
    return jnp.where(is_prompt, jnp.zeros(s0_ref.shape, F32), s0_ref[...])


def _ssd_fwd_kernel(xst_ref, bc_ref, dt_ref, dtb_ref, alog_ref, dsk_ref, s0_ref, ypt_ref, sfin_ref, xd_ref, snap_ref,
                    s_ref, *, cfg):
    t = pl.program_id(0)
    cps = TM_MIX // SSD_CHUNK

    @pl.when(t == 0)
    def _():
        s_ref[...] = jnp.zeros_like(s_ref)

    for s in range(NSUB):
        info = _tile_info(t * NSUB + s, TM_MIX, cfg)
        s_ref[...] = jnp.where(info["is_start"], _scan_init(info["is_prompt"], s0_ref), s_ref[...])
        for ck in range(s * cps, (s + 1) * cps):
            _ssd_fwd_chunk(ck, xst_ref, bc_ref, dt_ref, dtb_ref, alog_ref, dsk_ref, ypt_ref, xd_ref, s_ref)
        snap_ref[s] = s_ref[...]

    @pl.when(t * NSUB < cfg.batch * (cfg.seq // TM_MIX))
    def _():
        sfin_ref[...] = snap_ref[...]


def _ssd_bwd_chunk(ck, dt_ref, z_ref, ypt_ref, xst_ref, bc_ref, dtb_ref, alog_ref, yt_ref, y_ref, xd_ref, s_ref):
    q, hd, nh = SSD_CHUNK, HEAD_DIM_A, N_HEADS_A
    dts, dta = _dt_terms(dt_ref, dtb_ref, alog_ref, ck, slice(nh, 2 * nh))
    rcs_b = _cumsum_lanes(dta, _tri(True))
    e_rcs = jnp.exp(rcs_b)

    gw = HEADS_PER_GROUP * hd
    for g in range(N_GROUPS_A):
        _, c = _group_bc(bc_ref, ck, g)
        yoff = _dot_nt(s_ref[g * gw:(g + 1) * gw, :].astype(BF16), c)
        for e in range(HEADS_PER_GROUP):
            h = g * HEADS_PER_GROUP + e
            yt_ref[ck, h * hd:(h + 1) * hd, :] = (ypt_ref[ck, h * hd:(h + 1) * hd, :]
                                                  + yoff[e * hd:(e + 1) * hd, :] * e_rcs[h:h + 1, :])

    rows = slice(ck * q, (ck + 1) * q)
    for j in range(D_INNER_A // LANES):
        sl = slice(j * LANES, (j + 1) * LANES)
        y_ref[rows, sl] = yt_ref[ck, j * LANES:(j + 1) * LANES, :].T * _silu(z_ref[rows, sl])

    tot = rcs_b[:, 0:1]
    scale = dts * jnp.exp(tot - rcs_b)
    cdec = jnp.exp(jnp.broadcast_to(tot, (nh, D_STATE)))
    _state_update(s_ref, xst_ref, xd_ref, bc_ref, ck, scale, cdec)


def _ssd_bwd_kernel(dt_ref, z_ref, ypt_ref, xst_ref, bc_ref, dtb_ref, alog_ref, ng_ref, s0_ref, sf_ref, *rest,
                    cfg, n_steps, layer):
    if layer == 0:
        ya_ref, so_ref, xd_ref, yt_ref, y_ref, snap_ref, s_ref = rest
    else:
        sprev_ref, ya_ref, so_ref, xd_ref, yt_ref, y_ref, snap_ref, s_ref = rest
    t = n_steps - 1 - pl.program_id(0)
    cps = TM_MIX // SSD_CHUNK

    @pl.when(pl.program_id(0) == 0)
    def _():
        s_ref[...] = jnp.zeros_like(s_ref)

    for s in reversed(range(NSUB)):
        info = _tile_info(t * NSUB + s, TM_MIX, cfg)
        s_ref[...] = jnp.where(info["is_end"], _scan_init(info["is_prompt"], s0_ref), s_ref[...])
        for ck in reversed(range(s * cps, (s + 1) * cps)):
            _ssd_bwd_chunk(ck, dt_ref, z_ref, ypt_ref, xst_ref, bc_ref, dtb_ref, alog_ref, yt_ref, y_ref, xd_ref, s_ref)
        snap_ref[s] = s_ref[...]
    ya_ref[...] = _rms(y_ref[...], ng_ref[...]).astype(BF16)

    @pl.when(t * NSUB < cfg.batch * (cfg.seq // TM_MIX))
    def _():
        if layer == 0:
            so_ref[:, 0] = sf_ref[...]
            so_ref[:, 1] = snap_ref[...]
        else:
            so_ref[:, 0] = sprev_ref[...]
            so_ref[:, 1, 0] = sf_ref[...]
            so_ref[:, 1, 1] = snap_ref[...]


def _halo_specs(width, tile, halo, n_rows, chunk_of):
    per = tile // halo
    last = n_rows // halo - 1
    return [
        pl.BlockSpec((tile, width), lambda i: (chunk_of(i), 0)),
        pl.BlockSpec((halo, width), lambda i: (jnp.maximum(chunk_of(i) * per - 1, 0), 0)),
        pl.BlockSpec((halo, width), lambda i: (jnp.minimum((chunk_of(i) + 1) * per, last), 0)),
    ]


def _ssd(xst, bc, dt, z, dtb, alog, dsk_t, norm_g, s0, s_prev, layer, cfg):
    t_tok = bc.shape[0]
    q, tm = SSD_CHUNK, NSUB * TM_MIX
    cpt = tm // q
    n_steps = t_tok // tm
    hp = N_HEADS_A * HEAD_DIM_A
    nst = cfg.dec_seq // tm
    n_pt = (cfg.batch * cfg.seq) // tm

    def s_seq(t):
        return jnp.clip(lax.div(jnp.maximum(t - n_pt, 0), nst), 0, cfg.dec_batch - 1)

    def p_blk(t):
        return jnp.minimum(t, n_pt - 1)

    small = lambda shape: pl.BlockSpec(shape, lambda i: (0,) * len(shape))
    head_rows = _layer_block((2 * N_HEADS_A, q), layer)
    state_scratch = [pltpu.VMEM((NSUB, hp, D_STATE), F32), pltpu.VMEM((hp, D_STATE), F32)]

    ypt, sf = pl.pallas_call(
        functools.partial(_ssd_fwd_kernel, cfg=cfg),
        grid=(n_steps,),
        in_specs=[
            pl.BlockSpec((cpt, hp, q), lambda i: (i, 0, 0)),
            pl.BlockSpec((tm, 2 * D_BC), lambda i: (i, 0)),
            pl.BlockSpec((tm, DT_PAD), lambda i: (i, 0)),
            head_rows, head_rows, _layer_block((hp, q), layer),
            pl.BlockSpec((None, None, None, hp, D_STATE), lambda i: (s_seq(i), layer, 0, 0, 0)),
        ],
        out_specs=[pl.BlockSpec((cpt, hp, q), lambda i: (i, 0, 0)),
                   pl.BlockSpec((NSUB, hp, D_STATE), lambda i: (p_blk(i), 0, 0))],
        out_shape=[jax.ShapeDtypeStruct((t_tok // q, hp, q), F32),
                   jax.ShapeDtypeStruct((cfg.batch, hp, D_STATE), F32)],
        scratch_shapes=[pltpu.VMEM((cpt, hp, q), BF16)] + state_scratch,
        compiler_params=_params(("arbitrary",)),
        name="ssd_fwd",
    )(xst, bc, dt, dtb, alog, dsk_t, s0)

    rev = lambda i: n_steps - 1 - i
    in_specs = [
        pl.BlockSpec((tm, DT_PAD), lambda i: (rev(i), 0)),
        pl.BlockSpec((tm, D_INNER_A), lambda i: (rev(i), 0)),
        pl.BlockSpec((cpt, hp, q), lambda i: (rev(i), 0, 0)),
        pl.BlockSpec((cpt, hp, q), lambda i: (rev(i), 0, 0)),
        pl.BlockSpec((tm, 2 * D_BC), lambda i: (rev(i), 0)),
        head_rows, head_rows, _layer_block((1, D_INNER_A), layer),
        pl.BlockSpec((None, None, None, hp, D_STATE), lambda i: (s_seq(rev(i)), layer, 1, 0, 0)),
        pl.BlockSpec((NSUB, hp, D_STATE), lambda i: (p_blk(rev(i)), 0, 0)),
    ]
    args = [dt, z, ypt, xst, bc, dtb, alog, norm_g, s0, sf]
    if layer == 0:
        so_block, so_shape = (NSUB, 2, hp, D_STATE), (cfg.batch, 2, hp, D_STATE)
        so_index = lambda i: (p_blk(rev(i)), 0, 0, 0)
    else:
        in_specs.append(pl.BlockSpec((NSUB, 2, hp, D_STATE), lambda i: (p_blk(rev(i)), 0, 0, 0)))
        args.append(s_prev)
        so_block, so_shape = (NSUB, 2, 2, hp, D_STATE), (cfg.batch, 2, 2, hp, D_STATE)
        so_index = lambda i: (p_blk(rev(i)), 0, 0, 0, 0)
    ya, so = pl.pallas_call(
        functools.partial(_ssd_bwd_kernel, cfg=cfg, n_steps=n_steps, layer=layer),
        grid=(n_steps,),
        in_specs=in_specs,
        out_specs=[pl.BlockSpec((tm, D_INNER_A), lambda i: (rev(i), 0)), pl.BlockSpec(so_block, so_index)],
        out_shape=[jax.ShapeDtypeStruct((t_tok, D_INNER_A), BF16), jax.ShapeDtypeStruct(so_shape, F32)],
        scratch_shapes=[pltpu.VMEM((cpt, hp, q), BF16), pltpu.VMEM((cpt, hp, q), F32),
                        pltpu.VMEM((tm, D_INNER_A), F32)] + state_scratch,
        compiler_params=_params(("arbitrary",)),
        name="ssd_bwd",
    )(*args)
    return ya, so


def _sub_halo(s, main_ref, prev_ref, next_ref, halo):
    tm = TM_MIX
    before = prev_ref[...] if s == 0 else main_ref[s * tm - halo:s * tm, :]
    after = next_ref[...] if s == NSUB - 1 else main_ref[(s + 1) * tm:(s + 1) * tm + halo, :]
    return before, after


def _even_out_sub(s, ya_ref, pm_ref, pp_ref, pn_ref, x_ref, gt_ref, pw_ref, ps_ref, woa_ref, wop_ref,
                  o_ref, ext_ref, yp_ref, cfg):
    tm = TM_MIX
    t = pl.program_id(0) * NSUB + s
    info = _tile_info(t, tm, cfg)
    r = _mod_row(t, tm, cfg)
    rows = slice(s * tm, (s + 1) * tm)
    before, after = _sub_halo(s, pm_ref, pp_ref, pn_ref, HALO_A)
    zero = jnp.zeros((HALO_A, D_POOL), F32)
    ext_ref[s, 0:HALO_A, :] = jnp.where(info["is_start"], zero, before)
    ext_ref[s, HALO_A:HALO_A + tm, :] = pm_ref[rows, :]
    ext_ref[s, HALO_A + tm:, :] = jnp.where(info["is_end"], zero, after)
    pos = info["pos"] * tm + lax.broadcasted_iota(jnp.int32, (tm, POOL_GROUP_DIM), 0)
    oc = D_MODEL // len(POOL_WINDOWS)
    for gi, win in enumerate(POOL_WINDOWS):
        o_ref[rows, gi * oc:(gi + 1) * oc] = _dot(ya_ref[rows, :], woa_ref[:, gi * oc:(gi + 1) * oc])
        sl = slice(gi * POOL_GROUP_DIM, (gi + 1) * POOL_GROUP_DIM)
        lo = HALO_A - win // 2
        acc = ext_ref[s, lo:lo + tm, sl]
        for j in range(1, win):
            acc = acc + ext_ref[s, lo + j:lo + j + tm, sl]
        cnt = (jnp.minimum(pos - win // 2 + win, info["seq_len"]) - jnp.maximum(pos - win // 2, 0)).astype(F32)
        pooled = acc / cnt - ext_ref[s, HALO_A:HALO_A + tm, sl]
        yp_ref[rows, sl] = (_dot(pooled.astype(BF16), pw_ref[gi]) * ps_ref[:, sl]).astype(BF16)
    out = o_ref[rows, :] + _dot(yp_ref[rows, :], wop_ref[...])
    o_ref[rows, :] = x_ref[rows, :] + gt_ref[pl.ds(r, 1), :] * out


def _even_out_kernel(*refs, cfg):
    for s in range(NSUB):
        _even_out_sub(s, *refs, cfg)


def _even_out(ya, pool_in, x, mod, i, j, pool_w, pool_scale, w_out, cfg):
    t_tok = x.shape[0]
    tm = NSUB * TM_MIX
    small = lambda shape: pl.BlockSpec(shape, lambda t: (0,) * len(shape))
    return pl.pallas_call(
        functools.partial(_even_out_kernel, cfg=cfg),
        grid=(t_tok // tm,),
        in_specs=[pl.BlockSpec((tm, D_INNER_A), lambda t: (t, 0))]
        + _halo_specs(D_POOL, tm, HALO_A, t_tok, lambda t: t) + [
            pl.BlockSpec((tm, D_MODEL), lambda t: (t, 0)),
            _mod_spec(i, 5),
            _layer_block((len(POOL_WINDOWS), POOL_GROUP_DIM, POOL_GROUP_DIM), j), _layer_block((1, D_POOL), j),
            _layer_block((D_INNER_A, D_MODEL), j), _layer_block((D_POOL, D_MODEL), j, D_INNER_A // D_POOL),
        ],
        out_specs=pl.BlockSpec((tm, D_MODEL), lambda t: (t, 0)),
        out_shape=jax.ShapeDtypeStruct((t_tok, D_MODEL), F32),
        scratch_shapes=[pltpu.VMEM((NSUB, TM_MIX + 2 * HALO_A, D_POOL), F32), pltpu.VMEM((tm, D_POOL), BF16)],
        compiler_params=_params(("parallel",), n_in=10, cast_inputs=(6, 8, 9)),
        name="even_out",
    )(ya, pool_in, pool_in, pool_in, x, mod, pool_w, pool_scale, w_out, w_out)


def _odd_out_sub(s, u_ref, v_ref, gm_ref, gp_ref, gn_ref, x_ref, gt_ref, spw_ref, spb_ref, dww_ref, dwb_ref,
                 cg_ref, cb_ref, woc_ref, wod_ref, o_ref, ext_ref, sh_ref, yc_ref, yd_ref, cfg):
    tm = TM_MIX
    t = pl.program_id(0) * NSUB + s
    info = _tile_info(t, tm, cfg)
    r = _mod_row(t, tm, cfg)
    row0 = s * tm
    tile_rows = slice(row0, row0 + tm)
    hc = D_C // N_HEADS_C
    for ck in range(tm // MLP_CHUNK):
        rows = slice(row0 + ck * MLP_CHUNK, row0 + (ck + 1) * MLP_CHUNK)
        for h in range(N_HEADS_C):
            sl = slice(h * hc, (h + 1) * hc)
            sv = _dot(spw_ref[h], v_ref[rows, sl]) + spb_ref[:, sl]
            yc_ref[rows, sl] = (u_ref[rows, sl] * sv).astype(BF16)
    before, after = _sub_halo(s, gm_ref, gp_ref, gn_ref, HALO_D)
    zero = jnp.zeros((HALO_D, D_D), F32)
    ext_ref[s, 0:HALO_D, :] = jnp.where(info["is_start"], zero, before)
    ext_ref[s, HALO_D:HALO_D + tm, :] = gm_ref[tile_rows, :]
    ext_ref[s, HALO_D + tm:, :] = jnp.where(info["is_end"], zero, after)
    span = tm + 2 * HALO_D - SUBLANES
    base = HALO_D - CONV_D // 2
    rb = MLP_CHUNK
    for j in range(D_D // LANES):
        sl = slice(j * LANES, (j + 1) * LANES)
        if (j * LANES) % PROJ_COLS == 0:
            oc = slice(j * LANES, j * LANES + PROJ_COLS)
            o_ref[tile_rows, oc] = _dot(yc_ref[tile_rows, :], woc_ref[:, oc])
        for sh in range(1, SUBLANES):
            sh_ref[sh - 1, :, sl] = ext_ref[s, sh:sh + span, sl]
        for rr in range(tm // rb):
            acc = None
            for k in range(CONV_D):
                q8, sh = divmod(base + k, SUBLANES)
                r0 = q8 * SUBLANES + rr * rb
                tap = ext_ref[s, r0:r0 + rb, sl] if sh == 0 else sh_ref[sh - 1, r0:r0 + rb, sl]
                term = dww_ref[k:k + 1, sl] * tap
                acc = term if acc is None else acc + term
            yd_ref[row0 + rr * rb:row0 + (rr + 1) * rb, sl] = acc + dwb_ref[:, sl]
    ydn = _silu(_layer_norm(yd_ref[tile_rows, :], cg_ref[...], cb_ref[...])).astype(BF16)
    out = o_ref[tile_rows, :] + _dot(ydn, wod_ref[...])
    o_ref[tile_rows, :] = x_ref[tile_rows, :] + gt_ref[pl.ds(r, 1), :] * out


def _odd_out_kernel(*refs, cfg):
    for s in range(NSUB):
        _odd_out_sub(s, *refs, cfg)


def _odd_out(u, v, glu, x, mod, i, j, sp_w, sp_b_full, dw_w, dw_b, cn_g, cn_b, w_out, cfg):
    t_tok = x.shape[0]
    tm = NSUB * TM_MIX
    small = lambda shape: pl.BlockSpec(shape, lambda t: (0,) * len(shape))
    return pl.pallas_call(
        functools.partial(_odd_out_kernel, cfg=cfg),
        grid=(t_tok // tm,),
        in_specs=[pl.BlockSpec((tm, D_C), lambda t: (t, 0)), pl.BlockSpec((tm, D_C), lambda t: (t, 0))]
        + _halo_specs(D_D, tm, HALO_D, t_tok, lambda t: t) + [
            pl.BlockSpec((tm, D_MODEL), lambda t: (t, 0)),
            _mod_spec(i, 5),
            _layer_block((N_HEADS_C, MLP_CHUNK, MLP_CHUNK), j), _layer_block((MLP_CHUNK, D_C), j),
            _layer_block((CONV_D, D_D), j), _layer_block((1, D_D), j), _layer_block((1, D_D), j),
            _layer_block((1, D_D), j),
            _layer_block((D_C, D_MODEL), j), _layer_block((D_D, D_MODEL), j, 1),
        ],
        out_specs=pl.BlockSpec((tm, D_MODEL), lambda t: (t, 0)),
        out_shape=jax.ShapeDtypeStruct((t_tok, D_MODEL), F32),
        scratch_shapes=[pltpu.VMEM((NSUB, TM_MIX + 2 * HALO_D, D_D), F32),
                        pltpu.VMEM((SUBLANES - 1, TM_MIX + 2 * HALO_D - SUBLANES, D_D), F32),
                        pltpu.VMEM((tm, D_C), BF16), pltpu.VMEM((tm, D_D), F32)],
        compiler_params=_params(("parallel",), n_in=15, cast_inputs=(7, 13, 14)),
        name="odd_out",
    )(u, v, glu, glu, glu, x, mod, sp_w, sp_b_full, dw_w, dw_b, cn_g, cn_b, w_out, w_out)


def _position_tables(dec_seq):
    quarter = D_MODEL // 4
    freqs = jnp.exp(-math.log(POS_BASE) * jnp.arange(quarter, dtype=F32) / quarter)
    ang_r = jnp.arange(dec_seq // GRID_W, dtype=F32)[:, None] * freqs
    ang_c = jnp.arange(GRID_W, dtype=F32)[:, None] * freqs
    return (jnp.concatenate([jnp.sin(ang_r), jnp.cos(ang_r)], axis=-1),
            jnp.concatenate([jnp.sin(ang_c), jnp.cos(ang_c)], axis=-1))


def _forward(cfg, x_prompt, x_sample, c, state_ssd, c_ctx, w_mod, b_mod, norm_g, ffn_w_gate, ffn_w_up,
             ffn_w_down, ev_w_in, ev_conv_w, ev_conv_b, ev_dt_bias, ev_a_log, ev_d_skip, ev_ssd_norm_g,
             ev_pool_w, ev_pool_scale, ev_w_out, od_w_in, od_v_ln_g, od_v_ln_b, od_sp_w, od_sp_b, od_dw_w,
             od_dw_b, od_cn_g, od_cn_b, od_w_out, final_norm_g):
    assert DEPTH == 4 and cfg.seq % TM_MIX == 0 and cfg.dec_seq % TM_FFN == 0 and (cfg.batch * cfg.seq) % TM_FFN == 0
    t_p = cfg.batch * cfg.seq
    hp = N_HEADS_A * HEAD_DIM_A
    row_tab, col_tab = _position_tables(cfg.dec_seq)

    cond8 = jnp.concatenate([c_ctx[None, :], c, jnp.zeros((SUBLANES - 1 - cfg.dec_batch, D_MODEL), F32)], axis=0)
    mod = _modulation(cond8, w_mod, b_mod)

    w_ffn = tuple(w[0, 0].astype(BF16) for w in (ffn_w_gate, ffn_w_up, ffn_w_down))
    f32_ffn = (ffn_w_gate, ffn_w_up, ffn_w_down)
    norm_g4 = norm_g.reshape(DEPTH, 3, 1, D_MODEL)
    n_odd = od_w_in.shape[0]
    lane_b = lambda v: jnp.broadcast_to(v.reshape(N_EVEN, 2 * N_HEADS_A, 1), (N_EVEN, 2 * N_HEADS_A, SSD_CHUNK))
    w_in = ev_w_in.astype(BF16)
    o_dt = D_INNER_A + D_XBC
    o_pool = o_dt + 2 * N_HEADS_A
    ev = dict(
        w_in=w_in, w_pool=w_in[:, :, o_pool:],
        w_dt=jnp.pad(w_in[:, :, o_dt:o_pool], ((0, 0), (0, 0), (0, DT_PAD - 2 * N_HEADS_A))),
        conv_b=ev_conv_b.reshape(N_EVEN, 1, D_XBC),
        dt_bias=lane_b(ev_dt_bias), a_log=lane_b(ev_a_log),
        d_skip=jnp.broadcast_to(ev_d_skip[:, :, None, None],
                                (N_EVEN, N_HEADS_A, HEAD_DIM_A, SSD_CHUNK)).reshape(N_EVEN, hp, SSD_CHUNK),
        norm_g=ev_ssd_norm_g.reshape(N_EVEN, 1, D_INNER_A),
        pool_w=ev_pool_w.astype(BF16), pool_scale=ev_pool_scale.reshape(N_EVEN, 1, D_POOL),
        w_out=ev_w_out.astype(BF16),
    )
    od = dict(
        w_in=od_w_in.astype(BF16), ln_g=od_v_ln_g.reshape(n_odd, 1, D_C), ln_b=od_v_ln_b.reshape(n_odd, 1, D_C),
        sp_w=od_sp_w.astype(BF16),
        sp_b=jnp.broadcast_to(jnp.swapaxes(od_sp_b, 1, 2)[:, :, :, None],
                              (n_odd, MLP_CHUNK, N_HEADS_C, D_C // N_HEADS_C)).reshape(n_odd, MLP_CHUNK, D_C),
        dw_b=od_dw_b.reshape(n_odd, 1, D_D), cn_g=od_cn_g.reshape(n_odd, 1, D_D), cn_b=od_cn_b.reshape(n_odd, 1, D_D),
        w_out=od_w_out.astype(BF16),
    )
    s0 = state_ssd.reshape(cfg.dec_batch, N_EVEN, 2, hp, D_STATE)
    states = None
    for i in range(DEPTH):
        j = i // 2
        if i == 0:
            (x,), w_ffn = _ffn((x_prompt.reshape(t_p, D_MODEL), x_sample.reshape(-1, D_MODEL)), mod, i, 0, norm_g4,
                               w_ffn, f32_ffn + (i, 1), cfg, pos_tabs=(row_tab, col_tab))
        else:
            (x,), w_ffn = _ffn(x, mod, i, 0, norm_g4, w_ffn, f32_ffn + (i, 1), cfg)
        if i % 2 == 0:
            z, xst, bc, pool_in, dt = _even_in(x, mod, i, j, norm_g4, ev["w_in"], ev["w_pool"], ev["w_dt"], ev_conv_w,
                                               ev["conv_b"], cfg)
            ya, states = _ssd(xst, bc, dt, z, ev["dt_bias"], ev["a_log"], ev["d_skip"], ev["norm_g"], s0, states, j, cfg)
            x = _even_out(ya, pool_in, x, mod, i, j, ev["pool_w"], ev["pool_scale"], ev["w_out"], cfg)
        else:
            u, v, glu = _odd_in(x, mod, i, j, norm_g4, od["w_in"], od["ln_g"], od["ln_b"], cfg)
            x = _odd_out(u, v, glu, x, mod, i, j, od["sp_w"], od["sp_b"], od_dw_w, od["dw_b"], od["cn_g"], od["cn_b"],
                         od["w_out"], cfg)
        if i < DEPTH - 1:
            (x,), w_ffn = _ffn(x, mod, i, 1, norm_g4, w_ffn, f32_ffn + (i + 1, 0), cfg)
        else:
            (y_p, y_s), _ = _ffn(x, mod, i, 1, norm_g4, w_ffn, None, cfg, final_g=final_norm_g)

    return (y_p.reshape(cfg.batch, cfg.seq, D_MODEL), y_s.reshape(cfg.dec_batch, cfg.dec_seq, D_MODEL),
            states.reshape(cfg.batch, N_EVEN, 2, N_HEADS_A, HEAD_DIM_A, D_STATE))


def kernel(x_prompt, x_sample, c, state_ssd, c_ctx, w_mod, b_mod, norm_g, ffn_w_gate, ffn_w_up, ffn_w_down, ev_w_in, ev_conv_w, ev_conv_b, ev_dt_bias, ev_a_log, ev_d_skip, ev_ssd_norm_g, ev_pool_w, ev_pool_scale, ev_w_out, od_w_in, od_v_ln_g, od_v_ln_b, od_sp_w, od_sp_b, od_dw_w, od_dw_b, od_cn_g, od_cn_b, od_w_out, final_norm_g):
    cfg = Cfg(x_prompt.shape[0], x_prompt.shape[1], x_sample.shape[0], x_sample.shape[1])
    return _forward(cfg, x_prompt, x_sample, c, state_ssd, c_ctx, w_mod, b_mod, norm_g, ffn_w_gate, ffn_w_up,
                    ffn_w_down, ev_w_in, ev_conv_w, ev_conv_b, ev_dt_bias, ev_a_log, ev_d_skip, ev_ssd_norm_g,
                    ev_pool_w, ev_pool_scale, ev_w_out, od_w_in, od_v_ln_g, od_v_ln_b, od_sp_w, od_sp_b, od_dw_w,
                    od_dw_b, od_cn_g, od_cn_b, od_w_out, final_norm_g)
```
